```python
import math
import jax
import jax.numpy as jnp
from jax import lax
import numpy as np

D_MODEL = 1024
BATCH = 1
SEQ = 16384
DEPTH = 2
DEC_BATCH = 32
DEC_SEQ = 1
PAST_LEN = 16384
PAGE_SIZE = 128

EPS = 1e-6
PLE_DIM = 256
D_FF = ((8 * D_MODEL // 3 + 127) // 128) * 128
CONV_DIM = D_MODEL // 2
CONV_W = 3
RET_HEADS = 4
RET_DK = 128
RET_DV = 128
RET_DIM = RET_HEADS * RET_DK
RET_VDIM = RET_HEADS * RET_DV
RET_CHUNK = 128
RET_THETA = 10000.0
ATTN_GROUPS = ((128, 1), (512, 4), (2048, 16))
ATTN_HEADS = 4
HEAD_DIM = 64
ATTN_DIM = ATTN_HEADS * HEAD_DIM
ATTN_SCALE = HEAD_DIM ** -0.5
ROT_DIM = HEAD_DIM // 4
ROPE_THETA = 500000.0
ATTN_BLOCK = 128
NEG = -1e30
IN_SIZES = [CONV_DIM] * 3 + [RET_DIM, RET_DIM, RET_VDIM, RET_VDIM] + [ATTN_DIM] * (3 * len(ATTN_GROUPS))
N_IN = sum(IN_SIZES)
N_BRANCH = 3

kernel_name = 'hybrid_conv_retention_dilated_attn_decoder_step'


def rms_norm(x, g):
    xf = x.astype(jnp.float32)
    y = xf * lax.rsqrt(jnp.mean(xf * xf, axis=-1, keepdims=True) + EPS)
    return (y * g.astype(jnp.float32)).astype(x.dtype)


def swiglu(x, w_gu, w_down):
    gate, up = jnp.split(x @ w_gu, 2, axis=-1)
    return (jax.nn.silu(gate) * up) @ w_down


def rotary(x, pos, rot_dim, theta):
    half = rot_dim // 2
    inv_freq = jnp.exp(-jnp.arange(half, dtype=jnp.float32) * (math.log(theta) / half))
    ang = pos.astype(jnp.float32)[:, None] * inv_freq[None, :]
    cos = jnp.cos(ang)[:, None, :]
    sin = jnp.sin(ang)[:, None, :]
    xr = x[..., :rot_dim].astype(jnp.float32)
    x1, x2 = xr[..., :half], xr[..., half:]
    rot = jnp.concatenate([x1 * cos - x2 * sin, x2 * cos + x1 * sin], axis=-1).astype(x.dtype)
    if rot_dim == x.shape[-1]:
        return rot
    return jnp.concatenate([rot, x[..., rot_dim:]], axis=-1)


def short_conv(b_gate, c_gate, h, hist, w_conv, b_conv):
    u = c_gate * h
    t = u.shape[1]
    ext = jnp.concatenate([hist.astype(u.dtype), u], axis=1)
    y = b_conv + w_conv[0] * ext[:, 0:t]
    for j in range(1, CONV_W):
        y = y + w_conv[j] * ext[:, j:j + t]
    return b_gate * y, ext[:, -(CONV_W - 1):]


def retention(q, k, v, s0, chunk):
    f32 = jnp.float32
    b, t, nh, dk = q.shape
    dv = v.shape[-1]
    n = t // chunk
    log_g = jnp.log1p(-jnp.exp2(-5.0 - jnp.arange(nh, dtype=f32)))
    qc = q.astype(f32).reshape(b, n, chunk, nh, dk)
    kc = k.astype(f32).reshape(b, n, chunk, nh, dk)
    vc = v.astype(f32).reshape(b, n, chunk, nh, dv)
    i = jnp.arange(chunk, dtype=f32)
    diff = i[:, None] - i[None, :]
    decay = jnp.where(diff[None] >= 0, jnp.exp(jnp.maximum(diff, 0.0)[None] * log_g[:, None, None]), 0.0)
    scores = jnp.einsum('bnihd,bnjhd->bnhij', qc, kc) * decay
    o_intra = jnp.einsum('bnhij,bnjhe->bnihe', scores, vc)
    k_w = jnp.exp((chunk - 1 - i)[:, None] * log_g[None, :])
    kv = jnp.einsum('bnjhd,jh,bnjhe->nbhde', kc, k_w, vc)
    g_chunk = jnp.exp(chunk * log_g)[:, None, None]

    def step(s, kv_c):
        return s * g_chunk + kv_c, s

    s_final, s_before = lax.scan(step, s0.astype(f32), kv)
    q_w = jnp.exp((i + 1)[:, None] * log_g[None, :])
    o_inter = jnp.einsum('bnihd,ih,nbhde->bnihe', qc, q_w, s_before)
    return (o_intra + o_inter).reshape(b, t, nh, dv), s_final


def head_norm(o):
    mu = jnp.mean(o, axis=-1, keepdims=True)
    var = jnp.mean(jnp.square(o - mu), axis=-1, keepdims=True)
    return (o - mu) * lax.rsqrt(var + EPS)


def dilated_attn_prompt(q, k, v, dil, n_back):
    f32 = jnp.float32
    b, s, nh, dh = q.shape
    L = -(-s // dil)
    qb = ATTN_BLOCK if L > ATTN_BLOCK else L
    nb = -(-L // qb)

    def to_blocks(x):
        x = jnp.pad(x, ((0, 0), (0, L * dil - s), (0, 0), (0, 0)))
        x = jnp.moveaxis(x.reshape(b, L, dil, nh, dh), 2, 1).reshape(b * dil, L, nh, dh)
        x = jnp.pad(x, ((0, 0), (0, nb * qb - L), (0, 0), (0, 0)))
        return x.reshape(b * dil, nb, qb, nh, dh).astype(f32)

    def with_prev(x):
        prev = jnp.pad(x, ((0, 0), (1, 0), (0, 0), (0, 0), (0, 0)))[:, :-1]
        return jnp.concatenate([prev, x], axis=2)

    def from_blocks(x):
        tail = x.shape[3:]
        x = x.reshape((b * dil, nb * qb) + tail)[:, :L]
        x = jnp.moveaxis(x.reshape((b, dil, L) + tail), 1, 2).reshape((b, L * dil) + tail)
        return x[:, :s]

    qs = to_blocks(q)
    kk = with_prev(to_blocks(k))
    vv = with_prev(to_blocks(v))
    qi = jnp.arange(nb)[:, None] * qb + jnp.arange(qb)[None, :]
    kj = qi[:, :1] - qb + jnp.arange(2 * qb)[None, :]
    dist = qi[:, :, None] - kj[:, None, :]
    valid = (dist >= 0) & (dist <= n_back) & (kj[:, None, :] >= 0)
    sc = jnp.einsum('nbihd,nbjhd->nbhij', qs, kk) * ATTN_SCALE
    sc = jnp.where(valid[None, :, None], sc, NEG)
    lse = jax.nn.logsumexp(sc, axis=-1)
    o = jnp.einsum('nbhij,nbjhd->nbihd', jnp.exp(sc - lse[..., None]), vv)
    return from_blocks(o), from_blocks(jnp.swapaxes(lse, 2, 3))


def dilated_attn_decode(q, k, v, hist, dil, n_back):
    f32 = jnp.float32
    w = hist.shape[1]
    t = q.shape[1]
    k_all = jnp.concatenate([hist[:, :, 0].astype(k.dtype), k], axis=1)
    v_all = jnp.concatenate([hist[:, :, 1].astype(v.dtype), v], axis=1)
    idx = (w + jnp.arange(t))[:, None] - dil * jnp.arange(n_back + 1)[None, :]
    valid = idx >= 0
    idx = jnp.maximum(idx, 0)
    kg = jnp.take(k_all, idx, axis=1).astype(f32)
    vg = jnp.take(v_all, idx, axis=1).astype(f32)
    sc = jnp.einsum('bthd,btnhd->bthn', q.astype(f32), kg) * ATTN_SCALE
    sc = jnp.where(valid[None, :, None, :], sc, NEG)
    lse = jax.nn.logsumexp(sc, axis=-1)
    o = jnp.einsum('bthn,btnhd->bthd', jnp.exp(sc - lse[..., None]), vg)
    new_hist = jnp.stack([k_all[:, -w:], v_all[:, -w:]], axis=2)
    return o, lse, new_hist


def decoder_layer(x, ple, pos, conv_hist, ret_state, kv_hist, lw):
    g = lw['norm']
    bsz, t, _ = x.shape
    x = x + 0.5 * rms_norm(swiglu(rms_norm(x, g[0]), lw['ffn1_gu'], lw['ffn1_down']), g[1])

    h = rms_norm(x, g[2])
    offs = np.cumsum(IN_SIZES)[:-1].tolist()
    parts = jnp.split(h @ lw['w_in'], offs, axis=-1)

    def heads(a, d):
        return a.reshape(bsz, t, -1, d)

    conv_y, conv_state = short_conv(parts[0], parts[1], parts[2], conv_hist, lw['w_conv'], lw['b_conv'])

    rq = rotary(heads(parts[3], RET_DK), pos, RET_DK, RET_THETA)
    rk = rotary(heads(parts[4], RET_DK), pos, RET_DK, RET_THETA) * (RET_DK ** -0.5)
    chunk = RET_CHUNK if t % RET_CHUNK == 0 else t
    ro, ret_state = retention(rq, rk, heads(parts[5], RET_DV), ret_state, chunk)
    ret_y = head_norm(ro).reshape(bsz, t, RET_VDIM).astype(x.dtype) * jax.nn.silu(parts[6])

    outs, lses, new_kv = [], [], []
    for gi, (window, dil) in enumerate(ATTN_GROUPS):
        base = 7 + 3 * gi
        aq = rotary(heads(parts[base], HEAD_DIM), pos, ROT_DIM, ROPE_THETA)
        ak = rotary(heads(parts[base + 1], HEAD_DIM), pos, ROT_DIM, ROPE_THETA)
        av = heads(parts[base + 2], HEAD_DIM)
        if kv_hist is None:
            o, lse = dilated_attn_prompt(aq, ak, av, dil, window // dil)
            keep = min(window, t)
            kv = jnp.stack([ak[:, -keep:], av[:, -keep:]], axis=2)
        else:
            o, lse, kv = dilated_attn_decode(aq, ak, av, kv_hist[gi], dil, window // dil)
        outs.append(o)
        lses.append(lse)
        new_kv.append(kv)
    wts = jax.nn.softmax(jnp.stack(lses), axis=0)
    attn_o = jnp.sum(wts[..., None] * jnp.stack(outs), axis=0)
    attn_y = attn_o.reshape(bsz, t, ATTN_DIM).astype(x.dtype)

    gates = jax.nn.sigmoid((h @ lw['w_gate']).astype(jnp.float32)).astype(x.dtype)
    g_conv, g_ret, g_attn = jnp.split(gates, N_BRANCH, axis=-1)
    merged = (g_conv * (conv_y @ lw['w_conv_out']) + g_ret * (ret_y @ lw['w_ret_out'])
              + g_attn * (attn_y @ lw['w_attn_out']))
    x = x + rms_norm(merged @ lw['w_o'], g[3])

    x = x + 0.5 * rms_norm(swiglu(rms_norm(x, g[4]), lw['ffn2_gu'], lw['ffn2_down']), g[5])

    gate = jax.nn.sigmoid((rms_norm(x, g[6]) @ lw['w_ple_gate']).astype(jnp.float32)).astype(x.dtype)
    x = x + rms_norm(gate * (ple @ lw['w_ple_proj']), g[7])
    return x, conv_state, ret_state, new_kv


def setup_inputs(seed: int = 0) -> dict:
    key = jax.random.key(seed)
    keys = iter(jax.random.split(key, 32))

    def nrm(shape, scale):
        return jax.random.normal(next(keys), shape, jnp.float32) * scale

    bufs = [min(w, PAST_LEN) for w, _ in ATTN_GROUPS]
    return {
        'x_prompt': nrm((BATCH, SEQ, D_MODEL), 1.0),
        'x_sample': nrm((DEC_BATCH, DEC_SEQ, D_MODEL), 1.0),
        'state_conv': nrm((DEPTH, DEC_BATCH, CONV_W - 1, CONV_DIM), 1.0),
        'state_ret': nrm((DEPTH, DEC_BATCH, RET_HEADS, RET_DK, RET_DV), 1.0),
        'cache_kv_w128': nrm((DEPTH, DEC_BATCH, bufs[0], 2, ATTN_HEADS, HEAD_DIM), 1.0),
        'cache_kv_w512': nrm((DEPTH, DEC_BATCH, bufs[1], 2, ATTN_HEADS, HEAD_DIM), 1.0),
        'cache_kv_w2048': nrm((DEPTH, DEC_BATCH, bufs[2], 2, ATTN_HEADS, HEAD_DIM), 1.0),
        'p_prompt': nrm((DEPTH, BATCH, SEQ, PLE_DIM), 1.0),
        'p_sample': nrm((DEPTH, DEC_BATCH, DEC_SEQ, PLE_DIM), 1.0),
        'norm_gain': 1.0 + nrm((DEPTH, 8, D_MODEL), 0.02),
        'w_ffn1_gu': nrm((DEPTH, D_MODEL, 2 * D_FF), D_MODEL ** -0.5),
        'w_ffn1_down': nrm((DEPTH, D_FF, D_MODEL), D_FF ** -0.5),
        'w_in': nrm((DEPTH, D_MODEL, N_IN), D_MODEL ** -0.5),
        'w_conv': nrm((DEPTH, CONV_W, CONV_DIM), CONV_W ** -0.5),
        'b_conv': nrm((DEPTH, CONV_DIM), 0.02),
        'w_conv_out': nrm((DEPTH, CONV_DIM, D_MODEL), CONV_DIM ** -0.5),
        'w_ret_out': nrm((DEPTH, RET_VDIM, D_MODEL), RET_VDIM ** -0.5),
        'w_attn_out': nrm((DEPTH, ATTN_DIM, D_MODEL), ATTN_DIM ** -0.5),
        'w_gate': nrm((DEPTH, D_MODEL, N_BRANCH * D_MODEL), D_MODEL ** -0.5),
        'w_o': nrm((DEPTH, D_MODEL, D_MODEL), D_MODEL ** -0.5),
        'w_ffn2_gu': nrm((DEPTH, D_MODEL, 2 * D_FF), D_MODEL ** -0.5),
        'w_ffn2_down': nrm((DEPTH, D_FF, D_MODEL), D_FF ** -0.5),
        'w_ple_gate': nrm((DEPTH, D_MODEL, D_MODEL), D_MODEL ** -0.5),
        'w_ple_proj': nrm((DEPTH, PLE_DIM, D_MODEL), PLE_DIM ** -0.5),
    }


def reference(x_prompt, x_sample, state_conv, state_ret, cache_kv_w128, cache_kv_w512, cache_kv_w2048,
              p_prompt, p_sample, norm_gain, w_ffn1_gu, w_ffn1_down, w_in, w_conv, b_conv,
              w_conv_out, w_ret_out, w_attn_out, w_gate, w_o, w_ffn2_gu, w_ffn2_down,
              w_ple_gate, w_ple_proj):
    pos_p = jnp.arange(x_prompt.shape[1])
    pos_s = PAST_LEN + jnp.arange(x_sample.shape[1])
    conv0 = jnp.zeros((x_prompt.shape[0], CONV_W - 1, CONV_DIM), x_prompt.dtype)
    ret0 = jnp.zeros((x_prompt.shape[0], RET_HEADS, RET_DK, RET_DV), jnp.float32)
    yp, ys = x_prompt, x_sample
    conv_p, conv_s, ret_p, ret_s = [], [], [], []
    kv_p = [[] for _ in ATTN_GROUPS]
    kv_s = [[] for _ in ATTN_GROUPS]
    for l in range(DEPTH):
        lw = {
            'norm': norm_gain[l], 'ffn1_gu': w_ffn1_gu[l], 'ffn1_down': w_ffn1_down[l],
            'w_in': w_in[l], 'w_conv': w_conv[l], 'b_conv': b_conv[l],
            'w_conv_out': w_conv_out[l], 'w_ret_out': w_ret_out[l], 'w_attn_out': w_attn_out[l],
            'w_gate': w_gate[l], 'w_o': w_o[l], 'ffn2_gu': w_ffn2_gu[l], 'ffn2_down': w_ffn2_down[l],
            'w_ple_gate': w_ple_gate[l], 'w_ple_proj': w_ple_proj[l],
        }
        yp, cp, rp, kp = decoder_layer(yp, p_prompt[l], pos_p, conv0, ret0, None, lw)
        ys, cs, rs, ks = decoder_layer(ys, p_sample[l], pos_s, state_conv[l], state_ret[l],
                                       (cache_kv_w128[l], cache_kv_w512[l], cache_kv_w2048[l]), lw)
        conv_p.append(cp)
        conv_s.append(cs)
        ret_p.append(rp)
        ret_s.append(rs)
        for gi in range(len(ATTN_GROUPS)):
            kv_p[gi].append(kp[gi])
            kv_s[gi].append(ks[gi])
    return (yp, ys, jnp.stack(conv_p), jnp.stack(conv_s), jnp.stack(ret_p), jnp.stack(ret_s),
            jnp.stack(kv_p[0]), jnp.stack(kv_s[0]), jnp.stack(kv_p[1]), jnp.stack(kv_s[1]),
            jnp.stack(kv_p[2]), jnp.stack(kv_s[2]))
```

```python
import functools
import math

import jax
import jax.numpy as jnp
from jax import lax
from jax.experimental import pallas as pl
from jax.experimental.pallas import tpu as pltpu

F32 = jnp.float32
BF16 = jnp.bfloat16

EPS = 1e-6
PAST_LEN = 16384
D_MODEL = 1024
PLE_DIM = 256
CONV_DIM = 512
RET_HEADS = 4
RET_DK = 128
RET_CHUNK = 128
RET_THETA = 10000.0
ATTN_GROUPS = ((128, 1), (512, 4), (2048, 16))
ATTN_HEADS = 4
HEAD_DIM = 64
ATTN_DIM = ATTN_HEADS * HEAD_DIM
ATTN_SCALE = HEAD_DIM ** -0.5
ROT_DIM = 16
ROPE_THETA = 500000.0
ATTN_BLOCK = 128
NEG = -1e30
OFF_CONV = 0
OFF_RET = 3 * CONV_DIM
OFF_ATTN = OFF_RET + 4 * 512

VMEM_LIMIT_BYTES = 60000 * 1024
ROW_TILE = 512


def _params(*sem):
    return pltpu.CompilerParams(dimension_semantics=sem, vmem_limit_bytes=VMEM_LIMIT_BYTES)


def _const_spec(shape):
    zeros = (0,) * len(shape)
    return pl.BlockSpec(shape, lambda *_: zeros, pipeline_mode=pl.Buffered(1))


def _row_spec(tm, width):
    return pl.BlockSpec((tm, width), lambda i: (i, 0))


def _rms(x, g):
    return x * lax.rsqrt(jnp.mean(x * x, axis=-1, keepdims=True) + EPS) * g


def _dot(a, b):
    return jnp.dot(a, b, preferred_element_type=F32)


def _dot_nt(a, b):
    return lax.dot_general(a, b, (((1,), (1,)), ((), ())), preferred_element_type=F32)


def _silu(x):
    return x * jax.nn.sigmoid(x)


def _ffn_body(x_ref, g_ref, wgu_ref, wd_ref, o_ref, *, pre, post, d_ff, n_chunk):
    x = x_ref[...]
    h = _rms(x, g_ref[pre:pre + 1, :]).astype(BF16)
    tf = d_ff // n_chunk
    acc = None
    for c in range(n_chunk):
        gate = _dot(h, wgu_ref[:, c * tf:(c + 1) * tf])
        up = _dot(h, wgu_ref[:, d_ff + c * tf:d_ff + (c + 1) * tf])
        a = (_silu(gate) * up).astype(BF16)
        y = _dot(a, wd_ref[c * tf:(c + 1) * tf, :])
        acc = y if acc is None else acc + y
    o_ref[...] = x + 0.5 * _rms(acc, g_ref[post:post + 1, :])


def _ffn(x, gains, wgu, wd, pre, post, tm):
    m, d = x.shape
    d_ff = wd.shape[0]
    body = functools.partial(_ffn_body, pre=pre, post=post, d_ff=d_ff, n_chunk=2)
    return pl.pallas_call(
        body,
        grid=(m // tm,),
        in_specs=[_row_spec(tm, d), _const_spec(gains.shape), _const_spec(wgu.shape), _const_spec(wd.shape)],
        out_specs=_row_spec(tm, d),
        out_shape=jax.ShapeDtypeStruct((m, d), F32),
        compiler_params=_params("parallel"),
        name="ffn",
    )(x, gains, wgu, wd)


def _ple_body(x_ref, p_ref, g_ref, wg_ref, wp_ref, o_ref):
    x = x_ref[...]
    h = _rms(x, g_ref[6:7, :]).astype(BF16)
    gate = jax.nn.sigmoid(_dot(h, wg_ref[...]))
    proj = _dot(p_ref[...].astype(BF16), wp_ref[...])
    o_ref[...] = x + _rms(gate * proj, g_ref[7:8, :])


def _ple(x, p, gains, wg, wp, tm):
    m, d = x.shape
    return pl.pallas_call(
        _ple_body,
        grid=(m // tm,),
        in_specs=[_row_spec(tm, d), _row_spec(tm, p.shape[1]), _const_spec(gains.shape),
                  _const_spec(wg.shape), _const_spec(wp.shape)],
        out_specs=_row_spec(tm, d),
        out_shape=jax.ShapeDtypeStruct((m, d), F32),
        compiler_params=_params("parallel"),
        name="ple",
    )(x, p, gains, wg, wp)


def _rot_ret(x, cos, sin):
    return x * cos + pltpu.roll(x, RET_DK // 2, 1) * sin


def _rot_attn(x, cos, sin_lo, sin_hi):
    return x * cos + pltpu.roll(x, 128 - ROT_DIM // 2, 1) * sin_lo + pltpu.roll(x, ROT_DIM // 2, 1) * sin_hi


def _proj_body(*refs, decode):
    (x_ref, g_ref, w_ref, wc_ref, bc_ref, rc_ref, rs_ref, ac_ref, alo_ref, ahi_ref, hist_ref) = refs[:11]
    outs = refs[11:]
    cy_ref, ulast_ref, rq_ref, rk_ref, rv_ref, rg_ref = outs[:6]
    attn_refs = outs[6:15]
    carry_ref = None if decode else outs[15]

    x = x_ref[...]
    h = _rms(x, g_ref[2:3, :]).astype(BF16)

    def part(off, width):
        return _dot(h, w_ref[:, off:off + width])

    b_gate = part(OFF_CONV, CONV_DIM)
    u = part(OFF_CONV + CONV_DIM, CONV_DIM) * part(OFF_CONV + 2 * CONV_DIM, CONV_DIM)
    if decode:
        u2 = hist_ref[0]
        u1 = hist_ref[1]
        ulast_ref[0] = u1
        ulast_ref[1] = u
    else:
        tm = u.shape[0]

        @pl.when(pl.program_id(0) == 0)
        def _():
            carry_ref[...] = hist_ref[...]

        carry = carry_ref[...]
        row = lax.broadcasted_iota(jnp.int32, u.shape, 0)
        u1 = jnp.where(row == 0, carry[7:8, :], pltpu.roll(u, 1, 0))
        u2 = jnp.where(row == 0, carry[6:7, :], jnp.where(row == 1, carry[7:8, :], pltpu.roll(u, 2, 0)))
        carry_ref[...] = u[tm - 8:tm, :]
        ulast_ref[...] = u[tm - 8:tm, :]
    y = bc_ref[...] + wc_ref[0:1, :] * u2
    y = y + wc_ref[1:2, :] * u1
    y = y + wc_ref[2:3, :] * u
    cy_ref[...] = (b_gate * y).astype(cy_ref.dtype)

    cos, sin = rc_ref[...], rs_ref[...]
    zq = part(OFF_RET, 512)
    zk = part(OFF_RET + 512, 512)
    for hh in range(RET_HEADS):
        cols = slice(hh * RET_DK, (hh + 1) * RET_DK)
        rq_ref[:, cols] = _rot_ret(zq[:, cols], cos, sin)
        rk_ref[:, cols] = _rot_ret(zk[:, cols], cos, sin) * (RET_DK ** -0.5)
    rv_ref[...] = part(OFF_RET + 1024, 512)
    rg_ref[...] = part(OFF_RET + 1536, 512)

    acos, alo, ahi = ac_ref[...], alo_ref[...], ahi_ref[...]
    for gi in range(len(ATTN_GROUPS)):
        base = OFF_ATTN + gi * 3 * ATTN_DIM
        q_ref, k_ref, v_ref = attn_refs[3 * gi:3 * gi + 3]
        zq = part(base, ATTN_DIM)
        zk = part(base + ATTN_DIM, ATTN_DIM)
        for half in range(2):
            cols = slice(half * 128, (half + 1) * 128)
            q_ref[:, cols] = _rot_attn(zq[:, cols], acos, alo, ahi)
            k_ref[:, cols] = _rot_attn(zk[:, cols], acos, alo, ahi)
        v_ref[...] = part(base + 2 * ATTN_DIM, ATTN_DIM)


def _proj(x, gains, w_in, w_conv, b_conv, tabs, hist, tm, decode):
    m, d = x.shape
    rc, rs, ac, alo, ahi = tabs
    n = m // tm
    if decode:
        hist_spec = pl.BlockSpec((2, tm, CONV_DIM), lambda i: (0, i, 0))
        ulast_spec = pl.BlockSpec((2, tm, CONV_DIM), lambda i: (0, i, 0))
        ulast_shape = jax.ShapeDtypeStruct((2, m, CONV_DIM), F32)
    else:
        hist_spec = _const_spec((8, CONV_DIM))
        ulast_spec = pl.BlockSpec((8, CONV_DIM), lambda i: (0, 0))
        ulast_shape = jax.ShapeDtypeStruct((8, CONV_DIM), F32)
    in_specs = [
        _row_spec(tm, d), _const_spec(gains.shape), _const_spec(w_in.shape),
        _const_spec(w_conv.shape), _const_spec(b_conv.shape),
        _row_spec(tm, 128), _row_spec(tm, 128), _row_spec(tm, 128), _row_spec(tm, 128), _row_spec(tm, 128),
        hist_spec,
    ]
    out_specs = [_row_spec(tm, CONV_DIM), ulast_spec] + [_row_spec(tm, 512)] * 4 + [_row_spec(tm, ATTN_DIM)] * 9
    out_shape = ([jax.ShapeDtypeStruct((m, CONV_DIM), BF16), ulast_shape]
                 + [jax.ShapeDtypeStruct((m, 512), F32)] * 4
                 + [jax.ShapeDtypeStruct((m, ATTN_DIM), F32)] * 9)
    scratch = [] if decode else [pltpu.VMEM((8, CONV_DIM), F32)]
    return pl.pallas_call(
        functools.partial(_proj_body, decode=decode),
        grid=(n,),
        in_specs=in_specs,
        out_specs=out_specs,
        out_shape=out_shape,
        scratch_shapes=scratch,
        compiler_params=_params("arbitrary"),
        name="proj",
    )(x, gains, w_in, w_conv, b_conv, rc, rs, ac, alo, ahi, hist)


def _head_norm(o):
    mu = jnp.mean(o, axis=-1, keepdims=True)
    var = jnp.mean(jnp.square(o - mu), axis=-1, keepdims=True)
    return (o - mu) * lax.rsqrt(var + EPS)


def _ret_body(q_ref, k_ref, v_ref, g_ref, decay_ref, qw_ref, kw_ref, gch_ref, y_ref, sout_ref, s_scr, *, n_chunk):
    @pl.when(pl.program_id(0) == 0)
    def _():
        s_scr[...] = jnp.zeros_like(s_scr)

    for c in range(n_chunk):
        rows = slice(c * RET_CHUNK, (c + 1) * RET_CHUNK)
        for hh in range(RET_HEADS):
            cols = slice(hh * RET_DK, (hh + 1) * RET_DK)
            q = q_ref[rows, cols]
            k = k_ref[rows, cols]
            v = v_ref[rows, cols].astype(BF16)
            s = s_scr[hh]
            scores = _dot_nt(q.astype(BF16), k.astype(BF16)) * decay_ref[hh]
            o = _dot(scores.astype(BF16), v) + _dot((q * qw_ref[hh]).astype(BF16), s.astype(BF16))
            kv = _dot(jnp.transpose(k * kw_ref[hh]).astype(BF16), v)
            s_scr[hh] = s * gch_ref[hh] + kv
            y_ref[rows, cols] = (_head_norm(o) * _silu(g_ref[rows, cols])).astype(y_ref.dtype)

    @pl.when(pl.program_id(0) == pl.num_programs(0) - 1)
    def _():
        sout_ref[...] = s_scr[...]


def _ret_tables():
    i = jnp.arange(RET_CHUNK, dtype=F32)
    log_g = jnp.log1p(-jnp.exp2(-5.0 - jnp.arange(RET_HEADS, dtype=F32)))
    diff = i[:, None] - i[None, :]
    decay = jnp.where(diff[None] >= 0, jnp.exp(jnp.maximum(diff, 0.0)[None] * log_g[:, None, None]), 0.0)
    k_w = jnp.exp((RET_CHUNK - 1 - i)[:, None] * log_g[None, :])
    q_w = jnp.exp((i + 1)[:, None] * log_g[None, :])
    g_chunk = jnp.exp(RET_CHUNK * log_g)
    wide = (RET_HEADS, RET_CHUNK, RET_DK)
    qw = jnp.broadcast_to(q_w.T[:, :, None], wide)
    kw = jnp.broadcast_to(k_w.T[:, :, None], wide)
    gch = jnp.broadcast_to(g_chunk[:, None, None], wide)
    return decay, qw, kw, gch, log_g


def _retention(rq, rk, rv, rg, ret_tabs, tr):
    m = rq.shape[0]
    decay, qw, kw, gch = ret_tabs
    tab_spec = _const_spec((RET_HEADS, RET_CHUNK, RET_DK))
    return pl.pallas_call(
        functools.partial(_ret_body, n_chunk=tr // RET_CHUNK),
        grid=(m // tr,),
        in_specs=[_row_spec(tr, 512)] * 4 + [tab_spec] * 4,
        out_specs=[_row_spec(tr, 512), pl.BlockSpec((RET_HEADS, RET_DK, RET_DK), lambda i: (0, 0, 0))],
        out_shape=[jax.ShapeDtypeStruct((m, 512), BF16), jax.ShapeDtypeStruct((RET_HEADS, RET_DK, RET_DK), F32)],
        scratch_shapes=[pltpu.VMEM((RET_HEADS, RET_DK, RET_DK), F32)],
        compiler_params=_params("arbitrary"),
        name="retention",
    )(rq, rk, rv, rg, decay, qw, kw, gch)


def _ret_dec_body(q_ref, k_ref, v_ref, g_ref, gd_ref, s0_ref, y_ref, sout_ref):
    for hh in range(RET_HEADS):
        cols = slice(hh * RET_DK, (hh + 1) * RET_DK)
        q = q_ref[0, :, cols]
        k = k_ref[0, :, cols]
        v = v_ref[0, :, cols]
        gd = gd_ref[hh]
        s0 = s0_ref[0, hh]
        wide = (RET_DK, RET_DK)
        q_col = jnp.transpose(jnp.broadcast_to(q * gd, wide))
        k_col = jnp.transpose(jnp.broadcast_to(k, wide))
        o_inter = jnp.sum(q_col * s0, axis=0, keepdims=True)
        o_intra = jnp.sum(q * k, axis=-1, keepdims=True) * v
        sout_ref[0, hh] = s0 * gd + k_col * v
        o = o_intra + o_inter
        y_ref[0, :, cols] = _head_norm(o) * _silu(g_ref[0, :, cols])


def _retention_decode(rq, rk, rv, rg, gdec, s0):
    b = rq.shape[0]
    row = pl.BlockSpec((1, 1, 512), lambda i: (i, 0, 0))
    st = pl.BlockSpec((1, RET_HEADS, RET_DK, RET_DK), lambda i: (i, 0, 0, 0))
    r3 = lambda a: a.reshape(b, 1, 512)
    y, s_new = pl.pallas_call(
        _ret_dec_body,
        grid=(b,),
        in_specs=[row, row, row, row, _const_spec(gdec.shape), st],
        out_specs=[row, st],
        out_shape=[jax.ShapeDtypeStruct((b, 1, 512), F32), jax.ShapeDtypeStruct(s0.shape, F32)],
        compiler_params=_params("parallel"),
        name="retention_decode",
    )(r3(rq), r3(rk), r3(rv), r3(rg), gdec, s0)
    return y.reshape(b, 512), s_new


def _attn_body(q_ref, kc_ref, vc_ref, kp_ref, vp_ref, o_ref, lse_ref):
    n = pl.program_id(1)
    qb = ATTN_BLOCK
    i = lax.broadcasted_iota(jnp.int32, (qb, qb), 0)
    j = lax.broadcasted_iota(jnp.int32, (qb, qb), 1)
    mask_cur = i >= j
    mask_prev = jnp.logical_and(j >= i, n > 0)
    for hh in range(ATTN_HEADS):
        cols = slice(hh * HEAD_DIM, (hh + 1) * HEAD_DIM)
        q = q_ref[:, cols].astype(BF16)
        s_c = jnp.where(mask_cur, _dot_nt(q, kc_ref[:, cols].astype(BF16)) * ATTN_SCALE, NEG)
        s_p = jnp.where(mask_prev, _dot_nt(q, kp_ref[:, cols].astype(BF16)) * ATTN_SCALE, NEG)
        mx = jnp.maximum(jnp.max(s_c, axis=-1, keepdims=True), jnp.max(s_p, axis=-1, keepdims=True))
        p_c = jnp.exp(s_c - mx)
        p_p = jnp.exp(s_p - mx)
        den = jnp.sum(p_c, axis=-1, keepdims=True) + jnp.sum(p_p, axis=-1, keepdims=True)
        o = _dot(p_c.astype(BF16), vc_ref[:, cols].astype(BF16)) + _dot(p_p.astype(BF16), vp_ref[:, cols].astype(BF16))
        o_ref[:, cols] = o / den
        lse_ref[:, cols] = jnp.broadcast_to(mx + jnp.log(den), (qb, HEAD_DIM))


def _attention(q, k, v, dil):
    s = q.shape[0]
    length = s // dil
    nb = length // ATTN_BLOCK
    view = lambda a: a.reshape(length, dil * ATTN_DIM)
    cur = pl.BlockSpec((ATTN_BLOCK, ATTN_DIM), lambda r, n: (n, r))
    prev = pl.BlockSpec((ATTN_BLOCK, ATTN_DIM), lambda r, n: (jnp.maximum(n - 1, 0), r))
    o, lse = pl.pallas_call(
        _attn_body,
        grid=(dil, nb),
        in_specs=[cur, cur, cur, prev, prev],
        out_specs=[cur, cur],
        out_shape=[jax.ShapeDtypeStruct((length, dil * ATTN_DIM), F32)] * 2,
        compiler_params=_params("parallel", "arbitrary"),
        name="attention",
    )(view(q), view(k), view(v), view(k), view(v))
    return o.reshape(s, ATTN_DIM), lse.reshape(s, ATTN_DIM)


def _attn_dec_body(q_ref, kn_ref, vn_ref, c_ref, o_ref, lse_ref, cout_ref, *, window, dil):
    lane = lax.broadcasted_iota(jnp.int32, (1, window), 1)
    valid = (lane & (dil - 1)) == 0
    q = q_ref[0]
    kn = kn_ref[0]
    vn = vn_ref[0]
    new_rows = jnp.concatenate([kn, vn], axis=1)
    new_cols = jnp.transpose(jnp.broadcast_to(new_rows, (128, 2 * ATTN_DIM)))
    last = lax.broadcasted_iota(jnp.int32, (HEAD_DIM, 128), 1) == 127
    for hh in range(ATTN_HEADS):
        cols = slice(hh * HEAD_DIM, (hh + 1) * HEAD_DIM)
        qh = q[:, cols]
        k_t = c_ref[0, 0, hh]
        v_t = c_ref[0, 1, hh]
        q16 = jnp.broadcast_to(qh, (16, HEAD_DIM)).astype(BF16)
        s = _dot(q16, k_t.astype(BF16))[0:1, :] * ATTN_SCALE
        s = jnp.where(valid, s, NEG)
        s_new = jnp.sum(qh * kn[:, cols], axis=-1, keepdims=True) * ATTN_SCALE
        mx = jnp.maximum(jnp.max(s, axis=-1, keepdims=True), s_new)
        p = jnp.exp(s - mx)
        p_new = jnp.exp(s_new - mx)
        den = jnp.sum(p, axis=-1, keepdims=True) + p_new
        p16 = jnp.broadcast_to(p, (16, window)).astype(BF16)
        o = _dot_nt(p16, v_t.astype(BF16))[0:1, :] + p_new * vn[:, cols]
        o_ref[0, :, cols] = o / den
        lse_ref[0, :, cols] = jnp.broadcast_to(mx + jnp.log(den), (1, HEAD_DIM))
        for kv, t in ((0, k_t), (1, v_t)):
            shifted = pltpu.roll(t, window - 1, 1)
            col = new_cols[kv * ATTN_DIM + hh * HEAD_DIM:kv * ATTN_DIM + (hh + 1) * HEAD_DIM, :]
            if window > 128:
                cout_ref[0, kv, hh, :, 0:window - 128] = shifted[:, 0:window - 128]
            cout_ref[0, kv, hh, :, window - 128:window] = jnp.where(last, col, shifted[:, window - 128:window])


def _attention_decode(q, kn, vn, cache_t, layer, window, dil):
    b = q.shape[0]
    row = pl.BlockSpec((1, 1, ATTN_DIM), lambda i: (i, 0, 0))
    cin = pl.BlockSpec((None, 1, 2, ATTN_HEADS, HEAD_DIM, window), lambda i: (layer, i, 0, 0, 0, 0))
    cout = pl.BlockSpec((1, 2, ATTN_HEADS, HEAD_DIM, window), lambda i: (i, 0, 0, 0, 0))
    r3 = lambda a: a.reshape(b, 1, ATTN_DIM)
    o, lse, cnew = pl.pallas_call(
        functools.partial(_attn_dec_body, window=window, dil=dil),
        grid=(b,),
        in_specs=[row, row, row, cin],
        out_specs=[row, row, cout],
        out_shape=[jax.ShapeDtypeStruct((b, 1, ATTN_DIM), F32)] * 2
        + [jax.ShapeDtypeStruct((b, 2, ATTN_HEADS, HEAD_DIM, window), F32)],
        compiler_params=_params("parallel"),
        name="attention_decode",
    )(r3(q), r3(kn), r3(vn), cache_t)
    return o.reshape(b, ATTN_DIM), lse.reshape(b, ATTN_DIM), cnew


def _merge_body(x_ref, cy_ref, ry_ref, o0_ref, l0_ref, o1_ref, l1_ref, o2_ref, l2_ref, g_ref,
                wg_ref, wc_ref, wr_ref, wa_ref, wo_ref, out_ref):
    x = x_ref[...]
    h = _rms(x, g_ref[2:3, :]).astype(BF16)
    l0, l1, l2 = l0_ref[...], l1_ref[...], l2_ref[...]
    mx = jnp.maximum(jnp.maximum(l0, l1), l2)
    e0, e1, e2 = jnp.exp(l0 - mx), jnp.exp(l1 - mx), jnp.exp(l2 - mx)
    den = e0 + e1 + e2
    attn_y = ((e0 / den) * o0_ref[...] + (e1 / den) * o1_ref[...] + (e2 / den) * o2_ref[...]).astype(BF16)
    d = x.shape[1]
    merged = jax.nn.sigmoid(_dot(h, wg_ref[:, 0:d])) * _dot(cy_ref[...].astype(BF16), wc_ref[...])
    merged = merged + jax.nn.sigmoid(_dot(h, wg_ref[:, d:2 * d])) * _dot(ry_ref[...].astype(BF16), wr_ref[...])
    merged = merged + jax.nn.sigmoid(_dot(h, wg_ref[:, 2 * d:3 * d])) * _dot(attn_y, wa_ref[...])
    out_ref[...] = x + _rms(_dot(merged.astype(BF16), wo_ref[...]), g_ref[3:4, :])


def _merge(x, cy, ry, attn, gains, wg, wc, wr, wa, wo, tm):
    m, d = x.shape
    (o0, l0), (o1, l1), (o2, l2) = attn
    a_spec = _row_spec(tm, ATTN_DIM)
    return pl.pallas_call(
        _merge_body,
        grid=(m // tm,),
        in_specs=[_row_spec(tm, d), _row_spec(tm, CONV_DIM), _row_spec(tm, 512)] + [a_spec] * 6
        + [_const_spec(a.shape) for a in (gains, wg, wc, wr, wa, wo)],
        out_specs=_row_spec(tm, d),
        out_shape=jax.ShapeDtypeStruct((m, d), F32),
        compiler_params=_params("parallel"),
        name="merge",
    )(x, cy, ry, o0, l0, o1, l1, o2, l2, gains, wg, wc, wr, wa, wo)


def _rotary_tables(pos):
    posf = pos.astype(F32)[:, None]
    t = pos.shape[0]
    half = RET_DK // 2
    ang = posf * jnp.exp(-jnp.arange(half, dtype=F32) * (math.log(RET_THETA) / half))[None, :]
    cos, sin = jnp.cos(ang), jnp.sin(ang)
    rc = jnp.concatenate([cos, cos], axis=1)
    rs = jnp.concatenate([-sin, sin], axis=1)
    half = ROT_DIM // 2
    ang = posf * jnp.exp(-jnp.arange(half, dtype=F32) * (math.log(ROPE_THETA) / half))[None, :]
    cos, sin = jnp.cos(ang), jnp.sin(ang)
    rest = HEAD_DIM - ROT_DIM
    ac = jnp.concatenate([cos, cos, jnp.ones((t, rest), F32)], axis=1)
    alo = jnp.concatenate([-sin, jnp.zeros((t, HEAD_DIM - half), F32)], axis=1)
    ahi = jnp.concatenate([jnp.zeros((t, half), F32), sin, jnp.zeros((t, rest), F32)], axis=1)
    two = lambda a: jnp.concatenate([a, a], axis=1)
    return rc, rs, two(ac), two(alo), two(ahi)


def _layer_weights(l, norm_gain, w_ffn1_gu, w_ffn1_down, w_in, w_conv, b_conv, w_conv_out, w_ret_out, w_attn_out,
                   w_gate, w_o, w_ffn2_gu, w_ffn2_down, w_ple_gate, w_ple_proj):
    bf = lambda w: w[l].astype(BF16)
    return dict(
        gains=norm_gain[l], ffn1_gu=bf(w_ffn1_gu), ffn1_down=bf(w_ffn1_down), w_in=bf(w_in),
        w_conv=w_conv[l], b_conv=b_conv[l].reshape(1, CONV_DIM), w_conv_out=bf(w_conv_out),
        w_ret_out=bf(w_ret_out), w_attn_out=bf(w_attn_out), w_gate=bf(w_gate), w_o=bf(w_o),
        ffn2_gu=bf(w_ffn2_gu), ffn2_down=bf(w_ffn2_down), w_ple_gate=bf(w_ple_gate), w_ple_proj=bf(w_ple_proj))


def _finish(x, ple, cy, ry, attn, lw, tm):
    x = _merge(x, cy, ry, attn, lw['gains'], lw['w_gate'], lw['w_conv_out'], lw['w_ret_out'], lw['w_attn_out'],
               lw['w_o'], tm)
    x = _ffn(x, lw['gains'], lw['ffn2_gu'], lw['ffn2_down'], 4, 5, tm)
    return _ple(x, ple, lw['gains'], lw['w_ple_gate'], lw['w_ple_proj'], tm)


def _prompt_layer(x, ple, tabs, ret_tabs, lw):
    s = x.shape[0]
    tm = min(ROW_TILE, s)
    x = _ffn(x, lw['gains'], lw['ffn1_gu'], lw['ffn1_down'], 0, 1, tm)
    hist = jnp.zeros((8, CONV_DIM), F32)
    outs = _proj(x, lw['gains'], lw['w_in'], lw['w_conv'], lw['b_conv'], tabs, hist, tm, decode=False)
    cy, ulast, rq, rk, rv, rg = outs[:6]
    ry, ret_state = _retention(rq, rk, rv, rg, ret_tabs, tm)
    attn, kv = [], []
    for gi, (window, dil) in enumerate(ATTN_GROUPS):
        aq, ak, av = outs[6 + 3 * gi:9 + 3 * gi]
        attn.append(_attention(aq, ak, av, dil))
        keep = min(window, s)
        kv.append(jnp.stack([ak[s - keep:], av[s - keep:]], axis=1).reshape(1, keep, 2, ATTN_HEADS, HEAD_DIM))
    x = _finish(x, ple, cy, ry, attn, lw, tm)
    return x, ulast[6:8][None], ret_state[None], kv


def _sample_layer(x, ple, tabs, gdec, conv_hist, ret_state, caches_t, layer, lw):
    b = x.shape[0]
    x = _ffn(x, lw['gains'], lw['ffn1_gu'], lw['ffn1_down'], 0, 1, b)
    hist = jnp.swapaxes(conv_hist, 0, 1)
    outs = _proj(x, lw['gains'], lw['w_in'], lw['w_conv'], lw['b_conv'], tabs, hist, b, decode=True)
    cy, ulast, rq, rk, rv, rg = outs[:6]
    ry, ret_new = _retention_decode(rq, rk, rv, rg, gdec, ret_state)
    attn, kv = [], []
    for gi, (window, dil) in enumerate(ATTN_GROUPS):
        aq, ak, av = outs[6 + 3 * gi:9 + 3 * gi]
        o, lse, cnew = _attention_decode(aq, ak, av, caches_t[gi], layer, window, dil)
        attn.append((o, lse))
        kv.append(cnew)
    x = _finish(x, ple, cy, ry, attn, lw, b)
    return x, jnp.swapaxes(ulast, 0, 1), ret_new, kv


def kernel(x_prompt, x_sample, state_conv, state_ret, cache_kv_w128, cache_kv_w512, cache_kv_w2048, p_prompt, p_sample, norm_gain, w_ffn1_gu, w_ffn1_down, w_in, w_conv, b_conv, w_conv_out, w_ret_out, w_attn_out, w_gate, w_o, w_ffn2_gu, w_ffn2_down, w_ple_gate, w_ple_proj):
    depth = norm_gain.shape[0]
    seq = x_prompt.shape[1]
    nb = x_sample.shape[0]
    assert x_prompt.shape[0] == 1 and x_sample.shape[1] == 1

    tabs_p = _rotary_tables(jnp.arange(seq))
    tabs_s = tuple(jnp.broadcast_to(t, (nb, 128)) for t in _rotary_tables(jnp.full((1,), PAST_LEN)))
    decay, qw, kw, gch, log_g = _ret_tables()
    gdec = jnp.broadcast_to(jnp.exp(log_g)[:, None, None], (RET_HEADS, 1, RET_DK))
    caches_t = [jnp.transpose(c, (0, 1, 3, 4, 5, 2)) for c in (cache_kv_w128, cache_kv_w512, cache_kv_w2048)]

    yp, ys = x_prompt[0], x_sample[:, 0]
    conv_p, conv_s, ret_p, ret_s = [], [], [], []
    kv_p = [[] for _ in ATTN_GROUPS]
    kv_s = [[] for _ in ATTN_GROUPS]
    for l in range(depth):
        lw = _layer_weights(l, norm_gain, w_ffn1_gu, w_ffn1_down, w_in, w_conv, b_conv, w_conv_out, w_ret_out,
                            w_attn_out, w_gate, w_o, w_ffn2_gu, w_ffn2_down, w_ple_gate, w_ple_proj)
        yp, cp, rp, kp = _prompt_layer(yp, p_prompt[l, 0], tabs_p, (decay, qw, kw, gch), lw)
        ys, cs, rs, ks = _sample_layer(ys, p_sample[l, :, 0], tabs_s, gdec, state_conv[l], state_ret[l],
                                       caches_t, l, lw)
        conv_p.append(cp)
        conv_s.append(cs)
        ret_p.append(rp)
        ret_s.append(rs)
        for gi in range(len(ATTN_GROUPS)):
            kv_p[gi].append(kp[gi])
            kv_s[gi].append(ks[gi])
    back = lambda c: jnp.transpose(jnp.stack(c), (0, 1, 5, 2, 3, 4))
    return (yp[None], ys[:, None], jnp.stack(conv_p), jnp.stack(conv_s), jnp.stack(ret_p), jnp.stack(ret_s),
            jnp.stack(kv_p[0]), back(kv_s[0]), jnp.stack(kv_p[1]), back(kv_s[1]),
            jnp.stack(kv_p[2]), back(kv_s[2]))
```

```python
import functools
import math

import jax
import jax.numpy as jnp
from jax import lax
from jax.experimental import pallas as pl
from jax.experimental.pallas import tpu as pltpu

F32 = jnp.float32
BF16 = jnp.bfloat16

EPS = 1e-6
PAST_LEN = 16384
CONV_DIM = 512
RET_HEADS = 4
RET_DK = 128
RET_CHUNK = 128
RET_THETA = 10000.0
ATTN_GROUPS = ((128, 1), (512, 4), (2048, 16))
DILATIONS = tuple(d for _, d in ATTN_GROUPS)
ATTN_HEADS = 4
HEAD_DIM = 64
ATTN_DIM = ATTN_HEADS * HEAD_DIM
ATTN_SCALE = HEAD_DIM ** -0.5
ROT_DIM = 16
ROPE_THETA = 500000.0
ATTN_BLOCK = 128
ATTN_STEP = 512
NEG = -1e30
LANES = 128
OFF_CONV = 0
OFF_RET = 3 * CONV_DIM
OFF_ATTN = OFF_RET + 4 * 512

VMEM_LIMIT_BYTES = 60000 * 1024
ROW_TILE = 512


def _params(*sem):
    return pltpu.CompilerParams(dimension_semantics=sem, vmem_limit_bytes=VMEM_LIMIT_BYTES)


def _const_spec(shape):
    zeros = (0,) * len(shape)
    return pl.BlockSpec(shape, lambda *_: zeros, pipeline_mode=pl.Buffered(1))


def _layer_spec(shape, layer):
    tail = tuple(shape[1:])
    idx = (layer,) + (0,) * len(tail)
    return pl.BlockSpec((None,) + tail, lambda *_: idx, pipeline_mode=pl.Buffered(1))


def _row_spec(tm, width):
    return pl.BlockSpec((tm, width), lambda i: (i, 0))


def _stream_spec(tm, dil):
    return pl.BlockSpec((dil, tm // dil, ATTN_DIM), lambda i: (0, i, 0))


def _rms(x, g):
    return x * lax.rsqrt(jnp.mean(x * x, axis=-1, keepdims=True) + EPS) * g


def _dot(a, b):
    return jnp.dot(a, b, preferred_element_type=F32)


def _dot_nt(a, b):
    return lax.dot_general(a, b, (((1,), (1,)), ((), ())), preferred_element_type=F32)


def _silu(x):
    return x * jax.nn.sigmoid(x)


def _ffn_body(x_ref, g_ref, wgu_ref, wd_ref, o_ref, *, pre, post, d_ff, n_chunk):
    x = x_ref[...]
    h = _rms(x, g_ref[pre:pre + 1, :]).astype(BF16)
    tf = d_ff // n_chunk
    acc = None
    for c in range(n_chunk):
        gate = _dot(h, wgu_ref[:, c * tf:(c + 1) * tf])
        up = _dot(h, wgu_ref[:, d_ff + c * tf:d_ff + (c + 1) * tf])
        a = (_silu(gate) * up).astype(BF16)
        y = _dot(a, wd_ref[c * tf:(c + 1) * tf, :])
        acc = y if acc is None else acc + y
    o_ref[...] = x + 0.5 * _rms(acc, g_ref[post:post + 1, :])


def _ffn(x, gains, wgu, wd, layer, pre, post, tm):
    m, d = x.shape
    d_ff = wd.shape[1]
    body = functools.partial(_ffn_body, pre=pre, post=post, d_ff=d_ff, n_chunk=2)
    return pl.pallas_call(
        body,
        grid=(m // tm,),
        in_specs=[_row_spec(tm, d), _layer_spec(gains.shape, layer), _layer_spec(wgu.shape, layer),
                  _layer_spec(wd.shape, layer)],
        out_specs=_row_spec(tm, d),
        out_shape=jax.ShapeDtypeStruct((m, d), F32),
        compiler_params=_params("parallel"),
        name="ffn",
    )(x, gains, wgu, wd)


def _ple_body(x_ref, p_ref, g_ref, wg_ref, wp_ref, o_ref):
    x = x_ref[...]
    h = _rms(x, g_ref[6:7, :]).astype(BF16)
    gate = jax.nn.sigmoid(_dot(h, wg_ref[...]))
    proj = _dot(p_ref[...].astype(BF16), wp_ref[...])
    o_ref[...] = x + _rms(gate * proj, g_ref[7:8, :])


def _ple(x, p, gains, wg, wp, layer, tm):
    m, d = x.shape
    p_spec = pl.BlockSpec((None, tm, p.shape[2]), lambda i: (layer, i, 0))
    return pl.pallas_call(
        _ple_body,
        grid=(m // tm,),
        in_specs=[_row_spec(tm, d), p_spec, _layer_spec(gains.shape, layer),
                  _layer_spec(wg.shape, layer), _layer_spec(wp.shape, layer)],
        out_specs=_row_spec(tm, d),
        out_shape=jax.ShapeDtypeStruct((m, d), F32),
        compiler_params=_params("parallel"),
        name="ple",
    )(x, p, gains, wg, wp)


def _rot_ret(x, cos, sin):
    return x * cos + pltpu.roll(x, RET_DK // 2, 1) * sin


def _rot_attn(x, cos, sin_lo, sin_hi):
    return x * cos + pltpu.roll(x, LANES - ROT_DIM // 2, 1) * sin_lo + pltpu.roll(x, ROT_DIM // 2, 1) * sin_hi


def _to_streams(halves, out_ref, scr_ref, dil):
    if dil == 1:
        for s, v in enumerate(halves):
            out_ref[0, :, s * LANES:(s + 1) * LANES] = v
        return
    n = halves[0].shape[0] // dil
    for s, v in enumerate(halves):
        scr_ref[s] = v
    for r in range(dil):
        for s in range(2):
            out_ref[r, :, s * LANES:(s + 1) * LANES] = scr_ref[s, pl.ds(r, n, stride=dil), :]


def _proj_body(*refs, dils):
    (x_ref, g_ref, w_ref, wc_ref, bc_ref, rc_ref, rs_ref, ac_ref, alo_ref, ahi_ref, hist_ref) = refs[:11]
    outs = refs[11:]
    cy_ref, ulast_ref, rq_ref, rk_ref, rv_ref, rg_ref = outs[:6]
    attn_refs = outs[6:15]
    decode = dils is None
    if not decode:
        carry_ref, scr_ref = outs[15:17]

    x = x_ref[...]
    h = _rms(x, g_ref[2:3, :]).astype(BF16)

    def part(off, width):
        return _dot(h, w_ref[:, off:off + width])

    b_gate = part(OFF_CONV, CONV_DIM)
    u = part(OFF_CONV + CONV_DIM, CONV_DIM) * part(OFF_CONV + 2 * CONV_DIM, CONV_DIM)
    if decode:
        u2 = hist_ref[0]
        u1 = hist_ref[1]
        ulast_ref[0] = u1
        ulast_ref[1] = u
    else:
        tm = u.shape[0]

        @pl.when(pl.program_id(0) == 0)
        def _():
            carry_ref[...] = hist_ref[...]

        carry = carry_ref[...]
        row = lax.broadcasted_iota(jnp.int32, u.shape, 0)
        u1 = jnp.where(row == 0, carry[7:8, :], pltpu.roll(u, 1, 0))
        u2 = jnp.where(row == 0, carry[6:7, :], jnp.where(row == 1, carry[7:8, :], pltpu.roll(u, 2, 0)))
        carry_ref[...] = u[tm - 8:tm, :]
        ulast_ref[...] = u[tm - 8:tm, :]
    y = bc_ref[...] + wc_ref[0:1, :] * u2
    y = y + wc_ref[1:2, :] * u1
    y = y + wc_ref[2:3, :] * u
    cy_ref[...] = (b_gate * y).astype(cy_ref.dtype)

    cos, sin = rc_ref[...], rs_ref[...]
    zq = part(OFF_RET, 512)
    zk = part(OFF_RET + 512, 512)
    for hh in range(RET_HEADS):
        cols = slice(hh * RET_DK, (hh + 1) * RET_DK)
        rq_ref[:, cols] = _rot_ret(zq[:, cols], cos, sin)
        rk_ref[:, cols] = _rot_ret(zk[:, cols], cos, sin) * (RET_DK ** -0.5)
    rv_ref[...] = part(OFF_RET + 1024, 512)
    rg_ref[...] = part(OFF_RET + 1536, 512)

    acos, alo, ahi = ac_ref[...], alo_ref[...], ahi_ref[...]
    for gi in range(len(ATTN_GROUPS)):
        base = OFF_ATTN + gi * 3 * ATTN_DIM
        zq = part(base, ATTN_DIM)
        zk = part(base + ATTN_DIM, ATTN_DIM)
        zv = part(base + 2 * ATTN_DIM, ATTN_DIM)
        halves = lambda z: [z[:, s * LANES:(s + 1) * LANES] for s in range(2)]
        q_h = [_rot_attn(v, acos, alo, ahi) for v in halves(zq)]
        k_h = [_rot_attn(v, acos, alo, ahi) for v in halves(zk)]
        for j, vals in enumerate((q_h, k_h, halves(zv))):
            out_ref = attn_refs[3 * gi + j]
            if decode:
                for s, v in enumerate(vals):
                    out_ref[:, s * LANES:(s + 1) * LANES] = v
            else:
                _to_streams(vals, out_ref, scr_ref.at[3 * gi + j], dils[gi])


def _proj(x, gains, w_in, w_conv, b_conv, tabs, hist, layer, tm, decode):
    m, d = x.shape
    rc, rs, ac, alo, ahi = tabs
    n = m // tm
    if decode:
        dils = None
        hist_spec = pl.BlockSpec((None, 2, tm, CONV_DIM), lambda i: (layer, 0, i, 0))
        ulast_spec = pl.BlockSpec((2, tm, CONV_DIM), lambda i: (0, i, 0))
        ulast_shape = jax.ShapeDtypeStruct((2, m, CONV_DIM), F32)
        attn_specs = [_row_spec(tm, ATTN_DIM)] * 9
        attn_shapes = [jax.ShapeDtypeStruct((m, ATTN_DIM), F32)] * 9
        scratch = []
    else:
        dils = DILATIONS
        hist_spec = _const_spec((8, CONV_DIM))
        ulast_spec = pl.BlockSpec((8, CONV_DIM), lambda i: (0, 0))
        ulast_shape = jax.ShapeDtypeStruct((8, CONV_DIM), F32)
        attn_specs = [_stream_spec(tm, dl) for dl in dils for _ in range(3)]
        attn_shapes = [jax.ShapeDtypeStruct((dl, m // dl, ATTN_DIM), F32) for dl in dils for _ in range(3)]
        scratch = [pltpu.VMEM((8, CONV_DIM), F32), pltpu.VMEM((9, 2, tm, LANES), F32)]
    in_specs = [
        _row_spec(tm, d), _layer_spec(gains.shape, layer), _layer_spec(w_in.shape, layer),
        _layer_spec(w_conv.shape, layer), _layer_spec(b_conv.shape, layer),
        _row_spec(tm, LANES), _row_spec(tm, LANES), _row_spec(tm, LANES), _row_spec(tm, LANES), _row_spec(tm, LANES),
        hist_spec,
    ]
    out_specs = [_row_spec(tm, CONV_DIM), ulast_spec] + [_row_spec(tm, 512)] * 4 + attn_specs
    out_shape = ([jax.ShapeDtypeStruct((m, CONV_DIM), BF16), ulast_shape]
                 + [jax.ShapeDtypeStruct((m, 512), F32)] * 4 + attn_shapes)
    return pl.pallas_call(
        functools.partial(_proj_body, dils=dils),
        grid=(n,),
        in_specs=in_specs,
        out_specs=out_specs,
        out_shape=out_shape,
        scratch_shapes=scratch,
        compiler_params=_params("arbitrary"),
        name="proj",
    )(x, gains, w_in, w_conv, b_conv, rc, rs, ac, alo, ahi, hist)


def _head_norm(o):
    mu = jnp.mean(o, axis=-1, keepdims=True)
    var = jnp.mean(jnp.square(o - mu), axis=-1, keepdims=True)
    return (o - mu) * lax.rsqrt(var + EPS)


def _ret_body(q_ref, k_ref, v_ref, g_ref, decay_ref, qw_ref, kw_ref, gch_ref, y_ref, sout_ref, s_scr, *, n_chunk):
    @pl.when(pl.program_id(0) == 0)
    def _():
        s_scr[...] = jnp.zeros_like(s_scr)

    for c in range(n_chunk):
        rows = slice(c * RET_CHUNK, (c + 1) * RET_CHUNK)
        for hh in range(RET_HEADS):
            cols = slice(hh * RET_DK, (hh + 1) * RET_DK)
            q = q_ref[rows, cols]
            k = k_ref[rows, cols]
            v = v_ref[rows, cols].astype(BF16)
            s = s_scr[hh]
            scores = _dot_nt(q.astype(BF16), k.astype(BF16)) * decay_ref[hh]
            o = _dot(scores.astype(BF16), v) + _dot((q * qw_ref[hh]).astype(BF16), s.astype(BF16))
            kv = _dot(jnp.transpose(k * kw_ref[hh]).astype(BF16), v)
            s_scr[hh] = s * gch_ref[hh] + kv
            y_ref[rows, cols] = (_head_norm(o) * _silu(g_ref[rows, cols])).astype(y_ref.dtype)

    @pl.when(pl.program_id(0) == pl.num_programs(0) - 1)
    def _():
        sout_ref[...] = s_scr[...]


def _ret_tables():
    i = jnp.arange(RET_CHUNK, dtype=F32)
    log_g = jnp.log1p(-jnp.exp2(-5.0 - jnp.arange(RET_HEADS, dtype=F32)))
    diff = i[:, None] - i[None, :]
    decay = jnp.where(diff[None] >= 0, jnp.exp(jnp.maximum(diff, 0.0)[None] * log_g[:, None, None]), 0.0)
    k_w = jnp.exp((RET_CHUNK - 1 - i)[:, None] * log_g[None, :])
    q_w = jnp.exp((i + 1)[:, None] * log_g[None, :])
    g_chunk = jnp.exp(RET_CHUNK * log_g)
    wide = (RET_HEADS, RET_CHUNK, RET_DK)
    qw = jnp.broadcast_to(q_w.T[:, :, None], wide)
    kw = jnp.broadcast_to(k_w.T[:, :, None], wide)
    gch = jnp.broadcast_to(g_chunk[:, None, None], wide)
    return decay, qw, kw, gch, log_g


def _retention(rq, rk, rv, rg, ret_tabs, tr):
    m = rq.shape[0]
    decay, qw, kw, gch = ret_tabs
    tab_spec = _const_spec((RET_HEADS, RET_CHUNK, RET_DK))
    return pl.pallas_call(
        functools.partial(_ret_body, n_chunk=tr // RET_CHUNK),
        grid=(m // tr,),
        in_specs=[_row_spec(tr, 512)] * 4 + [tab_spec] * 4,
        out_specs=[_row_spec(tr, 512), pl.BlockSpec((RET_HEADS, RET_DK, RET_DK), lambda i: (0, 0, 0))],
        out_shape=[jax.ShapeDtypeStruct((m, 512), BF16), jax.ShapeDtypeStruct((RET_HEADS, RET_DK, RET_DK), F32)],
        scratch_shapes=[pltpu.VMEM((RET_HEADS, RET_DK, RET_DK), F32)],
        compiler_params=_params("arbitrary"),
        name="retention",
    )(rq, rk, rv, rg, decay, qw, kw, gch)


def _ret_dec_body(q_ref, k_ref, v_ref, g_ref, gd_ref, s0_ref, *rest):
    y_ref, sout_ref = rest[-2:]
    for hh in range(RET_HEADS):
        cols = slice(hh * RET_DK, (hh + 1) * RET_DK)
        q = q_ref[0, :, cols]
        k = k_ref[0, :, cols]
        v = v_ref[0, :, cols]
        gd = gd_ref[hh]
        s0 = s0_ref[0, hh]
        wide = (RET_DK, RET_DK)
        q_col = jnp.transpose(jnp.broadcast_to(q * gd, wide))
        k_col = jnp.transpose(jnp.broadcast_to(k, wide))
        o_inter = jnp.sum(q_col * s0, axis=0, keepdims=True)
        o_intra = jnp.sum(q * k, axis=-1, keepdims=True) * v
        sout_ref[0, hh] = s0 * gd + k_col * v
        o = o_intra + o_inter
        y_ref[0, :, cols] = _head_norm(o) * _silu(g_ref[0, :, cols])


def _retention_decode(rq, rk, rv, rg, gdec, state, layer, prev_out):
    b = rq.shape[0]
    row = pl.BlockSpec((1, 1, 512), lambda i: (i, 0, 0))
    st = pl.BlockSpec((None, 1, RET_HEADS, RET_DK, RET_DK), lambda i: (layer, i, 0, 0, 0))
    r3 = lambda a: a.reshape(b, 1, 512)
    args = [r3(rq), r3(rk), r3(rv), r3(rg), gdec, state]
    in_specs = [row, row, row, row, _const_spec(gdec.shape), st]
    aliases = {}
    if prev_out is not None:
        args.append(prev_out)
        in_specs.append(pl.BlockSpec(memory_space=pl.ANY))
        aliases = {6: 1}
    y, s_new = pl.pallas_call(
        _ret_dec_body,
        grid=(b,),
        in_specs=in_specs,
        out_specs=[row, st],
        out_shape=[jax.ShapeDtypeStruct((b, 1, 512), F32), jax.ShapeDtypeStruct(state.shape, F32)],
        input_output_aliases=aliases,
        compiler_params=_params("parallel"),
        name="retention_decode",
    )(*args)
    return y.reshape(b, 512), s_new


def _attn_body(q_ref, kc_ref, vc_ref, kp_ref, vp_ref, o_ref, lse_ref, *, n_sub):
    n = pl.program_id(1)
    qb = ATTN_BLOCK
    ii = lax.broadcasted_iota(jnp.int32, (2 * qb, 2 * qb), 0) & (qb - 1)
    jj = lax.broadcasted_iota(jnp.int32, (2 * qb, 2 * qb), 1)
    band = jnp.logical_and(jj >= ii, jj <= ii + qb)
    first = jnp.logical_and(band, jnp.logical_or(jj >= qb, n > 0))
    lo = lax.broadcasted_iota(jnp.int32, (qb, LANES), 1) < HEAD_DIM
    for hp in range(ATTN_HEADS // 2):
        cols = slice(hp * LANES, (hp + 1) * LANES)
        kk = jnp.concatenate([kp_ref[:, cols], kc_ref[:, cols]], axis=0).astype(BF16)
        vv = jnp.concatenate([vp_ref[:, cols], vc_ref[:, cols]], axis=0).astype(BF16)
        for b in range(n_sub):
            rows = slice(b * qb, (b + 1) * qb)
            q2 = q_ref[rows, cols]
            zero = jnp.zeros_like(q2)
            qs = jnp.concatenate([jnp.where(lo, q2, zero), jnp.where(lo, zero, q2)], axis=0).astype(BF16)
            s = _dot_nt(qs, kk[b * qb:(b + 2) * qb]) * ATTN_SCALE
            s = jnp.where(first if b == 0 else band, s, NEG)
            mx = jnp.max(s, axis=-1, keepdims=True)
            p = jnp.exp(s - mx)
            den = jnp.sum(p, axis=-1, keepdims=True)
            o = _dot(p.astype(BF16), vv[b * qb:(b + 2) * qb]) / den
            lse = mx + jnp.log(den)
            o_ref[rows, cols] = jnp.where(lo, o[:qb], o[qb:])
            lse_ref[rows, cols] = jnp.where(lo, lse[:qb], lse[qb:])


def _attention(q, k, v):
    dil, length, _ = q.shape
    step = min(ATTN_STEP, length)
    n_sub = step // ATTN_BLOCK
    cur = pl.BlockSpec((None, step, ATTN_DIM), lambda r, n: (r, n, 0))
    prev = pl.BlockSpec((None, ATTN_BLOCK, ATTN_DIM), lambda r, n: (r, jnp.maximum(n * n_sub - 1, 0), 0))
    return pl.pallas_call(
        functools.partial(_attn_body, n_sub=n_sub),
        grid=(dil, length // step),
        in_specs=[cur, cur, cur, prev, prev],
        out_specs=[cur, cur],
        out_shape=[jax.ShapeDtypeStruct((dil, length, ATTN_DIM), F32)] * 2,
        compiler_params=_params("parallel", "arbitrary"),
        name="attention",
    )(q, k, v, k, v)


def _attn_dec_body(q_ref, kn_ref, vn_ref, c_ref, *rest, window, dil):
    o_ref, lse_ref, cout_ref = rest[-3:]
    lane = lax.broadcasted_iota(jnp.int32, (1, window), 1)
    valid = (lane & (dil - 1)) == 0
    q = q_ref[0]
    kn = kn_ref[0]
    vn = vn_ref[0]
    new_rows = jnp.concatenate([kn, vn], axis=1)
    new_cols = jnp.transpose(jnp.broadcast_to(new_rows, (LANES, 2 * ATTN_DIM)))
    last = lax.broadcasted_iota(jnp.int32, (HEAD_DIM, LANES), 1) == LANES - 1
    for hh in range(ATTN_HEADS):
        cols = slice(hh * HEAD_DIM, (hh + 1) * HEAD_DIM)
        qh = q[:, cols]
        k_t = c_ref[0, 0, hh]
        v_t = c_ref[0, 1, hh]
        q16 = jnp.broadcast_to(qh, (16, HEAD_DIM)).astype(BF16)
        s = _dot(q16, k_t.astype(BF16))[0:1, :] * ATTN_SCALE
        s = jnp.where(valid, s, NEG)
        s_new = jnp.sum(qh * kn[:, cols], axis=-1, keepdims=True) * ATTN_SCALE
        mx = jnp.maximum(jnp.max(s, axis=-1, keepdims=True), s_new)
        p = jnp.exp(s - mx)
        p_new = jnp.exp(s_new - mx)
        den = jnp.sum(p, axis=-1, keepdims=True) + p_new
        p16 = jnp.broadcast_to(p, (16, window)).astype(BF16)
        o = _dot_nt(p16, v_t.astype(BF16))[0:1, :] + p_new * vn[:, cols]
        o_ref[0, :, cols] = o / den
        lse_ref[0, :, cols] = jnp.broadcast_to(mx + jnp.log(den), (1, HEAD_DIM))
        for kv, t in ((0, k_t), (1, v_t)):
            shifted = pltpu.roll(t, window - 1, 1)
            col = new_cols[kv * ATTN_DIM + hh * HEAD_DIM:kv * ATTN_DIM + (hh + 1) * HEAD_DIM, :]
            if window > LANES:
                cout_ref[0, kv, hh, :, 0:window - LANES] = shifted[:, 0:window - LANES]
            cout_ref[0, kv, hh, :, window - LANES:window] = jnp.where(last, col, shifted[:, window - LANES:window])


def _attention_decode(q, kn, vn, cache_t, layer, window, dil, prev_out):
    b = q.shape[0]
    row = pl.BlockSpec((1, 1, ATTN_DIM), lambda i: (i, 0, 0))
    cblk = pl.BlockSpec((None, 1, 2, ATTN_HEADS, HEAD_DIM, window), lambda i: (layer, i, 0, 0, 0, 0))
    r3 = lambda a: a.reshape(b, 1, ATTN_DIM)
    args = [r3(q), r3(kn), r3(vn), cache_t]
    in_specs = [row, row, row, cblk]
    aliases = {}
    if prev_out is not None:
        args.append(prev_out)
        in_specs.append(pl.BlockSpec(memory_space=pl.ANY))
        aliases = {4: 2}
    o, lse, cnew = pl.pallas_call(
        functools.partial(_attn_dec_body, window=window, dil=dil),
        grid=(b,),
        in_specs=in_specs,
        out_specs=[row, row, cblk],
        out_shape=[jax.ShapeDtypeStruct((b, 1, ATTN_DIM), F32)] * 2 + [jax.ShapeDtypeStruct(cache_t.shape, F32)],
        input_output_aliases=aliases,
        compiler_params=_params("parallel"),
        name="attention_decode",
    )(*args)
    return o.reshape(1, b, ATTN_DIM), lse.reshape(1, b, ATTN_DIM), cnew


def _from_streams(in_ref, scr_ref, dil):
    if dil == 1:
        return in_ref[0]
    n = in_ref.shape[1]
    for r in range(dil):
        for s in range(2):
            scr_ref[s, pl.ds(r, n, stride=dil), :] = in_ref[r, :, s * LANES:(s + 1) * LANES]
    return jnp.concatenate([scr_ref[0], scr_ref[1]], axis=1)


def _merge_body(x_ref, cy_ref, ry_ref, o0_ref, l0_ref, o1_ref, l1_ref, o2_ref, l2_ref, g_ref,
                wg_ref, wc_ref, wr_ref, wa_ref, wo_ref, out_ref, *scratch, dils):
    x = x_ref[...]
    h = _rms(x, g_ref[2:3, :]).astype(BF16)
    scr = scratch[0] if scratch else None
    vals = []
    for j, ref in enumerate((o0_ref, l0_ref, o1_ref, l1_ref, o2_ref, l2_ref)):
        vals.append(_from_streams(ref, None if scr is None else scr.at[j], dils[j // 2]))
    o0, l0, o1, l1, o2, l2 = vals
    mx = jnp.maximum(jnp.maximum(l0, l1), l2)
    e0, e1, e2 = jnp.exp(l0 - mx), jnp.exp(l1 - mx), jnp.exp(l2 - mx)
    den = e0 + e1 + e2
    attn_y = ((e0 / den) * o0 + (e1 / den) * o1 + (e2 / den) * o2).astype(BF16)
    d = x.shape[1]
    merged = jax.nn.sigmoid(_dot(h, wg_ref[:, 0:d])) * _dot(cy_ref[...].astype(BF16), wc_ref[...])
    merged = merged + jax.nn.sigmoid(_dot(h, wg_ref[:, d:2 * d])) * _dot(ry_ref[...].astype(BF16), wr_ref[...])
    merged = merged + jax.nn.sigmoid(_dot(h, wg_ref[:, 2 * d:3 * d])) * _dot(attn_y, wa_ref[...])
    out_ref[...] = x + _rms(_dot(merged.astype(BF16), wo_ref[...]), g_ref[3:4, :])


def _merge(x, cy, ry, attn, dils, gains, wg, wc, wr, wa, wo, layer, tm):
    m, d = x.shape
    flat = [a for pair in attn for a in pair]
    a_specs = [_stream_spec(tm, dils[j // 2]) for j in range(6)]
    scratch = [pltpu.VMEM((6, 2, tm, LANES), F32)] if max(dils) > 1 else []
    return pl.pallas_call(
        functools.partial(_merge_body, dils=dils),
        grid=(m // tm,),
        in_specs=[_row_spec(tm, d), _row_spec(tm, CONV_DIM), _row_spec(tm, 512)] + a_specs
        + [_layer_spec(a.shape, layer) for a in (gains, wg, wc, wr, wa, wo)],
        out_specs=_row_spec(tm, d),
        out_shape=jax.ShapeDtypeStruct((m, d), F32),
        scratch_shapes=scratch,
        compiler_params=_params("parallel"),
        name="merge",
    )(x, cy, ry, *flat, gains, wg, wc, wr, wa, wo)


def _rotary_tables(pos):
    posf = pos.astype(F32)[:, None]
    t = pos.shape[0]
    half = RET_DK // 2
    ang = posf * jnp.exp(-jnp.arange(half, dtype=F32) * (math.log(RET_THETA) / half))[None, :]
    cos, sin = jnp.cos(ang), jnp.sin(ang)
    rc = jnp.concatenate([cos, cos], axis=1)
    rs = jnp.concatenate([-sin, sin], axis=1)
    half = ROT_DIM // 2
    ang = posf * jnp.exp(-jnp.arange(half, dtype=F32) * (math.log(ROPE_THETA) / half))[None, :]
    cos, sin = jnp.cos(ang), jnp.sin(ang)
    rest = HEAD_DIM - ROT_DIM
    ac = jnp.concatenate([cos, cos, jnp.ones((t, rest), F32)], axis=1)
    alo = jnp.concatenate([-sin, jnp.zeros((t, HEAD_DIM - half), F32)], axis=1)
    ahi = jnp.concatenate([jnp.zeros((t, half), F32), sin, jnp.zeros((t, rest), F32)], axis=1)
    two = lambda a: jnp.concatenate([a, a], axis=1)
    return rc, rs, two(ac), two(alo), two(ahi)


def _finish(x, ple, cy, ry, attn, dils, w, layer, tm):
    x = _merge(x, cy, ry, attn, dils, w['gains'], w['w_gate'], w['w_conv_out'], w['w_ret_out'], w['w_attn_out'],
               w['w_o'], layer, tm)
    x = _ffn(x, w['gains'], w['ffn2_gu'], w['ffn2_down'], layer, 4, 5, tm)
    return _ple(x, ple, w['gains'], w['w_ple_gate'], w['w_ple_proj'], layer, tm)


def _prompt_layer(x, ple, tabs, ret_tabs, w, layer):
    s = x.shape[0]
    tm = min(ROW_TILE, s)
    x = _ffn(x, w['gains'], w['ffn1_gu'], w['ffn1_down'], layer, 0, 1, tm)
    hist = jnp.zeros((8, CONV_DIM), F32)
    outs = _proj(x, w['gains'], w['w_in'], w['w_conv'], w['b_conv'], tabs, hist, layer, tm, decode=False)
    cy, ulast, rq, rk, rv, rg = outs[:6]
    ry, ret_state = _retention(rq, rk, rv, rg, ret_tabs, tm)
    attn, kv = [], []
    for gi, (window, dil) in enumerate(ATTN_GROUPS):
        aq, ak, av = outs[6 + 3 * gi:9 + 3 * gi]
        attn.append(_attention(aq, ak, av))
        keep = min(window, s) // dil
        tail = lambda a: jnp.swapaxes(a[:, s // dil - keep:], 0, 1).reshape(keep * dil, ATTN_DIM)
        kv.append(jnp.stack([tail(ak), tail(av)], axis=1).reshape(1, keep * dil, 2, ATTN_HEADS, HEAD_DIM))
    x = _finish(x, ple, cy, ry, attn, DILATIONS, w, layer, tm)
    return x, ulast[6:8][None], ret_state[None], kv


def _sample_layer(x, ple, tabs, gdec, conv_hist, ret_state, caches_t, prev, w, layer):
    b = x.shape[0]
    prev_ret, prev_kv = prev
    x = _ffn(x, w['gains'], w['ffn1_gu'], w['ffn1_down'], layer, 0, 1, b)
    outs = _proj(x, w['gains'], w['w_in'], w['w_conv'], w['b_conv'], tabs, conv_hist, layer, b, decode=True)
    cy, ulast, rq, rk, rv, rg = outs[:6]
    ry, ret_new = _retention_decode(rq, rk, rv, rg, gdec, ret_state, layer, prev_ret)
    attn, kv = [], []
    for gi, (window, dil) in enumerate(ATTN_GROUPS):
        aq, ak, av = outs[6 + 3 * gi:9 + 3 * gi]
        o, lse, cnew = _attention_decode(aq, ak, av, caches_t[gi], layer, window, dil, prev_kv[gi])
        attn.append((o, lse))
        kv.append(cnew)
    x = _finish(x, ple, cy, ry, attn, (1, 1, 1), w, layer, b)
    return x, jnp.swapaxes(ulast, 0, 1), (ret_new, kv)


def kernel(x_prompt, x_sample, state_conv, state_ret, cache_kv_w128, cache_kv_w512, cache_kv_w2048, p_prompt, p_sample, norm_gain, w_ffn1_gu, w_ffn1_down, w_in, w_conv, b_conv, w_conv_out, w_ret_out, w_attn_out, w_gate, w_o, w_ffn2_gu, w_ffn2_down, w_ple_gate, w_ple_proj):
    depth = norm_gain.shape[0]
    seq = x_prompt.shape[1]
    nb = x_sample.shape[0]
    assert x_prompt.shape[0] == 1 and x_sample.shape[1] == 1

    tabs_p = _rotary_tables(jnp.arange(seq))
    tabs_s = tuple(jnp.broadcast_to(t, (nb, LANES)) for t in _rotary_tables(jnp.full((1,), PAST_LEN)))
    decay, qw, kw, gch, log_g = _ret_tables()
    gdec = jnp.broadcast_to(jnp.exp(log_g)[:, None, None], (RET_HEADS, 1, RET_DK))
    caches_t = [jnp.transpose(c, (0, 1, 3, 4, 5, 2)) for c in (cache_kv_w128, cache_kv_w512, cache_kv_w2048)]
    bf = lambda a: a.astype(BF16)
    w = dict(
        gains=norm_gain, ffn1_gu=bf(w_ffn1_gu), ffn1_down=bf(w_ffn1_down), w_in=bf(w_in),
        w_conv=w_conv, b_conv=b_conv.reshape(depth, 1, CONV_DIM), w_conv_out=bf(w_conv_out),
        w_ret_out=bf(w_ret_out), w_attn_out=bf(w_attn_out), w_gate=bf(w_gate), w_o=bf(w_o),
        ffn2_gu=bf(w_ffn2_gu), ffn2_down=bf(w_ffn2_down), w_ple_gate=bf(w_ple_gate), w_ple_proj=bf(w_ple_proj))
    ple_p = p_prompt[:, 0]
    ple_s = p_sample[:, :, 0]
    conv_hist = jnp.swapaxes(state_conv, 1, 2)

    yp, ys = x_prompt[0], x_sample[:, 0]
    conv_p, conv_s, ret_p = [], [], []
    kv_p = [[] for _ in ATTN_GROUPS]
    prev = (None, [None] * len(ATTN_GROUPS))
    for l in range(depth):
        yp, cp, rp, kp = _prompt_layer(yp, ple_p, tabs_p, (decay, qw, kw, gch), w, l)
        ys, cs, prev = _sample_layer(ys, ple_s, tabs_s, gdec, conv_hist, state_ret, caches_t, prev, w, l)
        conv_p.append(cp)
        conv_s.append(cs)
        ret_p.append(rp)
        for gi in range(len(ATTN_GROUPS)):
            kv_p[gi].append(kp[gi])
    ret_s, kv_s = prev
    back = lambda c: jnp.transpose(c, (0, 1, 5, 2, 3, 4))
    return (yp[None], ys[:, None], jnp.stack(conv_p), jnp.stack(conv_s), jnp.stack(ret_p), ret_s,
            jnp.stack(kv_p[0]), back(kv_s[0]), jnp.stack(kv_p[1]), back(kv_s[1]),
            jnp.stack(kv_p[2]), back(kv_s[2]))
```

```python
import functools
import math

import jax
import jax.numpy as jnp
from jax import lax
from jax.experimental import pallas as pl
from jax.experimental.pallas import tpu as pltpu

F32 = jnp.float32
BF16 = jnp.bfloat16

EPS = 1e-6
PAST_LEN = 16384
CONV_DIM = 512
RET_HEADS = 4
RET_DK = 128
RET_CHUNK = 128
RET_THETA = 10000.0
ATTN_GROUPS = ((128, 1), (512, 4), (2048, 16))
DILATIONS = tuple(d for _, d in ATTN_GROUPS)
ATTN_HEADS = 4
HEAD_DIM = 64
ATTN_DIM = ATTN_HEADS * HEAD_DIM
ATTN_SCALE = HEAD_DIM ** -0.5
ROT_DIM = 16
ROPE_THETA = 500000.0
ROT_SPLIT = 128
ATTN_BLOCK = 128
ATTN_STEP = 512
DEC_BLOCK_WINDOW = 2048
DEC_BLOCK_SAMPLES = 8
NEG = -1e30
LANES = 128
MXU_WIDTH = 256
OFF_CONV = 0
OFF_RET = 3 * CONV_DIM
OFF_ATTN = OFF_RET + 4 * 512

VMEM_LIMIT_BYTES = 60000 * 1024
ROW_TILE = 512


def _params(*sem):
    return pltpu.CompilerParams(dimension_semantics=sem, vmem_limit_bytes=VMEM_LIMIT_BYTES)


def _const_spec(shape):
    zeros = (0,) * len(shape)
    return pl.BlockSpec(shape, lambda *_: zeros, pipeline_mode=pl.Buffered(1))


def _layer_spec(shape, layer):
    tail = tuple(shape[1:])
    idx = (layer,) + (0,) * len(tail)
    return pl.BlockSpec((None,) + tail, lambda *_: idx, pipeline_mode=pl.Buffered(1))


def _row_spec(tm, width):
    return pl.BlockSpec((tm, width), lambda i: (i, 0))


def _stream_spec(tm, dil):
    return pl.BlockSpec((dil, tm // dil, ATTN_DIM), lambda i: (0, i, 0))


def _rms(x, g):
    return x * lax.rsqrt(jnp.mean(x * x, axis=-1, keepdims=True) + EPS) * g


def _dot(a, b):
    return jnp.dot(a, b, preferred_element_type=F32)


def _dot_nt(a, b):
    return lax.dot_general(a, b, (((1,), (1,)), ((), ())), preferred_element_type=F32)


def _silu(x):
    return x * jax.nn.sigmoid(x)


def _ff_chunks(d_ff):
    first = -(-(d_ff // 2) // MXU_WIDTH) * MXU_WIDTH
    return ((0, first), (first, d_ff - first)) if 0 < first < d_ff else ((0, d_ff),)


def _ffn_body(x_ref, g_ref, wgu_ref, wd_ref, *rest, pre, post, d_ff, with_ple):
    o_ref = rest[-1]
    x = x_ref[...]
    h = _rms(x, g_ref[pre:pre + 1, :]).astype(BF16)
    acc = None
    for start, size in _ff_chunks(d_ff):
        gate = _dot(h, wgu_ref[:, start:start + size])
        up = _dot(h, wgu_ref[:, d_ff + start:d_ff + start + size])
        a = (_silu(gate) * up).astype(BF16)
        y = _dot(a, wd_ref[start:start + size, :])
        acc = y if acc is None else acc + y
    x = x + 0.5 * _rms(acc, g_ref[post:post + 1, :])
    if with_ple:
        p_ref, wg_ref, wp_ref = rest[:3]
        h = _rms(x, g_ref[6:7, :]).astype(BF16)
        gate = jax.nn.sigmoid(_dot(h, wg_ref[...]))
        proj = _dot(p_ref[...].astype(BF16), wp_ref[...])
        x = x + _rms(gate * proj, g_ref[7:8, :])
    o_ref[...] = x


def _ffn(x, gains, wgu, wd, layer, pre, post, tm, ple=None):
    m, d = x.shape
    d_ff = wd.shape[1]
    args = [x, gains, wgu, wd]
    in_specs = [_row_spec(tm, d), _layer_spec(gains.shape, layer), _layer_spec(wgu.shape, layer),
                _layer_spec(wd.shape, layer)]
    if ple is not None:
        p, wg, wp = ple
        args += [p, wg, wp]
        in_specs += [pl.BlockSpec((None, tm, p.shape[2]), lambda i: (layer, i, 0)),
                     _layer_spec(wg.shape, layer), _layer_spec(wp.shape, layer)]
    body = functools.partial(_ffn_body, pre=pre, post=post, d_ff=d_ff, with_ple=ple is not None)
    return pl.pallas_call(
        body,
        grid=(m // tm,),
        in_specs=in_specs,
        out_specs=_row_spec(tm, d),
        out_shape=jax.ShapeDtypeStruct((m, d), F32),
        compiler_params=_params("parallel"),
        name="ffn",
    )(*args)


def _rot_ret(x, cos, sin):
    return x * cos + pltpu.roll(x, RET_DK // 2, 1) * sin


def _rot_attn(x, cos, sin_lo, sin_hi):
    return x * cos + pltpu.roll(x, LANES - ROT_DIM // 2, 1) * sin_lo + pltpu.roll(x, ROT_DIM // 2, 1) * sin_hi


def _to_streams(halves, out_ref, scr_ref, dil):
    if dil == 1:
        for s, v in enumerate(halves):
            out_ref[0, :, s * LANES:(s + 1) * LANES] = v
        return
    n = halves[0].shape[0] // dil
    for s, v in enumerate(halves):
        scr_ref[s] = v
    for r in range(dil):
        for s in range(2):
            out_ref[r, :, s * LANES:(s + 1) * LANES] = scr_ref[s, pl.ds(r, n, stride=dil), :]


def _proj_body(*refs, dils):
    (x_ref, g_ref, w_ref, wc_ref, bc_ref, rc_ref, rs_ref, ac_ref, alo_ref, ahi_ref, hist_ref) = refs[:11]
    outs = refs[11:]
    cy_ref, ulast_ref, rq_ref, rk_ref, rv_ref, rg_ref = outs[:6]
    attn_refs = outs[6:15]
    decode = dils is None
    if not decode:
        carry_ref, scr_ref = outs[15:17]

    x = x_ref[...]
    h = _rms(x, g_ref[2:3, :]).astype(BF16)

    def part(off, width):
        return _dot(h, w_ref[:, off:off + width])

    b_gate = part(OFF_CONV, CONV_DIM)
    u = part(OFF_CONV + CONV_DIM, CONV_DIM) * part(OFF_CONV + 2 * CONV_DIM, CONV_DIM)
    if decode:
        u2 = hist_ref[0]
        u1 = hist_ref[1]
        ulast_ref[0] = u1
        ulast_ref[1] = u
    else:
        tm = u.shape[0]

        @pl.when(pl.program_id(0) == 0)
        def _():
            carry_ref[...] = hist_ref[...]

        carry = carry_ref[...]
        row = lax.broadcasted_iota(jnp.int32, u.shape, 0)
        u1 = jnp.where(row == 0, carry[7:8, :], pltpu.roll(u, 1, 0))
        u2 = jnp.where(row == 0, carry[6:7, :], jnp.where(row == 1, carry[7:8, :], pltpu.roll(u, 2, 0)))
        carry_ref[...] = u[tm - 8:tm, :]
        ulast_ref[...] = u[tm - 8:tm, :]
    y = bc_ref[...] + wc_ref[0:1, :] * u2
    y = y + wc_ref[1:2, :] * u1
    y = y + wc_ref[2:3, :] * u
    cy_ref[...] = (b_gate * y).astype(cy_ref.dtype)

    cos, sin = rc_ref[...], rs_ref[...]
    zq = part(OFF_RET, 512)
    zk = part(OFF_RET + 512, 512)
    for hh in range(RET_HEADS):
        cols = slice(hh * RET_DK, (hh + 1) * RET_DK)
        rq_ref[:, cols] = _rot_ret(zq[:, cols], cos, sin)
        rk_ref[:, cols] = _rot_ret(zk[:, cols], cos, sin) * (RET_DK ** -0.5)
    rv_ref[...] = part(OFF_RET + 1024, 512)
    rg_ref[...] = part(OFF_RET + 1536, 512)

    acos, alo, ahi = ac_ref[...], alo_ref[...], ahi_ref[...]
    for gi in range(len(ATTN_GROUPS)):
        base = OFF_ATTN + gi * 3 * ATTN_DIM
        zq = part(base, ATTN_DIM)
        zk = part(base + ATTN_DIM, ATTN_DIM)
        zv = part(base + 2 * ATTN_DIM, ATTN_DIM)
        halves = lambda z: [z[:, s * LANES:(s + 1) * LANES] for s in range(2)]
        q_h = [_rot_attn(v, acos, alo, ahi) for v in halves(zq)]
        k_h = [_rot_attn(v, acos, alo, ahi) for v in halves(zk)]
        for j, vals in enumerate((q_h, k_h, halves(zv))):
            out_ref = attn_refs[3 * gi + j]
            if decode:
                for s, v in enumerate(vals):
                    out_ref[:, s * LANES:(s + 1) * LANES] = v
            else:
                _to_streams(vals, out_ref, scr_ref.at[3 * gi + j], dils[gi])


def _proj(x, gains, w_in, w_conv, b_conv, tabs, hist, layer, tm, decode):
    m, d = x.shape
    rc, rs, ac, alo, ahi = tabs
    n = m // tm
    if decode:
        dils = None
        hist_spec = pl.BlockSpec((None, 2, tm, CONV_DIM), lambda i: (layer, 0, i, 0))
        ulast_spec = pl.BlockSpec((2, tm, CONV_DIM), lambda i: (0, i, 0))
        ulast_shape = jax.ShapeDtypeStruct((2, m, CONV_DIM), F32)
        attn_specs = [_row_spec(tm, ATTN_DIM)] * 9
        attn_shapes = [jax.ShapeDtypeStruct((m, ATTN_DIM), F32)] * 9
        scratch = []
    else:
        dils = DILATIONS
        hist_spec = _const_spec((8, CONV_DIM))
        ulast_spec = pl.BlockSpec((8, CONV_DIM), lambda i: (0, 0))
        ulast_shape = jax.ShapeDtypeStruct((8, CONV_DIM), F32)
        attn_specs = [_stream_spec(tm, dl) for dl in dils for _ in range(3)]
        attn_shapes = [jax.ShapeDtypeStruct((dl, m // dl, ATTN_DIM), F32) for dl in dils for _ in range(3)]
        scratch = [pltpu.VMEM((8, CONV_DIM), F32), pltpu.VMEM((9, 2, tm, LANES), F32)]
    in_specs = [
        _row_spec(tm, d), _layer_spec(gains.shape, layer), _layer_spec(w_in.shape, layer),
        _layer_spec(w_conv.shape, layer), _layer_spec(b_conv.shape, layer),
        _row_spec(tm, LANES), _row_spec(tm, LANES), _row_spec(tm, LANES), _row_spec(tm, LANES), _row_spec(tm, LANES),
        hist_spec,
    ]
    out_specs = [_row_spec(tm, CONV_DIM), ulast_spec] + [_row_spec(tm, 512)] * 4 + attn_specs
    out_shape = ([jax.ShapeDtypeStruct((m, CONV_DIM), BF16), ulast_shape]
                 + [jax.ShapeDtypeStruct((m, 512), F32)] * 4 + attn_shapes)
    return pl.pallas_call(
        functools.partial(_proj_body, dils=dils),
        grid=(n,),
        in_specs=in_specs,
        out_specs=out_specs,
        out_shape=out_shape,
        scratch_shapes=scratch,
        compiler_params=_params("arbitrary"),
        name="proj",
    )(x, gains, w_in, w_conv, b_conv, rc, rs, ac, alo, ahi, hist)


def _head_norm(o):
    mu = jnp.mean(o, axis=-1, keepdims=True)
    var = jnp.mean(jnp.square(o - mu), axis=-1, keepdims=True)
    return (o - mu) * lax.rsqrt(var + EPS)


def _ret_body(q_ref, k_ref, v_ref, g_ref, decay_ref, qw_ref, kw_ref, gch_ref, y_ref, sout_ref, s_scr, *, n_chunk):
    @pl.when(pl.program_id(0) == 0)
    def _():
        s_scr[...] = jnp.zeros_like(s_scr)

    for c in range(n_chunk):
        rows = slice(c * RET_CHUNK, (c + 1) * RET_CHUNK)
        for hh in range(RET_HEADS):
            cols = slice(hh * RET_DK, (hh + 1) * RET_DK)
            q = q_ref[rows, cols]
            k = k_ref[rows, cols]
            v = v_ref[rows, cols].astype(BF16)
            s = s_scr[hh]
            scores = _dot_nt(q.astype(BF16), k.astype(BF16)) * decay_ref[hh]
            o = _dot(scores.astype(BF16), v) + _dot((q * qw_ref[hh]).astype(BF16), s.astype(BF16))
            kv = _dot(jnp.transpose(k * kw_ref[hh]).astype(BF16), v)
            s_scr[hh] = s * gch_ref[hh] + kv
            y_ref[rows, cols] = (_head_norm(o) * _silu(g_ref[rows, cols])).astype(y_ref.dtype)

    @pl.when(pl.program_id(0) == pl.num_programs(0) - 1)
    def _():
        sout_ref[...] = s_scr[...]


def _ret_tables():
    i = jnp.arange(RET_CHUNK, dtype=F32)
    log_g = jnp.log1p(-jnp.exp2(-5.0 - jnp.arange(RET_HEADS, dtype=F32)))
    diff = i[:, None] - i[None, :]
    decay = jnp.where(diff[None] >= 0, jnp.exp(jnp.maximum(diff, 0.0)[None] * log_g[:, None, None]), 0.0)
    k_w = jnp.exp((RET_CHUNK - 1 - i)[:, None] * log_g[None, :])
    q_w = jnp.exp((i + 1)[:, None] * log_g[None, :])
    g_chunk = jnp.exp(RET_CHUNK * log_g)
    wide = (RET_HEADS, RET_CHUNK, RET_DK)
    qw = jnp.broadcast_to(q_w.T[:, :, None], wide)
    kw = jnp.broadcast_to(k_w.T[:, :, None], wide)
    gch = jnp.broadcast_to(g_chunk[:, None, None], wide)
    return decay, qw, kw, gch, log_g


def _retention(rq, rk, rv, rg, ret_tabs, tr):
    m = rq.shape[0]
    decay, qw, kw, gch = ret_tabs
    tab_spec = _const_spec((RET_HEADS, RET_CHUNK, RET_DK))
    return pl.pallas_call(
        functools.partial(_ret_body, n_chunk=tr // RET_CHUNK),
        grid=(m // tr,),
        in_specs=[_row_spec(tr, 512)] * 4 + [tab_spec] * 4,
        out_specs=[_row_spec(tr, 512), pl.BlockSpec((RET_HEADS, RET_DK, RET_DK), lambda i: (0, 0, 0))],
        out_shape=[jax.ShapeDtypeStruct((m, 512), BF16), jax.ShapeDtypeStruct((RET_HEADS, RET_DK, RET_DK), F32)],
        scratch_shapes=[pltpu.VMEM((RET_HEADS, RET_DK, RET_DK), F32)],
        compiler_params=_params("arbitrary"),
        name="retention",
    )(rq, rk, rv, rg, decay, qw, kw, gch)


def _ret_dec_body(q_ref, k_ref, v_ref, g_ref, gd_ref, s0_ref, *rest):
    y_ref, sout_ref = rest[-2:]
    for hh in range(RET_HEADS):
        cols = slice(hh * RET_DK, (hh + 1) * RET_DK)
        q = q_ref[0, :, cols]
        k = k_ref[0, :, cols]
        v = v_ref[0, :, cols]
        gd = gd_ref[hh]
        s0 = s0_ref[0, hh]
        wide = (RET_DK, RET_DK)
        q_col = jnp.transpose(jnp.broadcast_to(q * gd, wide))
        k_col = jnp.transpose(jnp.broadcast_to(k, wide))
        o_inter = jnp.sum(q_col * s0, axis=0, keepdims=True)
        o_intra = jnp.sum(q * k, axis=-1, keepdims=True) * v
        sout_ref[0, hh] = s0 * gd + k_col * v
        o = o_intra + o_inter
        y_ref[0, :, cols] = _head_norm(o) * _silu(g_ref[0, :, cols])


def _retention_decode(rq, rk, rv, rg, gdec, state, layer, prev_out):
    b = rq.shape[0]
    row = pl.BlockSpec((1, 1, 512), lambda i: (i, 0, 0))
    st = pl.BlockSpec((None, 1, RET_HEADS, RET_DK, RET_DK), lambda i: (layer, i, 0, 0, 0))
    r3 = lambda a: a.reshape(b, 1, 512)
    args = [r3(rq), r3(rk), r3(rv), r3(rg), gdec, state]
    in_specs = [row, row, row, row, _const_spec(gdec.shape), st]
    aliases = {}
    if prev_out is not None:
        args.append(prev_out)
        in_specs.append(pl.BlockSpec(memory_space=pl.ANY))
        aliases = {6: 1}
    y, s_new = pl.pallas_call(
        _ret_dec_body,
        grid=(b,),
        in_specs=in_specs,
        out_specs=[row, st],
        out_shape=[jax.ShapeDtypeStruct((b, 1, 512), F32), jax.ShapeDtypeStruct(state.shape, F32)],
        input_output_aliases=aliases,
        compiler_params=_params("parallel"),
        name="retention_decode",
    )(*args)
    return y.reshape(b, 512), s_new


def _attn_body(q_ref, kc_ref, vc_ref, kp_ref, vp_ref, o_ref, lse_ref, *, n_sub):
    n = pl.program_id(1)
    qb = ATTN_BLOCK
    ii = lax.broadcasted_iota(jnp.int32, (2 * qb, 2 * qb), 0) & (qb - 1)
    jj = lax.broadcasted_iota(jnp.int32, (2 * qb, 2 * qb), 1)
    band = jnp.logical_and(jj >= ii, jj <= ii + qb)
    first = jnp.logical_and(band, jnp.logical_or(jj >= qb, n > 0))
    lo = lax.broadcasted_iota(jnp.int32, (qb, LANES), 1) < HEAD_DIM
    for hp in range(ATTN_HEADS // 2):
        cols = slice(hp * LANES, (hp + 1) * LANES)
        kk = jnp.concatenate([kp_ref[:, cols], kc_ref[:, cols]], axis=0).astype(BF16)
        vv = jnp.concatenate([vp_ref[:, cols], vc_ref[:, cols]], axis=0).astype(BF16)
        for b in range(n_sub):
            rows = slice(b * qb, (b + 1) * qb)
            q2 = q_ref[rows, cols]
            zero = jnp.zeros_like(q2)
            qs = jnp.concatenate([jnp.where(lo, q2, zero), jnp.where(lo, zero, q2)], axis=0).astype(BF16)
            s = _dot_nt(qs, kk[b * qb:(b + 2) * qb]) * ATTN_SCALE
            s = jnp.where(first if b == 0 else band, s, NEG)
            mx = jnp.max(s, axis=-1, keepdims=True)
            p = jnp.exp(s - mx)
            den = jnp.sum(p, axis=-1, keepdims=True)
            o = _dot(p.astype(BF16), vv[b * qb:(b + 2) * qb]) / den
            lse = mx + jnp.log(den)
            o_ref[rows, cols] = jnp.where(lo, o[:qb], o[qb:])
            lse_ref[rows, cols] = jnp.where(lo, lse[:qb], lse[qb:])


def _attention(q, k, v):
    dil, length, _ = q.shape
    step = min(ATTN_STEP, length)
    n_sub = step // ATTN_BLOCK
    cur = pl.BlockSpec((None, step, ATTN_DIM), lambda r, n: (r, n, 0))
    prev = pl.BlockSpec((None, ATTN_BLOCK, ATTN_DIM), lambda r, n: (r, jnp.maximum(n * n_sub - 1, 0), 0))
    return pl.pallas_call(
        functools.partial(_attn_body, n_sub=n_sub),
        grid=(dil, length // step),
        in_specs=[cur, cur, cur, prev, prev],
        out_specs=[cur, cur],
        out_shape=[jax.ShapeDtypeStruct((dil, length, ATTN_DIM), F32)] * 2,
        compiler_params=_params("parallel", "arbitrary"),
        name="attention",
    )(q, k, v, k, v)


def _attn_dec_body(q_ref, kn_ref, vn_ref, c_ref, *rest, window, dil):
    o_ref, lse_ref, cout_ref = rest[-3:]
    lane = lax.broadcasted_iota(jnp.int32, (1, window), 1)
    valid = (lane & (dil - 1)) == 0
    last = lax.broadcasted_iota(jnp.int32, (HEAD_DIM, LANES), 1) == LANES - 1
    for b in range(q_ref.shape[0]):
        q = q_ref[b]
        kn = kn_ref[b]
        vn = vn_ref[b]
        new_rows = jnp.concatenate([kn, vn], axis=1)
        new_cols = jnp.transpose(jnp.broadcast_to(new_rows, (LANES, 2 * ATTN_DIM)))
        for hh in range(ATTN_HEADS):
            cols = slice(hh * HEAD_DIM, (hh + 1) * HEAD_DIM)
            qh = q[:, cols]
            k_t = c_ref[b, 0, hh]
            v_t = c_ref[b, 1, hh]
            q16 = jnp.broadcast_to(qh, (16, HEAD_DIM)).astype(BF16)
            s = _dot(q16, k_t.astype(BF16))[0:1, :] * ATTN_SCALE
            s = jnp.where(valid, s, NEG)
            s_new = jnp.sum(qh * kn[:, cols], axis=-1, keepdims=True) * ATTN_SCALE
            mx = jnp.maximum(jnp.max(s, axis=-1, keepdims=True), s_new)
            p = jnp.exp(s - mx)
            p_new = jnp.exp(s_new - mx)
            den = jnp.sum(p, axis=-1, keepdims=True) + p_new
            p16 = jnp.broadcast_to(p, (16, window)).astype(BF16)
            o = _dot_nt(p16, v_t.astype(BF16))[0:1, :] + p_new * vn[:, cols]
            o_ref[b, :, cols] = o / den
            lse_ref[b, :, cols] = jnp.broadcast_to(mx + jnp.log(den), (1, HEAD_DIM))
            for kv, t in ((0, k_t), (1, v_t)):
                shifted = pltpu.roll(t, window - 1, 1)
                col = new_cols[kv * ATTN_DIM + hh * HEAD_DIM:kv * ATTN_DIM + (hh + 1) * HEAD_DIM, :]
                if window > LANES:
                    cout_ref[b, kv, hh, :, 0:window - LANES] = shifted[:, 0:window - LANES]
                cout_ref[b, kv, hh, :, window - LANES:window] = jnp.where(
                    last, col, shifted[:, window - LANES:window])


def _attention_decode(q, kn, vn, cache_t, layer, window, dil, prev_out):
    b = q.shape[0]
    bs = max(1, min(DEC_BLOCK_SAMPLES, DEC_BLOCK_WINDOW // window))
    while b % bs:
        bs -= 1
    row = pl.BlockSpec((bs, 1, ATTN_DIM), lambda i: (i, 0, 0))
    cblk = pl.BlockSpec((None, bs, 2, ATTN_HEADS, HEAD_DIM, window), lambda i: (layer, i, 0, 0, 0, 0))
    r3 = lambda a: a.reshape(b, 1, ATTN_DIM)
    args = [r3(q), r3(kn), r3(vn), cache_t]
    in_specs = [row, row, row, cblk]
    aliases = {}
    if prev_out is not None:
        args.append(prev_out)
        in_specs.append(pl.BlockSpec(memory_space=pl.ANY))
        aliases = {4: 2}
    o, lse, cnew = pl.pallas_call(
        functools.partial(_attn_dec_body, window=window, dil=dil),
        grid=(b // bs,),
        in_specs=in_specs,
        out_specs=[row, row, cblk],
        out_shape=[jax.ShapeDtypeStruct((b, 1, ATTN_DIM), F32)] * 2 + [jax.ShapeDtypeStruct(cache_t.shape, F32)],
        input_output_aliases=aliases,
        compiler_params=_params("parallel"),
        name="attention_decode",
    )(*args)
    return o.reshape(1, b, ATTN_DIM), lse.reshape(1, b, ATTN_DIM), cnew


def _from_streams(in_ref, scr_ref, dil):
    if dil == 1:
        return in_ref[0]
    n = in_ref.shape[1]
    for r in range(dil):
        for s in range(2):
            scr_ref[s, pl.ds(r, n, stride=dil), :] = in_ref[r, :, s * LANES:(s + 1) * LANES]
    return jnp.concatenate([scr_ref[0], scr_ref[1]], axis=1)


def _merge_body(x_ref, cy_ref, ry_ref, o0_ref, l0_ref, o1_ref, l1_ref, o2_ref, l2_ref, g_ref,
                wg_ref, wc_ref, wr_ref, wa_ref, wo_ref, out_ref, *scratch, dils):
    x = x_ref[...]
    h = _rms(x, g_ref[2:3, :]).astype(BF16)
    scr = scratch[0] if scratch else None
    vals = []
    for j, ref in enumerate((o0_ref, l0_ref, o1_ref, l1_ref, o2_ref, l2_ref)):
        vals.append(_from_streams(ref, None if scr is None else scr.at[j], dils[j // 2]))
    o0, l0, o1, l1, o2, l2 = vals
    mx = jnp.maximum(jnp.maximum(l0, l1), l2)
    e0, e1, e2 = jnp.exp(l0 - mx), jnp.exp(l1 - mx), jnp.exp(l2 - mx)
    den = e0 + e1 + e2
    attn_y = ((e0 / den) * o0 + (e1 / den) * o1 + (e2 / den) * o2).astype(BF16)
    d = x.shape[1]
    merged = jax.nn.sigmoid(_dot(h, wg_ref[:, 0:d])) * _dot(cy_ref[...].astype(BF16), wc_ref[...])
    merged = merged + jax.nn.sigmoid(_dot(h, wg_ref[:, d:2 * d])) * _dot(ry_ref[...].astype(BF16), wr_ref[...])
    merged = merged + jax.nn.sigmoid(_dot(h, wg_ref[:, 2 * d:3 * d])) * _dot(attn_y, wa_ref[...])
    out_ref[...] = x + _rms(_dot(merged.astype(BF16), wo_ref[...]), g_ref[3:4, :])


def _merge(x, cy, ry, attn, dils, gains, wg, wc, wr, wa, wo, layer, tm):
    m, d = x.shape
    flat = [a for pair in attn for a in pair]
    a_specs = [_stream_spec(tm, dils[j // 2]) for j in range(6)]
    scratch = [pltpu.VMEM((6, 2, tm, LANES), F32)] if max(dils) > 1 else []
    return pl.pallas_call(
        functools.partial(_merge_body, dils=dils),
        grid=(m // tm,),
        in_specs=[_row_spec(tm, d), _row_spec(tm, CONV_DIM), _row_spec(tm, 512)] + a_specs
        + [_layer_spec(a.shape, layer) for a in (gains, wg, wc, wr, wa, wo)],
        out_specs=_row_spec(tm, d),
        out_shape=jax.ShapeDtypeStruct((m, d), F32),
        scratch_shapes=scratch,
        compiler_params=_params("parallel"),
        name="merge",
    )(x, cy, ry, *flat, gains, wg, wc, wr, wa, wo)


def _cos_sin(start, count, freq, split):
    if split is None or count % split:
        ang = (start + jnp.arange(count)).astype(F32)[:, None] * freq[None, :]
        return jnp.cos(ang), jnp.sin(ang)
    base = (start + split * jnp.arange(count // split)).astype(F32)[:, None, None] * freq
    off = jnp.arange(split).astype(F32)[None, :, None] * freq
    cb, sb, co, so = jnp.cos(base), jnp.sin(base), jnp.cos(off), jnp.sin(off)
    return (cb * co - sb * so).reshape(count, LANES), (sb * co + cb * so).reshape(count, LANES)


def _rotary_tables(start, count, split=None):
    lane = jnp.arange(LANES)
    half = RET_DK // 2
    freq = jnp.exp(-(lane % half).astype(F32) * (math.log(RET_THETA) / half))
    cos, sin = _cos_sin(start, count, freq, split)
    rc, rs = cos, jnp.where(lane < half, -sin, sin)
    dim = lane % HEAD_DIM
    half = ROT_DIM // 2
    freq = jnp.exp(-(dim % half).astype(F32) * (math.log(ROPE_THETA) / half))
    cos, sin = _cos_sin(start, count, freq, split)
    ac = jnp.where(dim < ROT_DIM, cos, 1.0)
    alo = jnp.where(dim < half, -sin, 0.0)
    ahi = jnp.where(jnp.logical_and(dim >= half, dim < ROT_DIM), sin, 0.0)
    return rc, rs, ac, alo, ahi


def _finish(x, ple, cy, ry, attn, dils, w, layer, tm):
    x = _merge(x, cy, ry, attn, dils, w['gains'], w['w_gate'], w['w_conv_out'], w['w_ret_out'], w['w_attn_out'],
               w['w_o'], layer, tm)
    return _ffn(x, w['gains'], w['ffn2_gu'], w['ffn2_down'], layer, 4, 5, tm,
                ple=(ple, w['w_ple_gate'], w['w_ple_proj']))


def _prompt_layer(x, ple, tabs, ret_tabs, w, layer):
    s = x.shape[0]
    tm = min(ROW_TILE, s)
    x = _ffn(x, w['gains'], w['ffn1_gu'], w['ffn1_down'], layer, 0, 1, tm)
    hist = jnp.zeros((8, CONV_DIM), F32)
    outs = _proj(x, w['gains'], w['w_in'], w['w_conv'], w['b_conv'], tabs, hist, layer, tm, decode=False)
    cy, ulast, rq, rk, rv, rg = outs[:6]
    ry, ret_state = _retention(rq, rk, rv, rg, ret_tabs, tm)
    attn, kv = [], []
    for gi, (window, dil) in enumerate(ATTN_GROUPS):
        aq, ak, av = outs[6 + 3 * gi:9 + 3 * gi]
        attn.append(_attention(aq, ak, av))
        keep = min(window, s) // dil
        tail = lambda a: jnp.swapaxes(a[:, s // dil - keep:], 0, 1).reshape(keep * dil, ATTN_DIM)
        kv.append(jnp.stack([tail(ak), tail(av)], axis=1).reshape(1, keep * dil, 2, ATTN_HEADS, HEAD_DIM))
    x = _finish(x, ple, cy, ry, attn, DILATIONS, w, layer, tm)
    return x, ulast[6:8][None], ret_state[None], kv


def _sample_layer(x, ple, tabs, gdec, conv_hist, ret_state, caches_t, prev, w, layer):
    b = x.shape[0]
    prev_ret, prev_kv = prev
    x = _ffn(x, w['gains'], w['ffn1_gu'], w['ffn1_down'], layer, 0, 1, b)
    outs = _proj(x, w['gains'], w['w_in'], w['w_conv'], w['b_conv'], tabs, conv_hist, layer, b, decode=True)
    cy, ulast, rq, rk, rv, rg = outs[:6]
    ry, ret_new = _retention_decode(rq, rk, rv, rg, gdec, ret_state, layer, prev_ret)
    attn, kv = [], []
    for gi, (window, dil) in enumerate(ATTN_GROUPS):
        aq, ak, av = outs[6 + 3 * gi:9 + 3 * gi]
        o, lse, cnew = _attention_decode(aq, ak, av, caches_t[gi], layer, window, dil, prev_kv[gi])
        attn.append((o, lse))
        kv.append(cnew)
    x = _finish(x, ple, cy, ry, attn, (1, 1, 1), w, layer, b)
    return x, jnp.swapaxes(ulast, 0, 1), (ret_new, kv)


def kernel(x_prompt, x_sample, state_conv, state_ret, cache_kv_w128, cache_kv_w512, cache_kv_w2048, p_prompt, p_sample, norm_gain, w_ffn1_gu, w_ffn1_down, w_in, w_conv, b_conv, w_conv_out, w_ret_out, w_attn_out, w_gate, w_o, w_ffn2_gu, w_ffn2_down, w_ple_gate, w_ple_proj):
    depth = norm_gain.shape[0]
    seq = x_prompt.shape[1]
    nb = x_sample.shape[0]
    assert x_prompt.shape[0] == 1 and x_sample.shape[1] == 1

    tabs_p = _rotary_tables(0, seq, split=ROT_SPLIT)
    tabs_s = tuple(jnp.broadcast_to(t, (nb, LANES)) for t in _rotary_tables(PAST_LEN, 1))
    decay, qw, kw, gch, log_g = _ret_tables()
    gdec = jnp.broadcast_to(jnp.exp(log_g)[:, None, None], (RET_HEADS, 1, RET_DK))
    caches_t = [jnp.transpose(c, (0, 1, 3, 4, 5, 2)) for c in (cache_kv_w128, cache_kv_w512, cache_kv_w2048)]
    bf = lambda a: a.astype(BF16)
    w = dict(
        gains=norm_gain, ffn1_gu=bf(w_ffn1_gu), ffn1_down=bf(w_ffn1_down), w_in=bf(w_in),
        w_conv=w_conv, b_conv=b_conv.reshape(depth, 1, CONV_DIM), w_conv_out=bf(w_conv_out),
        w_ret_out=bf(w_ret_out), w_attn_out=bf(w_attn_out), w_gate=bf(w_gate), w_o=bf(w_o),
        ffn2_gu=bf(w_ffn2_gu), ffn2_down=bf(w_ffn2_down), w_ple_gate=bf(w_ple_gate), w_ple_proj=bf(w_ple_proj))
    ple_p = p_prompt[:, 0]
    ple_s = p_sample[:, :, 0]
    conv_hist = jnp.swapaxes(state_conv, 1, 2)

    yp, ys = x_prompt[0], x_sample[:, 0]
    conv_p, conv_s, ret_p = [], [], []
    kv_p = [[] for _ in ATTN_GROUPS]
    prev = (None, [None] * len(ATTN_GROUPS))
    for l in range(depth):
        yp, cp, rp, kp = _prompt_layer(yp, ple_p, tabs_p, (decay, qw, kw, gch), w, l)
        ys, cs, prev = _sample_layer(ys, ple_s, tabs_s, gdec, conv_hist, state_ret, caches_t, prev, w, l)
        conv_p.append(cp)
        conv_s.append(cs)
        ret_p.append(rp)
        for gi in range(len(ATTN_GROUPS)):
            kv_p[gi].append(kp[gi])
    ret_s, kv_s = prev
    back = lambda c: jnp.transpose(c, (0, 1, 5, 2, 3, 4))
    return (yp[None], ys[:, None], jnp.stack(conv_p), jnp.stack(conv_s), jnp.stack(ret_p), ret_s,
            jnp.stack(kv_p[0]), back(kv_s[0]), jnp.stack(kv_p[1]), back(kv_s[1]),
            jnp.stack(kv_p[2]), back(kv_s[2]))
```

```python
import functools
import math

import jax
import jax.numpy as jnp
from jax import lax
from jax.experimental import pallas as pl
from jax.experimental.pallas import tpu as pltpu

F32 = jnp.float32
BF16 = jnp.bfloat16

EPS = 1e-6
PAST_LEN = 16384
CONV_DIM = 512
RET_HEADS = 4
RET_DK = 128
RET_THETA = 10000.0
ATTN_GROUPS = ((128, 1), (512, 4), (2048, 16))
DILATIONS = tuple(d for _, d in ATTN_GROUPS)
ATTN_HEADS = 4
HEAD_DIM = 64
ATTN_DIM = ATTN_HEADS * HEAD_DIM
ATTN_SCALE = HEAD_DIM ** -0.5
ROT_DIM = 16
ROPE_THETA = 500000.0
ROT_SPLIT = 128
ATTN_BLOCK = 128
ATTN_STEP = 512
DEC_BLOCK_WINDOW = 2048
DEC_BLOCK_SAMPLES = 8
NEG = -1e30
LANES = 128
MXU_WIDTH = 256
OFF_CONV = 0
OFF_RET = 3 * CONV_DIM
OFF_ATTN = OFF_RET + 4 * 512

VMEM_LIMIT_BYTES = 60000 * 1024
ROW_TILE = 512


def _params(*sem):
    return pltpu.CompilerParams(dimension_semantics=sem, vmem_limit_bytes=VMEM_LIMIT_BYTES)


def _const_spec(shape):
    zeros = (0,) * len(shape)
    return pl.BlockSpec(shape, lambda *_: zeros, pipeline_mode=pl.Buffered(1))


def _layer_spec(shape, layer):
    tail = tuple(shape[1:])
    idx = (layer,) + (0,) * len(tail)
    return pl.BlockSpec((None,) + tail, lambda *_: idx, pipeline_mode=pl.Buffered(1))


def _row_spec(tm, width):
    return pl.BlockSpec((tm, width), lambda i: (i, 0))


def _stream_spec(tm, dil):
    return pl.BlockSpec((dil, tm // dil, ATTN_DIM), lambda i: (0, i, 0))


def _rms(x, g):
    return x * lax.rsqrt(jnp.mean(x * x, axis=-1, keepdims=True) + EPS) * g


def _dot(a, b):
    return jnp.dot(a, b, preferred_element_type=F32)


def _dot_nt(a, b):
    return lax.dot_general(a, b, (((1,), (1,)), ((), ())), preferred_element_type=F32)


def _silu(x):
    return x * jax.nn.sigmoid(x)


def _ff_chunks(d_ff):
    first = -(-(d_ff // 2) // MXU_WIDTH) * MXU_WIDTH
    return ((0, first), (first, d_ff - first)) if 0 < first < d_ff else ((0, d_ff),)


def _side_operands(n_args, n_outs, side):
    args, in_specs, out_shape, out_specs, aliases, layout = [], [], [], [], {}, []
    for job in side:
        for src, dst in job['aliases'].items():
            aliases[n_args + len(args) + src] = n_outs + len(out_shape) + dst
        layout.append((len(job['args']), len(job['out_shape']), job['fn']))
        args += job['args']
        in_specs += job['in_specs']
        out_shape += job['out_shape']
        out_specs += job['out_specs']
    return args, in_specs, out_shape, out_specs, aliases, tuple(layout)


def _run_side(layout, in_refs, out_refs):
    i = o = 0
    for n_in, n_out, fn in layout:
        fn(in_refs[i:i + n_in], out_refs[o:o + n_out])
        i += n_in
        o += n_out


def _shift_job(cache_t, new_cache, layer, n_steps, kv):
    _, b, _, nh, hd, window = cache_t.shape
    bs = -(-b // n_steps)
    assert b % bs == 0
    n_blk = b // bs
    n_kv, kv_idx = (2, 0) if kv is None else (1, kv)
    spec = pl.BlockSpec((None, bs, n_kv, nh, hd, window),
                        lambda i: (layer, jnp.minimum(i, n_blk - 1), kv_idx, 0, 0, 0))

    def fn(in_refs, out_refs):
        src, dst = in_refs[0], out_refs[0]
        for s in range(bs):
            for j in range(n_kv):
                for hh in range(nh):
                    dst[s, j, hh] = pltpu.roll(src[s, j, hh], window - 1, 1)

    job = dict(args=[cache_t], in_specs=[spec], out_shape=[jax.ShapeDtypeStruct(cache_t.shape, cache_t.dtype)],
               out_specs=[spec], aliases={}, fn=fn)
    if new_cache is not None:
        job['args'].append(new_cache)
        job['in_specs'].append(pl.BlockSpec(memory_space=pl.ANY))
        job['aliases'] = {1: 0}
    return job


def _ffn_body(x_ref, g_ref, wgu_ref, wd_ref, *rest, pre, post, d_ff, with_ple, layout):
    n_ple = 3 if with_ple else 0
    n_side_in = sum(n_in for n_in, _, _ in layout)
    o_ref = rest[n_ple + n_side_in]
    _run_side(layout, rest[n_ple:n_ple + n_side_in], rest[n_ple + n_side_in + 1:])
    x = x_ref[...]
    h = _rms(x, g_ref[pre:pre + 1, :]).astype(BF16)
    acc = None
    for start, size in _ff_chunks(d_ff):
        gate = _dot(h, wgu_ref[:, start:start + size])
        up = _dot(h, wgu_ref[:, d_ff + start:d_ff + start + size])
        a = (_silu(gate) * up).astype(BF16)
        y = _dot(a, wd_ref[start:start + size, :])
        acc = y if acc is None else acc + y
    x = x + 0.5 * _rms(acc, g_ref[post:post + 1, :])
    if with_ple:
        p_ref, wg_ref, wp_ref = rest[:3]
        h = _rms(x, g_ref[6:7, :]).astype(BF16)
        gate = jax.nn.sigmoid(_dot(h, wg_ref[...]))
        proj = _dot(p_ref[...].astype(BF16), wp_ref[...])
        x = x + _rms(gate * proj, g_ref[7:8, :])
    o_ref[...] = x


def _ffn(x, gains, wgu, wd, layer, pre, post, tm, ple=None, side=()):
    m, d = x.shape
    d_ff = wd.shape[1]
    args = [x, gains, wgu, wd]
    in_specs = [_row_spec(tm, d), _layer_spec(gains.shape, layer), _layer_spec(wgu.shape, layer),
                _layer_spec(wd.shape, layer)]
    if ple is not None:
        p, wg, wp = ple
        args += [p, wg, wp]
        in_specs += [pl.BlockSpec((None, tm, p.shape[2]), lambda i: (layer, i, 0)),
                     _layer_spec(wg.shape, layer), _layer_spec(wp.shape, layer)]
    s_args, s_in, s_shape, s_out, aliases, layout = _side_operands(len(args), 1, side)
    body = functools.partial(_ffn_body, pre=pre, post=post, d_ff=d_ff, with_ple=ple is not None, layout=layout)
    outs = pl.pallas_call(
        body,
        grid=(m // tm,),
        in_specs=in_specs + s_in,
        out_specs=[_row_spec(tm, d)] + s_out,
        out_shape=[jax.ShapeDtypeStruct((m, d), F32)] + s_shape,
        input_output_aliases=aliases,
        compiler_params=_params("arbitrary"),
        name="ffn",
    )(*args, *s_args)
    return outs[0], outs[1:]


def _rot_ret(x, cos, sin):
    return x * cos + pltpu.roll(x, RET_DK // 2, 1) * sin


def _rot_attn(x, cos, sin_lo, sin_hi):
    return x * cos + pltpu.roll(x, LANES - ROT_DIM // 2, 1) * sin_lo + pltpu.roll(x, ROT_DIM // 2, 1) * sin_hi


def _to_streams(halves, out_ref, scr_ref, dil):
    if dil == 1:
        for s, v in enumerate(halves):
            out_ref[0, :, s * LANES:(s + 1) * LANES] = v
        return
    n = halves[0].shape[0] // dil
    for s, v in enumerate(halves):
        scr_ref[s] = v
    for r in range(dil):
        for s in range(2):
            out_ref[r, :, s * LANES:(s + 1) * LANES] = scr_ref[s, pl.ds(r, n, stride=dil), :]


def _proj_body(*refs, dils):
    (x_ref, g_ref, w_ref, wc_ref, bc_ref, rc_ref, rs_ref, ac_ref, alo_ref, ahi_ref, hist_ref) = refs[:11]
    outs = refs[11:]
    cy_ref, ulast_ref, rq_ref, rk_ref, rv_ref, rg_ref = outs[:6]
    attn_refs = outs[6:15]
    decode = dils is None
    if not decode:
        carry_ref, scr_ref = outs[15:17]

    x = x_ref[...]
    h = _rms(x, g_ref[2:3, :]).astype(BF16)

    def part(off, width):
        return _dot(h, w_ref[:, off:off + width])


    acos, alo, ahi = ac_ref[...], alo_ref[...], ahi_ref[...]
    for gi in reversed(range(len(ATTN_GROUPS))):
        base = OFF_ATTN + gi * 3 * ATTN_DIM
        zq = part(base, ATTN_DIM)
        zk = part(base + ATTN_DIM, ATTN_DIM)
        zv = part(base + 2 * ATTN_DIM, ATTN_DIM)
        halves = lambda z: [z[:, s * LANES:(s + 1) * LANES] for s in range(2)]
        q_h = [_rot_attn(v, acos, alo, ahi) for v in halves(zq)]
        k_h = [_rot_attn(v, acos, alo, ahi) for v in halves(zk)]
        for j, vals in enumerate((q_h, k_h, halves(zv))):
            out_ref = attn_refs[3 * gi + j]
            if decode:
                for s, v in enumerate(vals):
                    out_ref[:, s * LANES:(s + 1) * LANES] = v
            else:
                _to_streams(vals, out_ref, scr_ref.at[3 * gi + j], dils[gi])

    b_gate = part(OFF_CONV, CONV_DIM)
    u = part(OFF_CONV + CONV_DIM, CONV_DIM) * part(OFF_CONV + 2 * CONV_DIM, CONV_DIM)
    if decode:
        u2 = hist_ref[0]
        u1 = hist_ref[1]
        ulast_ref[0] = u1
        ulast_ref[1] = u
    else:
        tm = u.shape[0]

        @pl.when(pl.program_id(0) == 0)
        def _():
            carry_ref[...] = hist_ref[...]

        carry = carry_ref[...]
        row = lax.broadcasted_iota(jnp.int32, u.shape, 0)
        u1 = jnp.where(row == 0, carry[7:8, :], pltpu.roll(u, 1, 0))
        u2 = jnp.where(row == 0, carry[6:7, :], jnp.where(row == 1, carry[7:8, :], pltpu.roll(u, 2, 0)))
        carry_ref[...] = u[tm - 8:tm, :]
        ulast_ref[...] = u[tm - 8:tm, :]
    y = bc_ref[...] + wc_ref[0:1, :] * u2
    y = y + wc_ref[1:2, :] * u1
    y = y + wc_ref[2:3, :] * u
    cy_ref[...] = (b_gate * y).astype(cy_ref.dtype)

    cos, sin = rc_ref[...], rs_ref[...]
    zq = part(OFF_RET, 512)
    zk = part(OFF_RET + 512, 512)
    for hh in range(RET_HEADS):
        cols = slice(hh * RET_DK, (hh + 1) * RET_DK)
        rq_ref[:, cols] = _rot_ret(zq[:, cols], cos, sin)
        rk_ref[:, cols] = _rot_ret(zk[:, cols], cos, sin) * (RET_DK ** -0.5)
    rv_ref[...] = part(OFF_RET + 1024, 512)
    rg_ref[...] = part(OFF_RET + 1536, 512)


def _proj(x, gains, w_in, w_conv, b_conv, tabs, hist, layer, tm, decode):
    m, d = x.shape
    rc, rs, ac, alo, ahi = tabs
    n = m // tm
    if decode:
        dils = None
        hist_spec = pl.BlockSpec((None, 2, tm, CONV_DIM), lambda i: (layer, 0, i, 0))
        ulast_spec = pl.BlockSpec((2, tm, CONV_DIM), lambda i: (0, i, 0))
        ulast_shape = jax.ShapeDtypeStruct((2, m, CONV_DIM), F32)
        attn_specs = [_row_spec(tm, ATTN_DIM)] * 9
        attn_shapes = [jax.ShapeDtypeStruct((m, ATTN_DIM), F32)] * 9
        scratch = []
    else:
        dils = DILATIONS
        hist_spec = _const_spec((8, CONV_DIM))
        ulast_spec = pl.BlockSpec((8, CONV_DIM), lambda i: (0, 0))
        ulast_shape = jax.ShapeDtypeStruct((8, CONV_DIM), F32)
        attn_specs = [_stream_spec(tm, dl) for dl in dils for _ in range(3)]
        attn_shapes = [jax.ShapeDtypeStruct((dl, m // dl, ATTN_DIM), F32) for dl in dils for _ in range(3)]
        scratch = [pltpu.VMEM((8, CONV_DIM), F32), pltpu.VMEM((9, 2, tm, LANES), F32)]
    in_specs = [
        _row_spec(tm, d), _layer_spec(gains.shape, layer), _layer_spec(w_in.shape, layer),
        _layer_spec(w_conv.shape, layer), _layer_spec(b_conv.shape, layer),
        _row_spec(tm, LANES), _row_spec(tm, LANES), _row_spec(tm, LANES), _row_spec(tm, LANES), _row_spec(tm, LANES),
        hist_spec,
    ]
    out_specs = [_row_spec(tm, CONV_DIM), ulast_spec] + [_row_spec(tm, 512)] * 4 + attn_specs
    out_shape = ([jax.ShapeDtypeStruct((m, CONV_DIM), BF16), ulast_shape]
                 + [jax.ShapeDtypeStruct((m, 512), F32)] * 4 + attn_shapes)
    return pl.pallas_call(
        functools.partial(_proj_body, dils=dils),
        grid=(n,),
        in_specs=in_specs,
        out_specs=out_specs,
        out_shape=out_shape,
        scratch_shapes=scratch,
        compiler_params=_params("arbitrary"),
        name="proj",
    )(x, gains, w_in, w_conv, b_conv, rc, rs, ac, alo, ahi, hist)


def _head_norm(o):
    mu = jnp.mean(o, axis=-1, keepdims=True)
    var = jnp.mean(jnp.square(o - mu), axis=-1, keepdims=True)
    return (o - mu) * lax.rsqrt(var + EPS)


def _ret_body(q_ref, k_ref, v_ref, g_ref, decay_ref, qw_ref, kw_ref, gch_ref, y_ref, sout_ref, s_scr):
    @pl.when(pl.program_id(0) == 0)
    def _():
        s_scr[...] = jnp.zeros_like(s_scr)

    for hh in range(RET_HEADS):
        cols = slice(hh * RET_DK, (hh + 1) * RET_DK)
        q = q_ref[:, cols]
        k = k_ref[:, cols]
        v = v_ref[:, cols].astype(BF16)
        s = s_scr[hh]
        scores = _dot_nt(q.astype(BF16), k.astype(BF16)) * decay_ref[hh]
        o = _dot(scores.astype(BF16), v) + _dot((q * qw_ref[hh]).astype(BF16), s.astype(BF16))
        kv = _dot(jnp.transpose(k * kw_ref[hh]).astype(BF16), v)
        s_scr[hh] = s * gch_ref[hh] + kv
        y_ref[:, cols] = (_head_norm(o) * _silu(g_ref[:, cols])).astype(y_ref.dtype)

    @pl.when(pl.program_id(0) == pl.num_programs(0) - 1)
    def _():
        sout_ref[...] = s_scr[...]


def _ret_log_decay():
    return jnp.log1p(-jnp.exp2(-5.0 - jnp.arange(RET_HEADS, dtype=F32)))


def _ret_tables(chunk):
    i = jnp.arange(chunk, dtype=F32)
    log_g = _ret_log_decay()
    diff = i[:, None] - i[None, :]
    decay = jnp.where(diff[None] >= 0, jnp.exp(jnp.maximum(diff, 0.0)[None] * log_g[:, None, None]), 0.0)
    k_w = jnp.exp((chunk - 1 - i)[:, None] * log_g[None, :])
    q_w = jnp.exp((i + 1)[:, None] * log_g[None, :])
    g_chunk = jnp.exp(chunk * log_g)
    qw = jnp.broadcast_to(q_w.T[:, :, None], (RET_HEADS, chunk, RET_DK))
    kw = jnp.broadcast_to(k_w.T[:, :, None], (RET_HEADS, chunk, RET_DK))
    gch = jnp.broadcast_to(g_chunk[:, None, None], (RET_HEADS, RET_DK, RET_DK))
    return decay, qw, kw, gch


def _retention(rq, rk, rv, rg, tr):
    m = rq.shape[0]
    decay, qw, kw, gch = _ret_tables(tr)
    return pl.pallas_call(
        _ret_body,
        grid=(m // tr,),
        in_specs=[_row_spec(tr, 512)] * 4 + [_const_spec(a.shape) for a in (decay, qw, kw, gch)],
        out_specs=[_row_spec(tr, 512), pl.BlockSpec((RET_HEADS, RET_DK, RET_DK), lambda i: (0, 0, 0))],
        out_shape=[jax.ShapeDtypeStruct((m, 512), BF16), jax.ShapeDtypeStruct((RET_HEADS, RET_DK, RET_DK), F32)],
        scratch_shapes=[pltpu.VMEM((RET_HEADS, RET_DK, RET_DK), F32)],
        compiler_params=_params("arbitrary"),
        name="retention",
    )(rq, rk, rv, rg, decay, qw, kw, gch)


def _ret_dec_body(q_ref, k_ref, v_ref, g_ref, gd_ref, s0_ref, *rest):
    y_ref, sout_ref = rest[-2:]
    for hh in range(RET_HEADS):
        cols = slice(hh * RET_DK, (hh + 1) * RET_DK)
        q = q_ref[0, :, cols]
        k = k_ref[0, :, cols]
        v = v_ref[0, :, cols]
        gd = gd_ref[hh]
        s0 = s0_ref[0, hh]
        wide = (RET_DK, RET_DK)
        q_col = jnp.transpose(jnp.broadcast_to(q * gd, wide))
        k_col = jnp.transpose(jnp.broadcast_to(k, wide))
        o_inter = jnp.sum(q_col * s0, axis=0, keepdims=True)
        o_intra = jnp.sum(q * k, axis=-1, keepdims=True) * v
        sout_ref[0, hh] = s0 * gd + k_col * v
        o = o_intra + o_inter
        y_ref[0, :, cols] = _head_norm(o) * _silu(g_ref[0, :, cols])


def _retention_decode(rq, rk, rv, rg, gdec, state, layer, prev_out):
    b = rq.shape[0]
    row = pl.BlockSpec((1, 1, 512), lambda i: (i, 0, 0))
    st = pl.BlockSpec((None, 1, RET_HEADS, RET_DK, RET_DK), lambda i: (layer, i, 0, 0, 0))
    r3 = lambda a: a.reshape(b, 1, 512)
    args = [r3(rq), r3(rk), r3(rv), r3(rg), gdec, state]
    in_specs = [row, row, row, row, _const_spec(gdec.shape), st]
    aliases = {}
    if prev_out is not None:
        args.append(prev_out)
        in_specs.append(pl.BlockSpec(memory_space=pl.ANY))
        aliases = {6: 1}
    y, s_new = pl.pallas_call(
        _ret_dec_body,
        grid=(b,),
        in_specs=in_specs,
        out_specs=[row, st],
        out_shape=[jax.ShapeDtypeStruct((b, 1, 512), F32), jax.ShapeDtypeStruct(state.shape, F32)],
        input_output_aliases=aliases,
        compiler_params=_params("parallel"),
        name="retention_decode",
    )(*args)
    return y.reshape(b, 512), s_new


def _attn_body(q_ref, kc_ref, vc_ref, kp_ref, vp_ref, o_ref, lse_ref, *, n_sub):
    n = pl.program_id(1)
    qb = ATTN_BLOCK
    ii = lax.broadcasted_iota(jnp.int32, (2 * qb, 2 * qb), 0) & (qb - 1)
    jj = lax.broadcasted_iota(jnp.int32, (2 * qb, 2 * qb), 1)
    band = jnp.logical_and(jj >= ii, jj <= ii + qb)
    first = jnp.logical_and(band, jnp.logical_or(jj >= qb, n > 0))
    lo = lax.broadcasted_iota(jnp.int32, (qb, LANES), 1) < HEAD_DIM
    for hp in range(ATTN_HEADS // 2):
        cols = slice(hp * LANES, (hp + 1) * LANES)
        kk = jnp.concatenate([kp_ref[:, cols], kc_ref[:, cols]], axis=0).astype(BF16)
        vv = jnp.concatenate([vp_ref[:, cols], vc_ref[:, cols]], axis=0).astype(BF16)
        for b in range(n_sub):
            rows = slice(b * qb, (b + 1) * qb)
            q2 = q_ref[rows, cols] * ATTN_SCALE
            zero = jnp.zeros_like(q2)
            qs = jnp.concatenate([jnp.where(lo, q2, zero), jnp.where(lo, zero, q2)], axis=0).astype(BF16)
            s = jnp.where(first if b == 0 else band, _dot_nt(qs, kk[b * qb:(b + 2) * qb]), NEG)
            mx = jnp.max(s, axis=-1, keepdims=True)
            p = jnp.exp(s - mx)
            den = jnp.sum(p, axis=-1, keepdims=True)
            o = _dot(p.astype(BF16), vv[b * qb:(b + 2) * qb]) / den
            lse = mx + jnp.log(den)
            o_ref[rows, cols] = jnp.where(lo, o[:qb], o[qb:])
            lse_ref[rows, cols] = jnp.where(lo, lse[:qb], lse[qb:])


def _attention(q, k, v):
    dil, length, _ = q.shape
    step = min(ATTN_STEP, length)
    n_sub = step // ATTN_BLOCK
    cur = pl.BlockSpec((None, step, ATTN_DIM), lambda r, n: (r, n, 0))
    prev = pl.BlockSpec((None, ATTN_BLOCK, ATTN_DIM), lambda r, n: (r, jnp.maximum(n * n_sub - 1, 0), 0))
    return pl.pallas_call(
        functools.partial(_attn_body, n_sub=n_sub),
        grid=(dil, length // step),
        in_specs=[cur, cur, cur, prev, prev],
        out_specs=[cur, cur],
        out_shape=[jax.ShapeDtypeStruct((dil, length, ATTN_DIM), F32)] * 2,
        compiler_params=_params("parallel", "arbitrary"),
        name="attention",
    )(q, k, v, k, v)


def _attn_dec_body(q_ref, kn_ref, vn_ref, c_ref, *rest, window, dil):
    o_ref, lse_ref, cout_ref = rest[-3:]
    lane = lax.broadcasted_iota(jnp.int32, (1, window), 1)
    valid = (lane & (dil - 1)) == 0
    last = lax.broadcasted_iota(jnp.int32, (HEAD_DIM, LANES), 1) == LANES - 1
    for b in range(q_ref.shape[0]):
        q = q_ref[b]
        kn = kn_ref[b]
        vn = vn_ref[b]
        new_rows = jnp.concatenate([kn, vn], axis=1)
        new_cols = jnp.transpose(jnp.broadcast_to(new_rows, (LANES, 2 * ATTN_DIM)))
        for hh in range(ATTN_HEADS):
            cols = slice(hh * HEAD_DIM, (hh + 1) * HEAD_DIM)
            qh = q[:, cols]
            k_t = c_ref[b, 0, hh]
            v_t = c_ref[b, 1, hh]
            q16 = jnp.broadcast_to(qh, (16, HEAD_DIM)).astype(BF16)
            s = _dot(q16, k_t.astype(BF16))[0:1, :] * ATTN_SCALE
            s = jnp.where(valid, s, NEG)
            s_new = jnp.sum(qh * kn[:, cols], axis=-1, keepdims=True) * ATTN_SCALE
            mx = jnp.maximum(jnp.max(s, axis=-1, keepdims=True), s_new)
            p = jnp.exp(s - mx)
            p_new = jnp.exp(s_new - mx)
            den = jnp.sum(p, axis=-1, keepdims=True) + p_new
            p16 = jnp.broadcast_to(p, (16, window)).astype(BF16)
            o = _dot_nt(p16, v_t.astype(BF16))[0:1, :] + p_new * vn[:, cols]
            o_ref[b, :, cols] = o / den
            lse_ref[b, :, cols] = jnp.broadcast_to(mx + jnp.log(den), (1, HEAD_DIM))
            for kv, t in ((0, k_t), (1, v_t)):
                shifted = pltpu.roll(t[:, window - LANES:window], LANES - 1, 1)
                col = new_cols[kv * ATTN_DIM + hh * HEAD_DIM:kv * ATTN_DIM + (hh + 1) * HEAD_DIM, :]
                cout_ref[b, kv, hh] = jnp.where(last, col, shifted)


def _attention_decode(q, kn, vn, cache_t, new_cache, layer, window, dil):
    b = q.shape[0]
    bs = max(1, min(DEC_BLOCK_SAMPLES, DEC_BLOCK_WINDOW // window))
    while b % bs:
        bs -= 1
    row = pl.BlockSpec((bs, 1, ATTN_DIM), lambda i: (i, 0, 0))
    cblk = pl.BlockSpec((None, bs, 2, ATTN_HEADS, HEAD_DIM, window), lambda i: (layer, i, 0, 0, 0, 0))
    tail = pl.BlockSpec((None, bs, 2, ATTN_HEADS, HEAD_DIM, LANES),
                        lambda i: (layer, i, 0, 0, 0, window // LANES - 1))
    r3 = lambda a: a.reshape(b, 1, ATTN_DIM)
    o, lse, cnew = pl.pallas_call(
        functools.partial(_attn_dec_body, window=window, dil=dil),
        grid=(b // bs,),
        in_specs=[row, row, row, cblk, pl.BlockSpec(memory_space=pl.ANY)],
        out_specs=[row, row, tail],
        out_shape=[jax.ShapeDtypeStruct((b, 1, ATTN_DIM), F32)] * 2 + [jax.ShapeDtypeStruct(cache_t.shape, F32)],
        input_output_aliases={4: 2},
        compiler_params=_params("parallel"),
        name="attention_decode",
    )(r3(q), r3(kn), r3(vn), cache_t, new_cache)
    return o.reshape(1, b, ATTN_DIM), lse.reshape(1, b, ATTN_DIM), cnew


def _from_streams(in_ref, scr_ref, dil):
    if dil == 1:
        return in_ref[0]
    n = in_ref.shape[1]
    for r in range(dil):
        for s in range(2):
            scr_ref[s, pl.ds(r, n, stride=dil), :] = in_ref[r, :, s * LANES:(s + 1) * LANES]
    return jnp.concatenate([scr_ref[0], scr_ref[1]], axis=1)


def _merge_body(x_ref, cy_ref, ry_ref, o0_ref, l0_ref, o1_ref, l1_ref, o2_ref, l2_ref, g_ref,
                wg_ref, wc_ref, wr_ref, wa_ref, wo_ref, *rest, dils, layout):
    n_side_in = sum(n_in for n_in, _, _ in layout)
    n_side_out = sum(n_out for _, n_out, _ in layout)
    out_ref = rest[n_side_in]
    scratch = rest[n_side_in + 1 + n_side_out:]
    _run_side(layout, rest[:n_side_in], rest[n_side_in + 1:n_side_in + 1 + n_side_out])
    x = x_ref[...]
    h = _rms(x, g_ref[2:3, :]).astype(BF16)
    scr = scratch[0] if scratch else None
    vals = []
    for j, ref in enumerate((o0_ref, l0_ref, o1_ref, l1_ref, o2_ref, l2_ref)):
        vals.append(_from_streams(ref, None if scr is None else scr.at[j], dils[j // 2]))
    o0, l0, o1, l1, o2, l2 = vals
    mx = jnp.maximum(jnp.maximum(l0, l1), l2)
    e0, e1, e2 = jnp.exp(l0 - mx), jnp.exp(l1 - mx), jnp.exp(l2 - mx)
    den = e0 + e1 + e2
    attn_y = ((e0 / den) * o0 + (e1 / den) * o1 + (e2 / den) * o2).astype(BF16)
    d = x.shape[1]
    merged = jax.nn.sigmoid(_dot(h, wg_ref[:, 0:d])) * _dot(cy_ref[...].astype(BF16), wc_ref[...])
    merged = merged + jax.nn.sigmoid(_dot(h, wg_ref[:, d:2 * d])) * _dot(ry_ref[...].astype(BF16), wr_ref[...])
    merged = merged + jax.nn.sigmoid(_dot(h, wg_ref[:, 2 * d:3 * d])) * _dot(attn_y, wa_ref[...])
    out_ref[...] = x + _rms(_dot(merged.astype(BF16), wo_ref[...]), g_ref[3:4, :])


def _merge(x, cy, ry, attn, dils, gains, wg, wc, wr, wa, wo, layer, tm, side=()):
    m, d = x.shape
    flat = [a for pair in attn for a in pair]
    a_specs = [_stream_spec(tm, dils[j // 2]) for j in range(6)]
    scratch = [pltpu.VMEM((6, 2, tm, LANES), F32)] if max(dils) > 1 else []
    args = [x, cy, ry, *flat, gains, wg, wc, wr, wa, wo]
    s_args, s_in, s_shape, s_out, aliases, layout = _side_operands(len(args), 1, side)
    outs = pl.pallas_call(
        functools.partial(_merge_body, dils=dils, layout=layout),
        grid=(m // tm,),
        in_specs=[_row_spec(tm, d), _row_spec(tm, CONV_DIM), _row_spec(tm, 512)] + a_specs
        + [_layer_spec(a.shape, layer) for a in (gains, wg, wc, wr, wa, wo)] + s_in,
        out_specs=[_row_spec(tm, d)] + s_out,
        out_shape=[jax.ShapeDtypeStruct((m, d), F32)] + s_shape,
        scratch_shapes=scratch,
        input_output_aliases=aliases,
        compiler_params=_params("arbitrary"),
        name="merge",
    )(*args, *s_args)
    return outs[0], outs[1:]


def _cos_sin(start, count, freq, split):
    if split is None or count % split:
        ang = (start + jnp.arange(count)).astype(F32)[:, None] * freq[None, :]
        return jnp.cos(ang), jnp.sin(ang)
    base = (start + split * jnp.arange(count // split)).astype(F32)[:, None, None] * freq
    off = jnp.arange(split).astype(F32)[None, :, None] * freq
    cb, sb, co, so = jnp.cos(base), jnp.sin(base), jnp.cos(off), jnp.sin(off)
    return (cb * co - sb * so).reshape(count, LANES), (sb * co + cb * so).reshape(count, LANES)


def _rotary_tables(start, count, split=None):
    lane = jnp.arange(LANES)
    half = RET_DK // 2
    freq = jnp.exp(-(lane % half).astype(F32) * (math.log(RET_THETA) / half))
    cos, sin = _cos_sin(start, count, freq, split)
    rc, rs = cos, jnp.where(lane < half, -sin, sin)
    dim = lane % HEAD_DIM
    half = ROT_DIM // 2
    freq = jnp.exp(-(dim % half).astype(F32) * (math.log(ROPE_THETA) / half))
    cos, sin = _cos_sin(start, count, freq, split)
    ac = jnp.where(dim < ROT_DIM, cos, 1.0)
    alo = jnp.where(dim < half, -sin, 0.0)
    ahi = jnp.where(jnp.logical_and(dim >= half, dim < ROT_DIM), sin, 0.0)
    return rc, rs, ac, alo, ahi


def _finish(x, ple, cy, ry, attn, dils, w, layer, tm, merge_side=(), ffn_side=()):
    x, merge_out = _merge(x, cy, ry, attn, dils, w['gains'], w['w_gate'], w['w_conv_out'], w['w_ret_out'],
                          w['w_attn_out'], w['w_o'], layer, tm, side=merge_side)
    x, ffn_out = _ffn(x, w['gains'], w['ffn2_gu'], w['ffn2_down'], layer, 4, 5, tm,
                      ple=(ple, w['w_ple_gate'], w['w_ple_proj']), side=ffn_side)
    return x, merge_out, ffn_out


def _prompt_layer(x, ple, tabs, caches_t, new_kv, w, layer):
    s = x.shape[0]
    tm = min(ROW_TILE, s)
    steps = s // tm
    x, (wide,) = _ffn(x, w['gains'], w['ffn1_gu'], w['ffn1_down'], layer, 0, 1, tm,
                      side=[_shift_job(caches_t[2], new_kv[2], layer, steps, 0)])
    hist = jnp.zeros((8, CONV_DIM), F32)
    outs = _proj(x, w['gains'], w['w_in'], w['w_conv'], w['b_conv'], tabs, hist, layer, tm, decode=False)
    cy, ulast, rq, rk, rv, rg = outs[:6]
    ry, ret_state = _retention(rq, rk, rv, rg, tm)
    attn, kv = [], []
    for gi, (window, dil) in enumerate(ATTN_GROUPS):
        aq, ak, av = outs[6 + 3 * gi:9 + 3 * gi]
        attn.append(_attention(aq, ak, av))
        keep = min(window, s) // dil
        tail = lambda a: jnp.swapaxes(a[:, s // dil - keep:], 0, 1).reshape(keep * dil, ATTN_DIM)
        kv.append(jnp.stack([tail(ak), tail(av)], axis=1).reshape(1, keep * dil, 2, ATTN_HEADS, HEAD_DIM))
    x, small, (wide,) = _finish(
        x, ple, cy, ry, attn, DILATIONS, w, layer, tm,
        merge_side=[_shift_job(caches_t[gi], new_kv[gi], layer, steps, None) for gi in (0, 1)],
        ffn_side=[_shift_job(caches_t[2], wide, layer, steps, 1)])
    return x, ulast[6:8][None], ret_state[None], kv, [small[0], small[1], wide]


def _sample_layer(x, ple, tabs, gdec, conv_hist, ret_state, caches_t, prev_ret, new_kv, w, layer):
    b = x.shape[0]
    x, _ = _ffn(x, w['gains'], w['ffn1_gu'], w['ffn1_down'], layer, 0, 1, b)
    outs = _proj(x, w['gains'], w['w_in'], w['w_conv'], w['b_conv'], tabs, conv_hist, layer, b, decode=True)
    cy, ulast, rq, rk, rv, rg = outs[:6]
    ry, ret_new = _retention_decode(rq, rk, rv, rg, gdec, ret_state, layer, prev_ret)
    attn, kv = [], []
    for gi, (window, dil) in enumerate(ATTN_GROUPS):
        aq, ak, av = outs[6 + 3 * gi:9 + 3 * gi]
        o, lse, cnew = _attention_decode(aq, ak, av, caches_t[gi], new_kv[gi], layer, window, dil)
        attn.append((o, lse))
        kv.append(cnew)
    x, _, _ = _finish(x, ple, cy, ry, attn, (1, 1, 1), w, layer, b)
    return x, jnp.swapaxes(ulast, 0, 1), ret_new, kv


def kernel(x_prompt, x_sample, state_conv, state_ret, cache_kv_w128, cache_kv_w512, cache_kv_w2048, p_prompt, p_sample, norm_gain, w_ffn1_gu, w_ffn1_down, w_in, w_conv, b_conv, w_conv_out, w_ret_out, w_attn_out, w_gate, w_o, w_ffn2_gu, w_ffn2_down, w_ple_gate, w_ple_proj):
    depth = norm_gain.shape[0]
    seq = x_prompt.shape[1]
    nb = x_sample.shape[0]
    assert x_prompt.shape[0] == 1 and x_sample.shape[1] == 1

    tabs_p = _rotary_tables(0, seq, split=ROT_SPLIT)
    tabs_s = tuple(jnp.broadcast_to(t, (nb, LANES)) for t in _rotary_tables(PAST_LEN, 1))
    gdec = jnp.broadcast_to(jnp.exp(_ret_log_decay())[:, None, None], (RET_HEADS, 1, RET_DK))
    caches_t = [jnp.transpose(c, (0, 1, 3, 4, 5, 2)) for c in (cache_kv_w128, cache_kv_w512, cache_kv_w2048)]
    bf = lambda a: a.astype(BF16)
    w = dict(
        gains=norm_gain, ffn1_gu=bf(w_ffn1_gu), ffn1_down=bf(w_ffn1_down), w_in=bf(w_in),
        w_conv=w_conv, b_conv=b_conv.reshape(depth, 1, CONV_DIM), w_conv_out=bf(w_conv_out),
        w_ret_out=bf(w_ret_out), w_attn_out=bf(w_attn_out), w_gate=bf(w_gate), w_o=bf(w_o),
        ffn2_gu=bf(w_ffn2_gu), ffn2_down=bf(w_ffn2_down), w_ple_gate=bf(w_ple_gate), w_ple_proj=bf(w_ple_proj))
    ple_p = p_prompt[:, 0]
    ple_s = p_sample[:, :, 0]
    conv_hist = jnp.swapaxes(state_conv, 1, 2)

    yp, ys = x_prompt[0], x_sample[:, 0]
    conv_p, conv_s, ret_p = [], [], []
    kv_p = [[] for _ in ATTN_GROUPS]
    ret_s, kv_s = None, [None] * len(ATTN_GROUPS)
    for l in range(depth):
        yp, cp, rp, kp, kv_s = _prompt_layer(yp, ple_p, tabs_p, caches_t, kv_s, w, l)
        ys, cs, ret_s, kv_s = _sample_layer(ys, ple_s, tabs_s, gdec, conv_hist, state_ret, caches_t, ret_s, kv_s,
                                            w, l)
        conv_p.append(cp)
        conv_s.append(cs)
        ret_p.append(rp)
        for gi in range(len(ATTN_GROUPS)):
            kv_p[gi].append(kp[gi])
    back = lambda c: jnp.transpose(c, (0, 1, 5, 2, 3, 4))
    return (yp[None], ys[:, None], jnp.stack(conv_p), jnp.stack(conv_s), jnp.stack(ret_p), ret_s,
            jnp.stack(kv_p[0]), back(kv_s[0]), jnp.stack(kv_p[1]), back(kv_s[1]),
            jnp.stack(kv_p[2]), back(kv_s[2]))
```

```python
import functools
import math

import jax
import jax.numpy as jnp
from jax import lax
from jax.experimental import pallas as pl
from jax.experimental.pallas import tpu as pltpu

F32 = jnp.float32
BF16 = jnp.bfloat16

EPS = 1e-6
PAST_LEN = 16384
CONV_DIM = 512
RET_HEADS = 4
RET_DK = 128
RET_THETA = 10000.0
ATTN_GROUPS = ((128, 1), (512, 4), (2048, 16))
DILATIONS = tuple(d for _, d in ATTN_GROUPS)
ATTN_HEADS = 4
HEAD_DIM = 64
ATTN_DIM = ATTN_HEADS * HEAD_DIM
ATTN_SCALE = HEAD_DIM ** -0.5
ROT_DIM = 16
ROPE_THETA = 500000.0
ROT_SPLIT = 128
ATTN_BLOCK = 128
ATTN_STEP = 512
DEC_BLOCK_WINDOW = 2048
DEC_BLOCK_SAMPLES = 8
NEG = -1e30
LANES = 128
MXU_WIDTH = 256
BF16_ROWS = 16
OFF_CONV = 0
OFF_RET = 3 * CONV_DIM
OFF_ATTN = OFF_RET + 4 * 512

VMEM_LIMIT_BYTES = 60000 * 1024
ROW_TILE = 512


def _params(*sem):
    return pltpu.CompilerParams(dimension_semantics=sem, vmem_limit_bytes=VMEM_LIMIT_BYTES)


def _const_spec(shape):
    zeros = (0,) * len(shape)
    return pl.BlockSpec(shape, lambda *_: zeros, pipeline_mode=pl.Buffered(1))


def _layer_spec(shape, layer):
    tail = tuple(shape[1:])
    idx = (layer,) + (0,) * len(tail)
    return pl.BlockSpec((None,) + tail, lambda *_: idx, pipeline_mode=pl.Buffered(1))


def _row_spec(tm, width):
    return pl.BlockSpec((tm, width), lambda i: (i, 0))


def _stream_spec(tm, dil):
    return pl.BlockSpec((dil, tm // dil, ATTN_DIM), lambda i: (0, i, 0))


def _rms(x, g):
    return x * lax.rsqrt(jnp.mean(x * x, axis=-1, keepdims=True) + EPS) * g


def _dot(a, b):
    return jnp.dot(a, b, preferred_element_type=F32)


def _dot_nt(a, b):
    return lax.dot_general(a, b, (((1,), (1,)), ((), ())), preferred_element_type=F32)


def _silu(x):
    return x * jax.nn.sigmoid(x)


def _ff_chunks(d_ff):
    first = -(-(d_ff // 2) // MXU_WIDTH) * MXU_WIDTH
    return ((0, first), (first, d_ff - first)) if 0 < first < d_ff else ((0, d_ff),)


def _side_operands(n_args, n_outs, side):
    args, in_specs, out_shape, out_specs, aliases, layout = [], [], [], [], {}, []
    for job in side:
        for src, dst in job['aliases'].items():
            aliases[n_args + len(args) + src] = n_outs + len(out_shape) + dst
        layout.append((len(job['args']), len(job['out_shape']), job['fn']))
        args += job['args']
        in_specs += job['in_specs']
        out_shape += job['out_shape']
        out_specs += job['out_specs']
    return args, in_specs, out_shape, out_specs, aliases, tuple(layout)


def _run_side(layout, in_refs, out_refs):
    i = o = 0
    for n_in, n_out, fn in layout:
        fn(in_refs[i:i + n_in], out_refs[o:o + n_out])
        i += n_in
        o += n_out


def _shift_job(cache_t, new_cache, compact, layer, n_steps, kv, dil):
    _, b, _, nh, hd, window = cache_t.shape
    n_pos = window // dil
    bs = -(-b // n_steps)
    assert b % bs == 0
    n_blk = b // bs
    n_kv, kv_idx = (2, 0) if kv is None else (1, kv)
    blk = lambda i: jnp.minimum(i, n_blk - 1)
    spec = pl.BlockSpec((None, bs, n_kv, nh, hd, window), lambda i: (layer, blk(i), kv_idx, 0, 0, 0))
    cspec = pl.BlockSpec((bs, n_kv, nh * hd, n_pos), lambda i: (blk(i), kv_idx, 0, 0))
    pos = jnp.arange(window)[:, None] == dil * jnp.arange(n_pos)[None, :]
    pick = pos.astype(BF16)

    def fn(in_refs, out_refs):
        src, pick_ref = in_refs[:2]
        dst, cdst = out_refs
        for s in range(bs):
            for j in range(n_kv):
                for hh in range(nh):
                    t = src[s, j, hh]
                    dst[s, j, hh] = pltpu.roll(t, window - 1, 1)
                    tb = t.astype(BF16)
                    if dil > 1:
                        tb = _dot(tb, pick_ref[...]).astype(BF16)
                    cdst[s, j, hh * hd:(hh + 1) * hd, :] = tb

    job = dict(args=[cache_t, pick], in_specs=[spec, _const_spec(pick.shape)],
               out_shape=[jax.ShapeDtypeStruct(cache_t.shape, cache_t.dtype),
                          jax.ShapeDtypeStruct((b, 2, nh * hd, n_pos), BF16)],
               out_specs=[spec, cspec], aliases={}, fn=fn)
    for out_idx, buf in enumerate((new_cache, compact)):
        if buf is not None:
            job['aliases'][len(job['args'])] = out_idx
            job['args'].append(buf)
            job['in_specs'].append(pl.BlockSpec(memory_space=pl.ANY))
    return job


def _convert_job(weights, layer, n_steps):
    job = dict(args=[], in_specs=[], out_shape=[], out_specs=[], aliases={})
    for wt in weights:
        _, k, n = wt.shape
        n_band = max(c for c in range(1, n_steps + 1)
                     if n_steps % c == 0 and k % c == 0 and ((k // c) % BF16_ROWS == 0 or c == 1))
        band = lambda i, n_band=n_band: i * n_band // n_steps
        job['args'].append(wt)
        job['in_specs'].append(pl.BlockSpec((None, k // n_band, n), lambda i, band=band: (layer, band(i), 0)))
        job['out_shape'].append(jax.ShapeDtypeStruct((k, n), BF16))
        job['out_specs'].append(pl.BlockSpec((k // n_band, n), lambda i, band=band: (band(i), 0)))

    def fn(in_refs, out_refs):
        for src, dst in zip(in_refs, out_refs):
            dst[...] = src[...].astype(BF16)

    job['fn'] = fn
    return job


def _ffn_body(x_ref, g_ref, wgu_ref, wd_ref, *rest, pre, post, d_ff, with_ple, layout):
    n_ple = 3 if with_ple else 0
    n_side_in = sum(n_in for n_in, _, _ in layout)
    o_ref = rest[n_ple + n_side_in]
    _run_side(layout, rest[n_ple:n_ple + n_side_in], rest[n_ple + n_side_in + 1:])
    x = x_ref[...]
    h = _rms(x, g_ref[pre:pre + 1, :]).astype(BF16)
    acc = None
    for start, size in _ff_chunks(d_ff):
        gate = _dot(h, wgu_ref[:, start:start + size])
        up = _dot(h, wgu_ref[:, d_ff + start:d_ff + start + size])
        a = (_silu(gate) * up).astype(BF16)
        y = _dot(a, wd_ref[start:start + size, :])
        acc = y if acc is None else acc + y
    x = x + 0.5 * _rms(acc, g_ref[post:post + 1, :])
    if with_ple:
        p_ref, wg_ref, wp_ref = rest[:3]
        h = _rms(x, g_ref[6:7, :]).astype(BF16)
        gate = jax.nn.sigmoid(_dot(h, wg_ref[...]))
        proj = _dot(p_ref[...].astype(BF16), wp_ref[...])
        x = x + _rms(gate * proj, g_ref[7:8, :])
    o_ref[...] = x


def _ffn(x, gains, wgu, wd, layer, pre, post, tm, ple=None, side=()):
    m, d = x.shape
    d_ff = wd.shape[0]
    args = [x, gains, wgu, wd]
    in_specs = [_row_spec(tm, d), _layer_spec(gains.shape, layer), _const_spec(wgu.shape), _const_spec(wd.shape)]
    if ple is not None:
        p, wg, wp = ple
        args += [p, wg, wp]
        in_specs += [pl.BlockSpec((None, tm, p.shape[2]), lambda i: (layer, i, 0)),
                     _const_spec(wg.shape), _const_spec(wp.shape)]
    s_args, s_in, s_shape, s_out, aliases, layout = _side_operands(len(args), 1, side)
    body = functools.partial(_ffn_body, pre=pre, post=post, d_ff=d_ff, with_ple=ple is not None, layout=layout)
    outs = pl.pallas_call(
        body,
        grid=(m // tm,),
        in_specs=in_specs + s_in,
        out_specs=[_row_spec(tm, d)] + s_out,
        out_shape=[jax.ShapeDtypeStruct((m, d), F32)] + s_shape,
        input_output_aliases=aliases,
        compiler_params=_params("arbitrary"),
        name="ffn",
    )(*args, *s_args)
    return outs[0], outs[1:]


def _rot_ret(x, cos, sin):
    return x * cos + pltpu.roll(x, RET_DK // 2, 1) * sin


def _rot_attn(x, cos, sin_lo, sin_hi):
    return x * cos + pltpu.roll(x, LANES - ROT_DIM // 2, 1) * sin_lo + pltpu.roll(x, ROT_DIM // 2, 1) * sin_hi


def _to_streams(halves, out_ref, scr_ref, dil):
    if dil == 1:
        for s, v in enumerate(halves):
            out_ref[0, :, s * LANES:(s + 1) * LANES] = v
        return
    n = halves[0].shape[0] // dil
    for s, v in enumerate(halves):
        scr_ref[s] = v
    for r in range(dil):
        for s in range(2):
            out_ref[r, :, s * LANES:(s + 1) * LANES] = scr_ref[s, pl.ds(r, n, stride=dil), :]


def _proj_body(*refs, dils):
    (x_ref, g_ref, w_ref, wc_ref, bc_ref, rc_ref, rs_ref, ac_ref, alo_ref, ahi_ref, hist_ref) = refs[:11]
    outs = refs[11:]
    cy_ref, ulast_ref, rq_ref, rk_ref, rv_ref, rg_ref = outs[:6]
    attn_refs = outs[6:15]
    decode = dils is None
    if not decode:
        carry_ref, scr_ref = outs[15:17]

    x = x_ref[...]
    h = _rms(x, g_ref[2:3, :]).astype(BF16)

    def part(off, width):
        return _dot(h, w_ref[:, off:off + width])


    acos, alo, ahi = ac_ref[...], alo_ref[...], ahi_ref[...]
    for gi in reversed(range(len(ATTN_GROUPS))):
        base = OFF_ATTN + gi * 3 * ATTN_DIM
        zq = part(base, ATTN_DIM)
        zk = part(base + ATTN_DIM, ATTN_DIM)
        zv = part(base + 2 * ATTN_DIM, ATTN_DIM)
        halves = lambda z: [z[:, s * LANES:(s + 1) * LANES] for s in range(2)]
        q_h = [_rot_attn(v, acos, alo, ahi) for v in halves(zq)]
        k_h = [_rot_attn(v, acos, alo, ahi) for v in halves(zk)]
        for j, vals in enumerate((q_h, k_h, halves(zv))):
            out_ref = attn_refs[3 * gi + j]
            if decode:
                for s, v in enumerate(vals):
                    out_ref[:, s * LANES:(s + 1) * LANES] = v
            else:
                _to_streams(vals, out_ref, scr_ref.at[3 * gi + j], dils[gi])

    b_gate = part(OFF_CONV, CONV_DIM)
    u = part(OFF_CONV + CONV_DIM, CONV_DIM) * part(OFF_CONV + 2 * CONV_DIM, CONV_DIM)
    if decode:
        u2 = hist_ref[0]
        u1 = hist_ref[1]
        ulast_ref[0] = u1
        ulast_ref[1] = u
    else:
        tm = u.shape[0]

        @pl.when(pl.program_id(0) == 0)
        def _():
            carry_ref[...] = hist_ref[...]

        carry = carry_ref[...]
        row = lax.broadcasted_iota(jnp.int32, u.shape, 0)
        u1 = jnp.where(row == 0, carry[7:8, :], pltpu.roll(u, 1, 0))
        u2 = jnp.where(row == 0, carry[6:7, :], jnp.where(row == 1, carry[7:8, :], pltpu.roll(u, 2, 0)))
        carry_ref[...] = u[tm - 8:tm, :]
        ulast_ref[...] = u[tm - 8:tm, :]
    y = bc_ref[...] + wc_ref[0:1, :] * u2
    y = y + wc_ref[1:2, :] * u1
    y = y + wc_ref[2:3, :] * u
    cy_ref[...] = (b_gate * y).astype(cy_ref.dtype)

    cos, sin = rc_ref[...], rs_ref[...]
    zq = part(OFF_RET, 512)
    zk = part(OFF_RET + 512, 512)
    for hh in range(RET_HEADS):
        cols = slice(hh * RET_DK, (hh + 1) * RET_DK)
        rq_ref[:, cols] = _rot_ret(zq[:, cols], cos, sin)
        rk_ref[:, cols] = _rot_ret(zk[:, cols], cos, sin) * (RET_DK ** -0.5)
    rv_ref[...] = part(OFF_RET + 1024, 512)
    rg_ref[...] = part(OFF_RET + 1536, 512)


def _proj(x, gains, w_in, w_conv, b_conv, tabs, hist, layer, tm, decode):
    m, d = x.shape
    rc, rs, ac, alo, ahi = tabs
    n = m // tm
    if decode:
        dils = None
        hist_spec = pl.BlockSpec((None, 2, tm, CONV_DIM), lambda i: (layer, 0, i, 0))
        ulast_spec = pl.BlockSpec((2, tm, CONV_DIM), lambda i: (0, i, 0))
        ulast_shape = jax.ShapeDtypeStruct((2, m, CONV_DIM), F32)
        attn_specs = [_row_spec(tm, ATTN_DIM)] * 9
        attn_shapes = [jax.ShapeDtypeStruct((m, ATTN_DIM), F32)] * 9
        scratch = []
    else:
        dils = DILATIONS
        hist_spec = _const_spec((8, CONV_DIM))
        ulast_spec = pl.BlockSpec((8, CONV_DIM), lambda i: (0, 0))
        ulast_shape = jax.ShapeDtypeStruct((8, CONV_DIM), F32)
        attn_specs = [_stream_spec(tm, dl) for dl in dils for _ in range(3)]
        attn_shapes = [jax.ShapeDtypeStruct((dl, m // dl, ATTN_DIM), F32) for dl in dils for _ in range(3)]
        scratch = [pltpu.VMEM((8, CONV_DIM), F32), pltpu.VMEM((9, 2, tm, LANES), F32)]
    in_specs = [
        _row_spec(tm, d), _layer_spec(gains.shape, layer), _const_spec(w_in.shape),
        _layer_spec(w_conv.shape, layer), _layer_spec(b_conv.shape, layer),
        _row_spec(tm, LANES), _row_spec(tm, LANES), _row_spec(tm, LANES), _row_spec(tm, LANES), _row_spec(tm, LANES),
        hist_spec,
    ]
    out_specs = [_row_spec(tm, CONV_DIM), ulast_spec] + [_row_spec(tm, 512)] * 4 + attn_specs
    out_shape = ([jax.ShapeDtypeStruct((m, CONV_DIM), BF16), ulast_shape]
                 + [jax.ShapeDtypeStruct((m, 512), F32)] * 4 + attn_shapes)
    return pl.pallas_call(
        functools.partial(_proj_body, dils=dils),
        grid=(n,),
        in_specs=in_specs,
        out_specs=out_specs,
        out_shape=out_shape,
        scratch_shapes=scratch,
        compiler_params=_params("arbitrary"),
        name="proj",
    )(x, gains, w_in, w_conv, b_conv, rc, rs, ac, alo, ahi, hist)


def _head_norm(o):
    mu = jnp.mean(o, axis=-1, keepdims=True)
    var = jnp.mean(jnp.square(o - mu), axis=-1, keepdims=True)
    return (o - mu) * lax.rsqrt(var + EPS)


def _ret_body(q_ref, k_ref, v_ref, g_ref, decay_ref, qw_ref, kw_ref, gch_ref, y_ref, sout_ref, s_scr):
    @pl.when(pl.program_id(0) == 0)
    def _():
        s_scr[...] = jnp.zeros_like(s_scr)

    for hh in range(RET_HEADS):
        cols = slice(hh * RET_DK, (hh + 1) * RET_DK)
        q = q_ref[:, cols]
        k = k_ref[:, cols]
        v = v_ref[:, cols].astype(BF16)
        s = s_scr[hh]
        scores = _dot_nt(q.astype(BF16), k.astype(BF16)) * decay_ref[hh]
        o = _dot(scores.astype(BF16), v) + _dot((q * qw_ref[hh]).astype(BF16), s.astype(BF16))
        kv = _dot(jnp.transpose(k * kw_ref[hh]).astype(BF16), v)
        s_scr[hh] = s * gch_ref[hh] + kv
        y_ref[:, cols] = (_head_norm(o) * _silu(g_ref[:, cols])).astype(y_ref.dtype)

    @pl.when(pl.program_id(0) == pl.num_programs(0) - 1)
    def _():
        sout_ref[...] = s_scr[...]


def _ret_log_decay():
    return jnp.log1p(-jnp.exp2(-5.0 - jnp.arange(RET_HEADS, dtype=F32)))


def _ret_tables(chunk):
    i = jnp.arange(chunk, dtype=F32)
    log_g = _ret_log_decay()
    diff = i[:, None] - i[None, :]
    decay = jnp.where(diff[None] >= 0, jnp.exp(jnp.maximum(diff, 0.0)[None] * log_g[:, None, None]), 0.0)
    k_w = jnp.exp((chunk - 1 - i)[:, None] * log_g[None, :])
    q_w = jnp.exp((i + 1)[:, None] * log_g[None, :])
    g_chunk = jnp.exp(chunk * log_g)
    qw = jnp.broadcast_to(q_w.T[:, :, None], (RET_HEADS, chunk, RET_DK))
    kw = jnp.broadcast_to(k_w.T[:, :, None], (RET_HEADS, chunk, RET_DK))
    gch = jnp.broadcast_to(g_chunk[:, None, None], (RET_HEADS, RET_DK, RET_DK))
    return decay, qw, kw, gch


def _retention(rq, rk, rv, rg, tr):
    m = rq.shape[0]
    decay, qw, kw, gch = _ret_tables(tr)
    return pl.pallas_call(
        _ret_body,
        grid=(m // tr,),
        in_specs=[_row_spec(tr, 512)] * 4 + [_const_spec(a.shape) for a in (decay, qw, kw, gch)],
        out_specs=[_row_spec(tr, 512), pl.BlockSpec((RET_HEADS, RET_DK, RET_DK), lambda i: (0, 0, 0))],
        out_shape=[jax.ShapeDtypeStruct((m, 512), BF16), jax.ShapeDtypeStruct((RET_HEADS, RET_DK, RET_DK), F32)],
        scratch_shapes=[pltpu.VMEM((RET_HEADS, RET_DK, RET_DK), F32)],
        compiler_params=_params("arbitrary"),
        name="retention",
    )(rq, rk, rv, rg, decay, qw, kw, gch)


def _ret_dec_body(q_ref, k_ref, v_ref, g_ref, gd_ref, s0_ref, *rest):
    y_ref, sout_ref = rest[-2:]
    for hh in range(RET_HEADS):
        cols = slice(hh * RET_DK, (hh + 1) * RET_DK)
        q = q_ref[0, :, cols]
        k = k_ref[0, :, cols]
        v = v_ref[0, :, cols]
        gd = gd_ref[hh]
        s0 = s0_ref[0, hh]
        wide = (RET_DK, RET_DK)
        q_col = jnp.transpose(jnp.broadcast_to(q * gd, wide))
        k_col = jnp.transpose(jnp.broadcast_to(k, wide))
        o_inter = jnp.sum(q_col * s0, axis=0, keepdims=True)
        o_intra = jnp.sum(q * k, axis=-1, keepdims=True) * v
        sout_ref[0, hh] = s0 * gd + k_col * v
        o = o_intra + o_inter
        y_ref[0, :, cols] = _head_norm(o) * _silu(g_ref[0, :, cols])


def _retention_decode(rq, rk, rv, rg, gdec, state, layer, prev_out):
    b = rq.shape[0]
    row = pl.BlockSpec((1, 1, 512), lambda i: (i, 0, 0))
    st = pl.BlockSpec((None, 1, RET_HEADS, RET_DK, RET_DK), lambda i: (layer, i, 0, 0, 0))
    r3 = lambda a: a.reshape(b, 1, 512)
    args = [r3(rq), r3(rk), r3(rv), r3(rg), gdec, state]
    in_specs = [row, row, row, row, _const_spec(gdec.shape), st]
    aliases = {}
    if prev_out is not None:
        args.append(prev_out)
        in_specs.append(pl.BlockSpec(memory_space=pl.ANY))
        aliases = {6: 1}
    y, s_new = pl.pallas_call(
        _ret_dec_body,
        grid=(b,),
        in_specs=in_specs,
        out_specs=[row, st],
        out_shape=[jax.ShapeDtypeStruct((b, 1, 512), F32), jax.ShapeDtypeStruct(state.shape, F32)],
        input_output_aliases=aliases,
        compiler_params=_params("parallel"),
        name="retention_decode",
    )(*args)
    return y.reshape(b, 512), s_new


def _attn_body(q_ref, kc_ref, vc_ref, kp_ref, vp_ref, o_ref, lse_ref, *, n_sub):
    n = pl.program_id(1)
    qb = ATTN_BLOCK
    ii = lax.broadcasted_iota(jnp.int32, (2 * qb, 2 * qb), 0) & (qb - 1)
    jj = lax.broadcasted_iota(jnp.int32, (2 * qb, 2 * qb), 1)
    band = jnp.logical_and(jj >= ii, jj <= ii + qb)
    first = jnp.logical_and(band, jnp.logical_or(jj >= qb, n > 0))
    lo = lax.broadcasted_iota(jnp.int32, (qb, LANES), 1) < HEAD_DIM
    for hp in range(ATTN_HEADS // 2):
        cols = slice(hp * LANES, (hp + 1) * LANES)
        kk = jnp.concatenate([kp_ref[:, cols], kc_ref[:, cols]], axis=0).astype(BF16)
        vv = jnp.concatenate([vp_ref[:, cols], vc_ref[:, cols]], axis=0).astype(BF16)
        for b in range(n_sub):
            rows = slice(b * qb, (b + 1) * qb)
            q2 = q_ref[rows, cols] * ATTN_SCALE
            zero = jnp.zeros_like(q2)
            qs = jnp.concatenate([jnp.where(lo, q2, zero), jnp.where(lo, zero, q2)], axis=0).astype(BF16)
            s = jnp.where(first if b == 0 else band, _dot_nt(qs, kk[b * qb:(b + 2) * qb]), NEG)
            mx = jnp.max(s, axis=-1, keepdims=True)
            p = jnp.exp(s - mx)
            den = jnp.sum(p, axis=-1, keepdims=True)
            o = _dot(p.astype(BF16), vv[b * qb:(b + 2) * qb]) / den
            lse = mx + jnp.log(den)
            o_ref[rows, cols] = jnp.where(lo, o[:qb], o[qb:])
            lse_ref[rows, cols] = jnp.where(lo, lse[:qb], lse[qb:])


def _attention(q, k, v):
    dil, length, _ = q.shape
    step = min(ATTN_STEP, length)
    n_sub = step // ATTN_BLOCK
    cur = pl.BlockSpec((None, step, ATTN_DIM), lambda r, n: (r, n, 0))
    prev = pl.BlockSpec((None, ATTN_BLOCK, ATTN_DIM), lambda r, n: (r, jnp.maximum(n * n_sub - 1, 0), 0))
    return pl.pallas_call(
        functools.partial(_attn_body, n_sub=n_sub),
        grid=(dil, length // step),
        in_specs=[cur, cur, cur, prev, prev],
        out_specs=[cur, cur],
        out_shape=[jax.ShapeDtypeStruct((dil, length, ATTN_DIM), F32)] * 2,
        compiler_params=_params("parallel", "arbitrary"),
        name="attention",
    )(q, k, v, k, v)


def _attn_dec_body(q_ref, kn_ref, vn_ref, kc_ref, tail_ref, new_ref, o_ref, lse_ref, cout_ref):
    del new_ref
    rows = BF16_ROWS
    head_of_lane = lax.broadcasted_iota(jnp.int32, (rows, ATTN_DIM), 1) // HEAD_DIM
    own = head_of_lane == lax.broadcasted_iota(jnp.int32, (rows, ATTN_DIM), 0)
    last = lax.broadcasted_iota(jnp.int32, (HEAD_DIM, LANES), 1) == LANES - 1
    wide = lambda a: jnp.broadcast_to(a, (rows, ATTN_DIM))
    pick = lambda a: jnp.sum(jnp.where(own, a, 0.0), axis=0, keepdims=True)
    for b in range(q_ref.shape[0]):
        q = q_ref[b]
        kn = kn_ref[b]
        vn = vn_ref[b]
        q_rows = jnp.where(own, wide(q), 0.0)
        s = _dot(q_rows.astype(BF16), kc_ref[b, 0]) * ATTN_SCALE
        s_new = jnp.sum(q_rows * wide(kn), axis=-1, keepdims=True) * ATTN_SCALE
        mx = jnp.maximum(jnp.max(s, axis=-1, keepdims=True), s_new)
        p = jnp.exp(s - mx)
        p_new = jnp.exp(s_new - mx)
        den = jnp.sum(p, axis=-1, keepdims=True) + p_new
        o = (_dot_nt(p.astype(BF16), kc_ref[b, 1]) + p_new * wide(vn)) / den
        o_ref[b] = pick(o)
        lse_ref[b] = pick(wide(mx + jnp.log(den)))
        new_rows = jnp.concatenate([kn, vn], axis=1)
        new_cols = jnp.transpose(jnp.broadcast_to(new_rows, (LANES, 2 * ATTN_DIM)))
        for kv in range(2):
            for hh in range(ATTN_HEADS):
                col = new_cols[kv * ATTN_DIM + hh * HEAD_DIM:kv * ATTN_DIM + (hh + 1) * HEAD_DIM, :]
                cout_ref[b, kv, hh] = jnp.where(last, col, pltpu.roll(tail_ref[b, kv, hh], LANES - 1, 1))


def _attention_decode(q, kn, vn, compact, cache_t, new_cache, layer):
    b = q.shape[0]
    window = cache_t.shape[-1]
    bs = DEC_BLOCK_SAMPLES
    while b % bs:
        bs -= 1
    row = pl.BlockSpec((bs, 1, ATTN_DIM), lambda i: (i, 0, 0))
    kc = pl.BlockSpec((bs,) + compact.shape[1:], lambda i: (i, 0, 0, 0))
    tail = pl.BlockSpec((None, bs, 2, ATTN_HEADS, HEAD_DIM, LANES),
                        lambda i: (layer, i, 0, 0, 0, window // LANES - 1))
    r3 = lambda a: a.reshape(b, 1, ATTN_DIM)
    o, lse, cnew = pl.pallas_call(
        _attn_dec_body,
        grid=(b // bs,),
        in_specs=[row, row, row, kc, tail, pl.BlockSpec(memory_space=pl.ANY)],
        out_specs=[row, row, tail],
        out_shape=[jax.ShapeDtypeStruct((b, 1, ATTN_DIM), F32)] * 2 + [jax.ShapeDtypeStruct(cache_t.shape, F32)],
        input_output_aliases={5: 2},
        compiler_params=_params("parallel"),
        name="attention_decode",
    )(r3(q), r3(kn), r3(vn), compact, cache_t, new_cache)
    return o.reshape(1, b, ATTN_DIM), lse.reshape(1, b, ATTN_DIM), cnew


def _from_streams(in_ref, scr_ref, dil):
    if dil == 1:
        return in_ref[0]
    n = in_ref.shape[1]
    for r in range(dil):
        for s in range(2):
            scr_ref[s, pl.ds(r, n, stride=dil), :] = in_ref[r, :, s * LANES:(s + 1) * LANES]
    return jnp.concatenate([scr_ref[0], scr_ref[1]], axis=1)


def _merge_body(x_ref, cy_ref, ry_ref, o0_ref, l0_ref, o1_ref, l1_ref, o2_ref, l2_ref, g_ref,
                wg_ref, wc_ref, wr_ref, wa_ref, wo_ref, *rest, dils, layout):
    n_side_in = sum(n_in for n_in, _, _ in layout)
    n_side_out = sum(n_out for _, n_out, _ in layout)
    out_ref = rest[n_side_in]
    scratch = rest[n_side_in + 1 + n_side_out:]
    _run_side(layout, rest[:n_side_in], rest[n_side_in + 1:n_side_in + 1 + n_side_out])
    x = x_ref[...]
    h = _rms(x, g_ref[2:3, :]).astype(BF16)
    scr = scratch[0] if scratch else None
    vals = []
    for j, ref in enumerate((o0_ref, l0_ref, o1_ref, l1_ref, o2_ref, l2_ref)):
        vals.append(_from_streams(ref, None if scr is None else scr.at[j], dils[j // 2]))
    o0, l0, o1, l1, o2, l2 = vals
    mx = jnp.maximum(jnp.maximum(l0, l1), l2)
    e0, e1, e2 = jnp.exp(l0 - mx), jnp.exp(l1 - mx), jnp.exp(l2 - mx)
    den = e0 + e1 + e2
    attn_y = ((e0 / den) * o0 + (e1 / den) * o1 + (e2 / den) * o2).astype(BF16)
    d = x.shape[1]
    merged = jax.nn.sigmoid(_dot(h, wg_ref[:, 0:d])) * _dot(cy_ref[...].astype(BF16), wc_ref[...])
    merged = merged + jax.nn.sigmoid(_dot(h, wg_ref[:, d:2 * d])) * _dot(ry_ref[...].astype(BF16), wr_ref[...])
    merged = merged + jax.nn.sigmoid(_dot(h, wg_ref[:, 2 * d:3 * d])) * _dot(attn_y, wa_ref[...])
    out_ref[...] = x + _rms(_dot(merged.astype(BF16), wo_ref[...]), g_ref[3:4, :])


def _merge(x, cy, ry, attn, dils, gains, wg, wc, wr, wa, wo, layer, tm, side=()):
    m, d = x.shape
    flat = [a for pair in attn for a in pair]
    a_specs = [_stream_spec(tm, dils[j // 2]) for j in range(6)]
    scratch = [pltpu.VMEM((6, 2, tm, LANES), F32)] if max(dils) > 1 else []
    args = [x, cy, ry, *flat, gains, wg, wc, wr, wa, wo]
    s_args, s_in, s_shape, s_out, aliases, layout = _side_operands(len(args), 1, side)
    outs = pl.pallas_call(
        functools.partial(_merge_body, dils=dils, layout=layout),
        grid=(m // tm,),
        in_specs=[_row_spec(tm, d), _row_spec(tm, CONV_DIM), _row_spec(tm, 512)] + a_specs
        + [_layer_spec(gains.shape, layer)] + [_const_spec(a.shape) for a in (wg, wc, wr, wa, wo)] + s_in,
        out_specs=[_row_spec(tm, d)] + s_out,
        out_shape=[jax.ShapeDtypeStruct((m, d), F32)] + s_shape,
        scratch_shapes=scratch,
        input_output_aliases=aliases,
        compiler_params=_params("arbitrary"),
        name="merge",
    )(*args, *s_args)
    return outs[0], outs[1:]


def _cos_sin(start, count, freq, split):
    if split is None or count % split:
        ang = (start + jnp.arange(count)).astype(F32)[:, None] * freq[None, :]
        return jnp.cos(ang), jnp.sin(ang)
    base = (start + split * jnp.arange(count // split)).astype(F32)[:, None, None] * freq
    off = jnp.arange(split).astype(F32)[None, :, None] * freq
    cb, sb, co, so = jnp.cos(base), jnp.sin(base), jnp.cos(off), jnp.sin(off)
    return (cb * co - sb * so).reshape(count, LANES), (sb * co + cb * so).reshape(count, LANES)


def _rotary_tables(start, count, split=None):
    lane = jnp.arange(LANES)
    half = RET_DK // 2
    freq = jnp.exp(-(lane % half).astype(F32) * (math.log(RET_THETA) / half))
    cos, sin = _cos_sin(start, count, freq, split)
    rc, rs = cos, jnp.where(lane < half, -sin, sin)
    dim = lane % HEAD_DIM
    half = ROT_DIM // 2
    freq = jnp.exp(-(dim % half).astype(F32) * (math.log(ROPE_THETA) / half))
    cos, sin = _cos_sin(start, count, freq, split)
    ac = jnp.where(dim < ROT_DIM, cos, 1.0)
    alo = jnp.where(dim < half, -sin, 0.0)
    ahi = jnp.where(jnp.logical_and(dim >= half, dim < ROT_DIM), sin, 0.0)
    return rc, rs, ac, alo, ahi


def _finish(x, ple, cy, ry, attn, dils, w, layer, tm, merge_side=(), ffn_side=()):
    x, merge_out = _merge(x, cy, ry, attn, dils, w['gains'], w['w_gate'], w['w_conv_out'], w['w_ret_out'],
                          w['w_attn_out'], w['w_o'], layer, tm, side=merge_side)
    x, ffn_out = _ffn(x, w['gains'], w['ffn2_gu'], w['ffn2_down'], layer, 4, 5, tm,
                      ple=(ple, w['w_ple_gate'], w['w_ple_proj']), side=ffn_side)
    return x, merge_out, ffn_out


W_FFN1 = ('ffn1_gu', 'ffn1_down')
W_EARLY = W_FFN1 + ('w_in',)
W_LATE = ('w_gate', 'w_conv_out', 'w_ret_out', 'w_attn_out', 'w_o', 'ffn2_gu', 'ffn2_down', 'w_ple_gate',
          'w_ple_proj')


def _prompt_layer(x, ple, tabs, caches_t, new_kv, raw, w, layer):
    depth = raw['w_in'].shape[0]
    s = x.shape[0]
    tm = min(ROW_TILE, s)
    steps = s // tm
    wide_dil = DILATIONS[2]
    todo = tuple(n for n in W_EARLY[2:] + W_LATE if n not in w)
    side = [_shift_job(caches_t[2], new_kv[2], None, layer, steps, 0, wide_dil)]
    if todo:
        side.append(_convert_job([raw[n] for n in todo], layer, steps))
    x, outs = _ffn(x, w['gains'], w['ffn1_gu'], w['ffn1_down'], layer, 0, 1, tm, side=side)
    wide, wide_c = outs[:2]
    w = dict(w, **dict(zip(todo, outs[2:])))
    hist = jnp.zeros((8, CONV_DIM), F32)
    outs = _proj(x, w['gains'], w['w_in'], w['w_conv'], w['b_conv'], tabs, hist, layer, tm, decode=False)
    cy, ulast, rq, rk, rv, rg = outs[:6]
    ry, ret_state = _retention(rq, rk, rv, rg, tm)
    attn, kv = [], []
    for gi, (window, dil) in enumerate(ATTN_GROUPS):
        aq, ak, av = outs[6 + 3 * gi:9 + 3 * gi]
        attn.append(_attention(aq, ak, av))
        keep = min(window, s) // dil
        tail = lambda a: jnp.swapaxes(a[:, s // dil - keep:], 0, 1).reshape(keep * dil, ATTN_DIM)
        kv.append(jnp.stack([tail(ak), tail(av)], axis=1).reshape(1, keep * dil, 2, ATTN_HEADS, HEAD_DIM))
    merge_side = [_shift_job(caches_t[gi], new_kv[gi], None, layer, steps, None, DILATIONS[gi]) for gi in (0, 1)]
    ffn_side = [_shift_job(caches_t[2], wide, wide_c, layer, steps, 1, wide_dil)]
    if layer + 1 < depth:
        merge_side.append(_convert_job([raw[n] for n in W_EARLY], layer + 1, steps))
        ffn_side.append(_convert_job([raw[n] for n in W_LATE], layer + 1, steps))
    x, m_out, f_out = _finish(x, ple, cy, ry, attn, DILATIONS, w, layer, tm, merge_side, ffn_side)
    w_next = dict(zip(W_EARLY, m_out[4:]), **dict(zip(W_LATE, f_out[2:])))
    new_kv = [m_out[0], m_out[2], f_out[0]]
    compact = [m_out[1], m_out[3], f_out[1]]
    return x, ulast[6:8][None], ret_state[None], kv, new_kv, compact, w, w_next


def _sample_layer(x, ple, tabs, gdec, conv_hist, ret_state, caches_t, prev_ret, new_kv, compact, w, layer):
    b = x.shape[0]
    x, _ = _ffn(x, w['gains'], w['ffn1_gu'], w['ffn1_down'], layer, 0, 1, b)
    outs = _proj(x, w['gains'], w['w_in'], w['w_conv'], w['b_conv'], tabs, conv_hist, layer, b, decode=True)
    cy, ulast, rq, rk, rv, rg = outs[:6]
    ry, ret_new = _retention_decode(rq, rk, rv, rg, gdec, ret_state, layer, prev_ret)
    attn, kv = [], []
    for gi, (window, dil) in enumerate(ATTN_GROUPS):
        aq, ak, av = outs[6 + 3 * gi:9 + 3 * gi]
        o, lse, cnew = _attention_decode(aq, ak, av, compact[gi], caches_t[gi], new_kv[gi], layer)
        attn.append((o, lse))
        kv.append(cnew)
    x, _, _ = _finish(x, ple, cy, ry, attn, (1, 1, 1), w, layer, b)
    return x, jnp.swapaxes(ulast, 0, 1), ret_new, kv


def kernel(x_prompt, x_sample, state_conv, state_ret, cache_kv_w128, cache_kv_w512, cache_kv_w2048, p_prompt, p_sample, norm_gain, w_ffn1_gu, w_ffn1_down, w_in, w_conv, b_conv, w_conv_out, w_ret_out, w_attn_out, w_gate, w_o, w_ffn2_gu, w_ffn2_down, w_ple_gate, w_ple_proj):
    depth = norm_gain.shape[0]
    seq = x_prompt.shape[1]
    nb = x_sample.shape[0]
    assert x_prompt.shape[0] == 1 and x_sample.shape[1] == 1

    tabs_p = _rotary_tables(0, seq, split=ROT_SPLIT)
    tabs_s = tuple(jnp.broadcast_to(t, (nb, LANES)) for t in _rotary_tables(PAST_LEN, 1))
    gdec = jnp.broadcast_to(jnp.exp(_ret_log_decay())[:, None, None], (RET_HEADS, 1, RET_DK))
    caches_t = [jnp.transpose(c, (0, 1, 3, 4, 5, 2)) for c in (cache_kv_w128, cache_kv_w512, cache_kv_w2048)]
    raw = dict(ffn1_gu=w_ffn1_gu, ffn1_down=w_ffn1_down, w_in=w_in, w_conv_out=w_conv_out, w_ret_out=w_ret_out,
               w_attn_out=w_attn_out, w_gate=w_gate, w_o=w_o, ffn2_gu=w_ffn2_gu, ffn2_down=w_ffn2_down,
               w_ple_gate=w_ple_gate, w_ple_proj=w_ple_proj)
    shared = dict(gains=norm_gain, w_conv=w_conv, b_conv=b_conv.reshape(depth, 1, CONV_DIM))
    w_next = {n: raw[n][0].astype(BF16) for n in W_FFN1}
    ple_p = p_prompt[:, 0]
    ple_s = p_sample[:, :, 0]
    conv_hist = jnp.swapaxes(state_conv, 1, 2)

    yp, ys = x_prompt[0], x_sample[:, 0]
    conv_p, conv_s, ret_p = [], [], []
    kv_p = [[] for _ in ATTN_GROUPS]
    ret_s, kv_s = None, [None] * len(ATTN_GROUPS)
    for l in range(depth):
        yp, cp, rp, kp, kv_s, compact, w, w_next = _prompt_layer(yp, ple_p, tabs_p, caches_t, kv_s, raw,
                                                                 dict(shared, **w_next), l)
        ys, cs, ret_s, kv_s = _sample_layer(ys, ple_s, tabs_s, gdec, conv_hist, state_ret, caches_t, ret_s, kv_s,
                                            compact, w, l)
        conv_p.append(cp)
        conv_s.append(cs)
        ret_p.append(rp)
        for gi in range(len(ATTN_GROUPS)):
            kv_p[gi].append(kp[gi])
    back = lambda c: jnp.transpose(c, (0, 1, 5, 2, 3, 4))
    return (yp[None], ys[:, None], jnp.stack(conv_p), jnp.stack(conv_s), jnp.stack(ret_p), ret_s,
            jnp.stack(kv_p[0]), back(kv_s[0]), jnp.stack(kv_p[1]), back(kv_s[1]),
            jnp.stack(kv_p[2]), back(kv_s[2]))
```

```python
import functools
import math

import jax
import jax.numpy as jnp
from jax import lax
from jax.experimental import pallas as pl
from jax.experimental.pallas import tpu as pltpu

F32 = jnp.float32
BF16 = jnp.bfloat16

EPS = 1e-6
PAST_LEN = 16384
CONV_DIM = 512
RET_HEADS = 4
RET_DK = 128
RET_THETA = 10000.0
ATTN_GROUPS = ((128, 1), (512, 4), (2048, 16))
DILATIONS = tuple(d for _, d in ATTN_GROUPS)
ATTN_HEADS = 4
HEAD_DIM = 64
ATTN_DIM = ATTN_HEADS * HEAD_DIM
ATTN_SCALE = HEAD_DIM ** -0.5
ROT_DIM = 16
ROPE_THETA = 500000.0
ROT_SPLIT = 128
ATTN_BLOCK = 128
ATTN_STEP = 1024
DEC_BLOCK_WINDOW = 2048
DEC_BLOCK_SAMPLES = 8
NEG = -1e30
LANES = 128
MXU_WIDTH = 256
BF16_ROWS = 16
OFF_CONV = 0
OFF_RET = 3 * CONV_DIM
OFF_ATTN = OFF_RET + 4 * 512

VMEM_LIMIT_BYTES = 60000 * 1024
ROW_TILE = 512


def _params(*sem):
    return pltpu.CompilerParams(dimension_semantics=sem, vmem_limit_bytes=VMEM_LIMIT_BYTES)


def _const_spec(shape):
    zeros = (0,) * len(shape)
    return pl.BlockSpec(shape, lambda *_: zeros, pipeline_mode=pl.Buffered(1))


def _layer_spec(shape, layer):
    tail = tuple(shape[1:])
    idx = (layer,) + (0,) * len(tail)
    return pl.BlockSpec((None,) + tail, lambda *_: idx, pipeline_mode=pl.Buffered(1))


def _row_spec(tm, width):
    return pl.BlockSpec((tm, width), lambda i: (i, 0))


def _rms(x, g):
    return x * lax.rsqrt(jnp.mean(x * x, axis=-1, keepdims=True) + EPS) * g


def _dot(a, b):
    return jnp.dot(a, b, preferred_element_type=F32)


def _dot_nt(a, b):
    return lax.dot_general(a, b, (((1,), (1,)), ((), ())), preferred_element_type=F32)


def _silu(x):
    return x * jax.nn.sigmoid(x)


def _ff_chunks(d_ff):
    first = -(-(d_ff // 2) // MXU_WIDTH) * MXU_WIDTH
    return ((0, first), (first, d_ff - first)) if 0 < first < d_ff else ((0, d_ff),)


def _side_operands(n_args, n_outs, side):
    args, in_specs, out_shape, out_specs, aliases, layout = [], [], [], [], {}, []
    for job in side:
        for src, dst in job['aliases'].items():
            aliases[n_args + len(args) + src] = n_outs + len(out_shape) + dst
        layout.append((len(job['args']), len(job['out_shape']), job['fn']))
        args += job['args']
        in_specs += job['in_specs']
        out_shape += job['out_shape']
        out_specs += job['out_specs']
    return args, in_specs, out_shape, out_specs, aliases, tuple(layout)


def _run_side(layout, in_refs, out_refs):
    i = o = 0
    for n_in, n_out, fn in layout:
        fn(in_refs[i:i + n_in], out_refs[o:o + n_out])
        i += n_in
        o += n_out


def _shift_job(cache_t, new_cache, compact, layer, n_steps, kv, dil):
    _, b, _, nh, hd, window = cache_t.shape
    n_pos = window // dil
    bs = -(-b // n_steps)
    assert b % bs == 0
    n_blk = b // bs
    n_kv, kv_idx = (2, 0) if kv is None else (1, kv)
    blk = lambda i: jnp.minimum(i, n_blk - 1)
    spec = pl.BlockSpec((None, bs, n_kv, nh, hd, window), lambda i: (layer, blk(i), kv_idx, 0, 0, 0))
    cspec = pl.BlockSpec((bs, n_kv, nh * hd, n_pos), lambda i: (blk(i), kv_idx, 0, 0))
    pos = jnp.arange(window)[:, None] == dil * jnp.arange(n_pos)[None, :]
    pick = pos.astype(BF16)

    def fn(in_refs, out_refs):
        src, pick_ref = in_refs[:2]
        dst, cdst = out_refs
        for s in range(bs):
            for j in range(n_kv):
                for hh in range(nh):
                    t = src[s, j, hh]
                    dst[s, j, hh] = pltpu.roll(t, window - 1, 1)
                    tb = t.astype(BF16)
                    if dil > 1:
                        tb = _dot(tb, pick_ref[...]).astype(BF16)
                    cdst[s, j, hh * hd:(hh + 1) * hd, :] = tb

    job = dict(args=[cache_t, pick], in_specs=[spec, _const_spec(pick.shape)],
               out_shape=[jax.ShapeDtypeStruct(cache_t.shape, cache_t.dtype),
                          jax.ShapeDtypeStruct((b, 2, nh * hd, n_pos), BF16)],
               out_specs=[spec, cspec], aliases={}, fn=fn)
    for out_idx, buf in enumerate((new_cache, compact)):
        if buf is not None:
            job['aliases'][len(job['args'])] = out_idx
            job['args'].append(buf)
            job['in_specs'].append(pl.BlockSpec(memory_space=pl.ANY))
    return job


def _convert_job(weights, layer, n_steps):
    job = dict(args=[], in_specs=[], out_shape=[], out_specs=[], aliases={})
    for wt in weights:
        _, k, n = wt.shape
        n_band = max(c for c in range(1, n_steps + 1)
                     if n_steps % c == 0 and k % c == 0 and ((k // c) % BF16_ROWS == 0 or c == 1))
        band = lambda i, n_band=n_band: i * n_band // n_steps
        job['args'].append(wt)
        job['in_specs'].append(pl.BlockSpec((None, k // n_band, n), lambda i, band=band: (layer, band(i), 0)))
        job['out_shape'].append(jax.ShapeDtypeStruct((k, n), BF16))
        job['out_specs'].append(pl.BlockSpec((k // n_band, n), lambda i, band=band: (band(i), 0)))

    def fn(in_refs, out_refs):
        for src, dst in zip(in_refs, out_refs):
            dst[...] = src[...].astype(BF16)

    job['fn'] = fn
    return job


def _ffn_body(x_ref, g_ref, wgu_ref, wd_ref, *rest, pre, post, d_ff, with_ple, layout):
    n_ple = 3 if with_ple else 0
    n_side_in = sum(n_in for n_in, _, _ in layout)
    o_ref = rest[n_ple + n_side_in]
    _run_side(layout, rest[n_ple:n_ple + n_side_in], rest[n_ple + n_side_in + 1:])
    x = x_ref[...]
    h = _rms(x, g_ref[pre:pre + 1, :]).astype(BF16)
    acc = None
    for start, size in _ff_chunks(d_ff):
        gate = _dot(h, wgu_ref[:, start:start + size])
        up = _dot(h, wgu_ref[:, d_ff + start:d_ff + start + size])
        a = (_silu(gate) * up).astype(BF16)
        y = _dot(a, wd_ref[start:start + size, :])
        acc = y if acc is None else acc + y
    x = x + 0.5 * _rms(acc, g_ref[post:post + 1, :])
    if with_ple:
        p_ref, wg_ref, wp_ref = rest[:3]
        h = _rms(x, g_ref[6:7, :]).astype(BF16)
        gate = jax.nn.sigmoid(_dot(h, wg_ref[...]))
        proj = _dot(p_ref[...].astype(BF16), wp_ref[...])
        x = x + _rms(gate * proj, g_ref[7:8, :])
    o_ref[...] = x


def _ffn(x, gains, wgu, wd, layer, pre, post, tm, ple=None, side=()):
    m, d = x.shape
    d_ff = wd.shape[0]
    args = [x, gains, wgu, wd]
    in_specs = [_row_spec(tm, d), _layer_spec(gains.shape, layer), _const_spec(wgu.shape), _const_spec(wd.shape)]
    if ple is not None:
        p, wg, wp = ple
        args += [p, wg, wp]
        in_specs += [pl.BlockSpec((None, tm, p.shape[2]), lambda i: (layer, i, 0)),
                     _const_spec(wg.shape), _const_spec(wp.shape)]
    s_args, s_in, s_shape, s_out, aliases, layout = _side_operands(len(args), 1, side)
    body = functools.partial(_ffn_body, pre=pre, post=post, d_ff=d_ff, with_ple=ple is not None, layout=layout)
    outs = pl.pallas_call(
        body,
        grid=(m // tm,),
        in_specs=in_specs + s_in,
        out_specs=[_row_spec(tm, d)] + s_out,
        out_shape=[jax.ShapeDtypeStruct((m, d), F32)] + s_shape,
        input_output_aliases=aliases,
        compiler_params=_params("arbitrary"),
        name="ffn",
    )(*args, *s_args)
    return outs[0], outs[1:]


def _rot_ret(x, cos, sin):
    return x * cos + pltpu.roll(x, RET_DK // 2, 1) * sin


def _rot_attn(x, cos, sin_lo, sin_hi):
    return x * cos + pltpu.roll(x, LANES - ROT_DIM // 2, 1) * sin_lo + pltpu.roll(x, ROT_DIM // 2, 1) * sin_hi


def _to_streams(halves, out_ref, lane0, scr_ref, dil):
    if dil == 1:
        for s, v in enumerate(halves):
            out_ref[0, :, lane0 + s * LANES:lane0 + (s + 1) * LANES] = v
        return
    n = halves[0].shape[0] // dil
    for s, v in enumerate(halves):
        scr_ref[s] = v
    for r in range(dil):
        for s in range(2):
            out_ref[r, :, lane0 + s * LANES:lane0 + (s + 1) * LANES] = scr_ref[s, pl.ds(r, n, stride=dil), :]


def _proj_body(*refs, dils):
    x_ref, g_ref, w_ref, wc_ref, bc_ref, tab_ref, hist_ref = refs[:7]
    outs = refs[7:]
    cy_ref, ulast_ref, ret_ref = outs[:3]
    attn_refs = outs[3:6]
    decode = dils is None
    if not decode:
        carry_ref, scr_ref = outs[6:8]
    tab = lambda j: tab_ref[:, j * LANES:(j + 1) * LANES]

    x = x_ref[...]
    h = _rms(x, g_ref[2:3, :]).astype(BF16)

    def part(off, width):
        return _dot(h, w_ref[:, off:off + width])


    acos, alo, ahi = tab(2), tab(3), tab(4)
    for gi in reversed(range(len(ATTN_GROUPS))):
        base = OFF_ATTN + gi * 3 * ATTN_DIM
        zq = part(base, ATTN_DIM)
        zk = part(base + ATTN_DIM, ATTN_DIM)
        zv = part(base + 2 * ATTN_DIM, ATTN_DIM)
        halves = lambda z: [z[:, s * LANES:(s + 1) * LANES] for s in range(2)]
        q_h = [_rot_attn(v, acos, alo, ahi) for v in halves(zq)]
        k_h = [_rot_attn(v, acos, alo, ahi) for v in halves(zk)]
        out_ref = attn_refs[gi]
        for j, vals in enumerate((q_h, k_h, halves(zv))):
            if decode:
                for s, v in enumerate(vals):
                    out_ref[:, j * ATTN_DIM + s * LANES:j * ATTN_DIM + (s + 1) * LANES] = v
            else:
                _to_streams(vals, out_ref, j * ATTN_DIM, scr_ref.at[3 * gi + j], dils[gi])

    b_gate = part(OFF_CONV, CONV_DIM)
    u = part(OFF_CONV + CONV_DIM, CONV_DIM) * part(OFF_CONV + 2 * CONV_DIM, CONV_DIM)
    if decode:
        u2 = hist_ref[0]
        u1 = hist_ref[1]
        ulast_ref[0] = u1
        ulast_ref[1] = u
    else:
        tm = u.shape[0]

        @pl.when(pl.program_id(0) == 0)
        def _():
            carry_ref[...] = hist_ref[...]

        carry = carry_ref[...]
        row = lax.broadcasted_iota(jnp.int32, u.shape, 0)
        u1 = jnp.where(row == 0, carry[7:8, :], pltpu.roll(u, 1, 0))
        u2 = jnp.where(row == 0, carry[6:7, :], jnp.where(row == 1, carry[7:8, :], pltpu.roll(u, 2, 0)))
        carry_ref[...] = u[tm - 8:tm, :]
        ulast_ref[...] = u[tm - 8:tm, :]
    y = bc_ref[...] + wc_ref[0:1, :] * u2
    y = y + wc_ref[1:2, :] * u1
    y = y + wc_ref[2:3, :] * u
    cy_ref[...] = (b_gate * y).astype(cy_ref.dtype)

    cos, sin = tab(0), tab(1)
    zq = part(OFF_RET, 512)
    zk = part(OFF_RET + 512, 512)
    for hh in range(RET_HEADS):
        cols = slice(hh * RET_DK, (hh + 1) * RET_DK)
        ret_ref[:, hh * RET_DK:(hh + 1) * RET_DK] = _rot_ret(zq[:, cols], cos, sin)
        ret_ref[:, 512 + hh * RET_DK:512 + (hh + 1) * RET_DK] = _rot_ret(zk[:, cols], cos, sin) * (RET_DK ** -0.5)
    ret_ref[:, 1024:1536] = part(OFF_RET + 1024, 512)
    ret_ref[:, 1536:2048] = part(OFF_RET + 1536, 512)


def _proj(x, gains, w_in, w_conv, b_conv, tabs, hist, layer, tm, decode):
    m, d = x.shape
    n = m // tm
    qkv = 3 * ATTN_DIM
    if decode:
        dils = None
        hist_spec = pl.BlockSpec((None, 2, tm, CONV_DIM), lambda i: (layer, 0, i, 0))
        ulast_spec = pl.BlockSpec((2, tm, CONV_DIM), lambda i: (0, i, 0))
        ulast_shape = jax.ShapeDtypeStruct((2, m, CONV_DIM), F32)
        attn_specs = [_row_spec(tm, qkv)] * 3
        attn_shapes = [jax.ShapeDtypeStruct((m, qkv), F32)] * 3
        scratch = []
    else:
        dils = DILATIONS
        hist_spec = _const_spec((8, CONV_DIM))
        ulast_spec = pl.BlockSpec((8, CONV_DIM), lambda i: (0, 0))
        ulast_shape = jax.ShapeDtypeStruct((8, CONV_DIM), F32)
        attn_specs = [pl.BlockSpec((dl, tm // dl, qkv), lambda i: (0, i, 0)) for dl in dils]
        attn_shapes = [jax.ShapeDtypeStruct((dl, m // dl, qkv), F32) for dl in dils]
        scratch = [pltpu.VMEM((8, CONV_DIM), F32), pltpu.VMEM((9, 2, tm, LANES), F32)]
    in_specs = [
        _row_spec(tm, d), _layer_spec(gains.shape, layer), _const_spec(w_in.shape),
        _layer_spec(w_conv.shape, layer), _layer_spec(b_conv.shape, layer), _row_spec(tm, tabs.shape[1]), hist_spec,
    ]
    out_specs = [_row_spec(tm, CONV_DIM), ulast_spec, _row_spec(tm, 4 * 512)] + attn_specs
    out_shape = [jax.ShapeDtypeStruct((m, CONV_DIM), BF16), ulast_shape,
                 jax.ShapeDtypeStruct((m, 4 * 512), F32)] + attn_shapes
    return pl.pallas_call(
        functools.partial(_proj_body, dils=dils),
        grid=(n,),
        in_specs=in_specs,
        out_specs=out_specs,
        out_shape=out_shape,
        scratch_shapes=scratch,
        compiler_params=_params("arbitrary"),
        name="proj",
    )(x, gains, w_in, w_conv, b_conv, tabs, hist)


def _head_norm(o):
    mu = jnp.mean(o, axis=-1, keepdims=True)
    var = jnp.mean(jnp.square(o - mu), axis=-1, keepdims=True)
    return (o - mu) * lax.rsqrt(var + EPS)


def _ret_body(r_ref, decay_ref, qw_ref, kw_ref, gch_ref, y_ref, sout_ref, s_scr):
    @pl.when(pl.program_id(0) == 0)
    def _():
        s_scr[...] = jnp.zeros_like(s_scr)

    for hh in range(RET_HEADS):
        cols = slice(hh * RET_DK, (hh + 1) * RET_DK)
        part = lambda j: r_ref[:, j * 512 + hh * RET_DK:j * 512 + (hh + 1) * RET_DK]
        q = part(0)
        k = part(1)
        v = part(2).astype(BF16)
        s = s_scr[hh]
        scores = _dot_nt(q.astype(BF16), k.astype(BF16)) * decay_ref[hh]
        o = _dot(scores.astype(BF16), v) + _dot((q * qw_ref[hh]).astype(BF16), s.astype(BF16))
        kv = _dot(jnp.transpose(k * kw_ref[hh]).astype(BF16), v)
        s_scr[hh] = s * gch_ref[hh] + kv
        y_ref[:, cols] = (_head_norm(o) * _silu(part(3))).astype(y_ref.dtype)

    @pl.when(pl.program_id(0) == pl.num_programs(0) - 1)
    def _():
        sout_ref[...] = s_scr[...]


def _ret_log_decay():
    return jnp.log1p(-jnp.exp2(-5.0 - jnp.arange(RET_HEADS, dtype=F32)))


def _ret_tables(chunk):
    i = jnp.arange(chunk, dtype=F32)
    log_g = _ret_log_decay()
    diff = i[:, None] - i[None, :]
    decay = jnp.where(diff[None] >= 0, jnp.exp(jnp.maximum(diff, 0.0)[None] * log_g[:, None, None]), 0.0)
    k_w = jnp.exp((chunk - 1 - i)[:, None] * log_g[None, :])
    q_w = jnp.exp((i + 1)[:, None] * log_g[None, :])
    g_chunk = jnp.exp(chunk * log_g)
    qw = jnp.broadcast_to(q_w.T[:, :, None], (RET_HEADS, chunk, RET_DK))
    kw = jnp.broadcast_to(k_w.T[:, :, None], (RET_HEADS, chunk, RET_DK))
    gch = jnp.broadcast_to(g_chunk[:, None, None], (RET_HEADS, RET_DK, RET_DK))
    return decay, qw, kw, gch


def _retention(r, tr):
    m = r.shape[0]
    decay, qw, kw, gch = _ret_tables(tr)
    return pl.pallas_call(
        _ret_body,
        grid=(m // tr,),
        in_specs=[_row_spec(tr, r.shape[1])] + [_const_spec(a.shape) for a in (decay, qw, kw, gch)],
        out_specs=[_row_spec(tr, 512), pl.BlockSpec((RET_HEADS, RET_DK, RET_DK), lambda i: (0, 0, 0))],
        out_shape=[jax.ShapeDtypeStruct((m, 512), BF16), jax.ShapeDtypeStruct((RET_HEADS, RET_DK, RET_DK), F32)],
        scratch_shapes=[pltpu.VMEM((RET_HEADS, RET_DK, RET_DK), F32)],
        compiler_params=_params("arbitrary"),
        name="retention",
    )(r, decay, qw, kw, gch)


def _ret_dec_body(r_ref, gd_ref, s0_ref, *rest):
    y_ref, sout_ref = rest[-2:]
    for b in range(r_ref.shape[0]):
        for hh in range(RET_HEADS):
            part = lambda j: r_ref[b:b + 1, j * 512 + hh * RET_DK:j * 512 + (hh + 1) * RET_DK]
            q, k, v = part(0), part(1), part(2)
            gd = gd_ref[hh]
            s0 = s0_ref[b, hh]
            wide = (RET_DK, RET_DK)
            q_col = jnp.transpose(jnp.broadcast_to(q * gd, wide))
            k_col = jnp.transpose(jnp.broadcast_to(k, wide))
            o_inter = jnp.sum(q_col * s0, axis=0, keepdims=True)
            o_intra = jnp.sum(q * k, axis=-1, keepdims=True) * v
            sout_ref[b, hh] = s0 * gd + k_col * v
            o = o_intra + o_inter
            y_ref[b:b + 1, hh * RET_DK:(hh + 1) * RET_DK] = _head_norm(o) * _silu(part(3))


def _retention_decode(r, gdec, state, layer, prev_out):
    b = r.shape[0]
    bs = DEC_BLOCK_SAMPLES
    while b % bs:
        bs -= 1
    st = pl.BlockSpec((None, bs, RET_HEADS, RET_DK, RET_DK), lambda i: (layer, i, 0, 0, 0))
    args = [r, gdec, state]
    in_specs = [_row_spec(bs, r.shape[1]), _const_spec(gdec.shape), st]
    aliases = {}
    if prev_out is not None:
        args.append(prev_out)
        in_specs.append(pl.BlockSpec(memory_space=pl.ANY))
        aliases = {3: 1}
    return pl.pallas_call(
        _ret_dec_body,
        grid=(b // bs,),
        in_specs=in_specs,
        out_specs=[_row_spec(bs, 512), st],
        out_shape=[jax.ShapeDtypeStruct((b, 512), F32), jax.ShapeDtypeStruct(state.shape, F32)],
        input_output_aliases=aliases,
        compiler_params=_params("parallel"),
        name="retention_decode",
    )(*args)


def _attn_body(cur_ref, prev_ref, out_ref, *, n_sub):
    n = pl.program_id(1)
    qb = ATTN_BLOCK
    ii = lax.broadcasted_iota(jnp.int32, (2 * qb, 2 * qb), 0) & (qb - 1)
    jj = lax.broadcasted_iota(jnp.int32, (2 * qb, 2 * qb), 1)
    band = jnp.logical_and(jj >= ii, jj <= ii + qb)
    first = jnp.logical_and(band, jnp.logical_or(jj >= qb, n > 0))
    lo = lax.broadcasted_iota(jnp.int32, (qb, LANES), 1) < HEAD_DIM
    for hp in range(ATTN_HEADS // 2):
        cols = lambda j: slice(j * ATTN_DIM + hp * LANES, j * ATTN_DIM + (hp + 1) * LANES)
        kk = jnp.concatenate([prev_ref[:, cols(1)], cur_ref[:, cols(1)]], axis=0).astype(BF16)
        vv = jnp.concatenate([prev_ref[:, cols(2)], cur_ref[:, cols(2)]], axis=0).astype(BF16)
        for b in range(n_sub):
            rows = slice(b * qb, (b + 1) * qb)
            q2 = cur_ref[rows, cols(0)] * ATTN_SCALE
            zero = jnp.zeros_like(q2)
            qs = jnp.concatenate([jnp.where(lo, q2, zero), jnp.where(lo, zero, q2)], axis=0).astype(BF16)
            s = jnp.where(first if b == 0 else band, _dot_nt(qs, kk[b * qb:(b + 2) * qb]), NEG)
            mx = jnp.max(s, axis=-1, keepdims=True)
            p = jnp.exp(s - mx)
            den = jnp.sum(p, axis=-1, keepdims=True)
            o = _dot(p.astype(BF16), vv[b * qb:(b + 2) * qb]) / den
            lse = mx + jnp.log(den)
            out_ref[rows, cols(0)] = jnp.where(lo, o[:qb], o[qb:])
            out_ref[rows, cols(1)] = jnp.where(lo, lse[:qb], lse[qb:])


def _attention(qkv):
    dil, length, width = qkv.shape
    step = min(ATTN_STEP, length)
    n_sub = step // ATTN_BLOCK
    cur = pl.BlockSpec((None, step, width), lambda r, n: (r, n, 0))
    prev = pl.BlockSpec((None, ATTN_BLOCK, width), lambda r, n: (r, jnp.maximum(n * n_sub - 1, 0), 0))
    return pl.pallas_call(
        functools.partial(_attn_body, n_sub=n_sub),
        grid=(dil, length // step),
        in_specs=[cur, prev],
        out_specs=pl.BlockSpec((None, step, 2 * ATTN_DIM), lambda r, n: (r, n, 0)),
        out_shape=jax.ShapeDtypeStruct((dil, length, 2 * ATTN_DIM), F32),
        compiler_params=_params("parallel", "arbitrary"),
        name="attention",
    )(qkv, qkv)


def _attn_dec_body(qkv_ref, kc_ref, tail_ref, new_ref, out_ref, cout_ref):
    del new_ref
    rows = BF16_ROWS
    head_of_lane = lax.broadcasted_iota(jnp.int32, (rows, ATTN_DIM), 1) // HEAD_DIM
    own = head_of_lane == lax.broadcasted_iota(jnp.int32, (rows, ATTN_DIM), 0)
    last = lax.broadcasted_iota(jnp.int32, (HEAD_DIM, LANES), 1) == LANES - 1
    wide = lambda a: jnp.broadcast_to(a, (rows, ATTN_DIM))
    pick = lambda a: jnp.sum(jnp.where(own, a, 0.0), axis=0, keepdims=True)
    for b in range(qkv_ref.shape[0]):
        q, kn, vn = (qkv_ref[b:b + 1, j * ATTN_DIM:(j + 1) * ATTN_DIM] for j in range(3))
        q_rows = jnp.where(own, wide(q), 0.0)
        s = _dot(q_rows.astype(BF16), kc_ref[b, 0]) * ATTN_SCALE
        s_new = jnp.sum(q_rows * wide(kn), axis=-1, keepdims=True) * ATTN_SCALE
        mx = jnp.maximum(jnp.max(s, axis=-1, keepdims=True), s_new)
        p = jnp.exp(s - mx)
        p_new = jnp.exp(s_new - mx)
        den = jnp.sum(p, axis=-1, keepdims=True) + p_new
        o = (_dot_nt(p.astype(BF16), kc_ref[b, 1]) + p_new * wide(vn)) / den
        out_ref[b:b + 1, 0:ATTN_DIM] = pick(o)
        out_ref[b:b + 1, ATTN_DIM:2 * ATTN_DIM] = pick(wide(mx + jnp.log(den)))
        new_rows = jnp.concatenate([kn, vn], axis=1)
        new_cols = jnp.transpose(jnp.broadcast_to(new_rows, (LANES, 2 * ATTN_DIM)))
        for kv in range(2):
            for hh in range(ATTN_HEADS):
                col = new_cols[kv * ATTN_DIM + hh * HEAD_DIM:kv * ATTN_DIM + (hh + 1) * HEAD_DIM, :]
                cout_ref[b, kv, hh] = jnp.where(last, col, pltpu.roll(tail_ref[b, kv, hh], LANES - 1, 1))


def _attention_decode(qkv, compact, cache_t, new_cache, layer):
    b = qkv.shape[0]
    window = cache_t.shape[-1]
    bs = DEC_BLOCK_SAMPLES
    while b % bs:
        bs -= 1
    kc = pl.BlockSpec((bs,) + compact.shape[1:], lambda i: (i, 0, 0, 0))
    tail = pl.BlockSpec((None, bs, 2, ATTN_HEADS, HEAD_DIM, LANES),
                        lambda i: (layer, i, 0, 0, 0, window // LANES - 1))
    out, cnew = pl.pallas_call(
        _attn_dec_body,
        grid=(b // bs,),
        in_specs=[_row_spec(bs, qkv.shape[1]), kc, tail, pl.BlockSpec(memory_space=pl.ANY)],
        out_specs=[_row_spec(bs, 2 * ATTN_DIM), tail],
        out_shape=[jax.ShapeDtypeStruct((b, 2 * ATTN_DIM), F32), jax.ShapeDtypeStruct(cache_t.shape, F32)],
        input_output_aliases={3: 1},
        compiler_params=_params("parallel"),
        name="attention_decode",
    )(qkv, compact, cache_t, new_cache)
    return out[None], cnew


def _from_streams(in_ref, lane0, scr_ref, dil):
    if dil == 1:
        return in_ref[0, :, lane0:lane0 + ATTN_DIM]
    n = in_ref.shape[1]
    for r in range(dil):
        for s in range(2):
            scr_ref[s, pl.ds(r, n, stride=dil), :] = in_ref[r, :, lane0 + s * LANES:lane0 + (s + 1) * LANES]
    return jnp.concatenate([scr_ref[0], scr_ref[1]], axis=1)


def _merge_body(x_ref, cy_ref, ry_ref, a0_ref, a1_ref, a2_ref, g_ref,
                wg_ref, wc_ref, wr_ref, wa_ref, wo_ref, *rest, dils, layout):
    n_side_in = sum(n_in for n_in, _, _ in layout)
    n_side_out = sum(n_out for _, n_out, _ in layout)
    out_ref = rest[n_side_in]
    scratch = rest[n_side_in + 1 + n_side_out:]
    _run_side(layout, rest[:n_side_in], rest[n_side_in + 1:n_side_in + 1 + n_side_out])
    x = x_ref[...]
    h = _rms(x, g_ref[2:3, :]).astype(BF16)
    scr = scratch[0] if scratch else None
    d = x.shape[1]
    merged = jax.nn.sigmoid(_dot(h, wg_ref[:, 0:d])) * _dot(cy_ref[...].astype(BF16), wc_ref[...])
    merged = merged + jax.nn.sigmoid(_dot(h, wg_ref[:, d:2 * d])) * _dot(ry_ref[...].astype(BF16), wr_ref[...])
    attn_gate = jax.nn.sigmoid(_dot(h, wg_ref[:, 2 * d:3 * d]))
    vals = []
    for j in range(6):
        ref = (a0_ref, a1_ref, a2_ref)[j // 2]
        vals.append(_from_streams(ref, (j % 2) * ATTN_DIM, None if scr is None else scr.at[j], dils[j // 2]))
    o0, l0, o1, l1, o2, l2 = vals
    mx = jnp.maximum(jnp.maximum(l0, l1), l2)
    e0, e1, e2 = jnp.exp(l0 - mx), jnp.exp(l1 - mx), jnp.exp(l2 - mx)
    den = e0 + e1 + e2
    attn_y = ((e0 / den) * o0 + (e1 / den) * o1 + (e2 / den) * o2).astype(BF16)
    merged = merged + attn_gate * _dot(attn_y, wa_ref[...])
    out_ref[...] = x + _rms(_dot(merged.astype(BF16), wo_ref[...]), g_ref[3:4, :])


def _merge(x, cy, ry, attn, dils, gains, wg, wc, wr, wa, wo, layer, tm, side=()):
    m, d = x.shape
    a_specs = [pl.BlockSpec((dl, tm // dl, 2 * ATTN_DIM), lambda i: (0, i, 0)) for dl in dils]
    scratch = [pltpu.VMEM((6, 2, tm, LANES), F32)] if max(dils) > 1 else []
    args = [x, cy, ry, *attn, gains, wg, wc, wr, wa, wo]
    s_args, s_in, s_shape, s_out, aliases, layout = _side_operands(len(args), 1, side)
    outs = pl.pallas_call(
        functools.partial(_merge_body, dils=dils, layout=layout),
        grid=(m // tm,),
        in_specs=[_row_spec(tm, d), _row_spec(tm, CONV_DIM), _row_spec(tm, 512)] + a_specs
        + [_layer_spec(gains.shape, layer)] + [_const_spec(a.shape) for a in (wg, wc, wr, wa, wo)] + s_in,
        out_specs=[_row_spec(tm, d)] + s_out,
        out_shape=[jax.ShapeDtypeStruct((m, d), F32)] + s_shape,
        scratch_shapes=scratch,
        input_output_aliases=aliases,
        compiler_params=_params("arbitrary"),
        name="merge",
    )(*args, *s_args)
    return outs[0], outs[1:]


def _cos_sin(start, count, freq, split):
    if split is None or count % split:
        ang = (start + jnp.arange(count)).astype(F32)[:, None] * freq[None, :]
        return jnp.cos(ang), jnp.sin(ang)
    base = (start + split * jnp.arange(count // split)).astype(F32)[:, None, None] * freq
    off = jnp.arange(split).astype(F32)[None, :, None] * freq
    cb, sb, co, so = jnp.cos(base), jnp.sin(base), jnp.cos(off), jnp.sin(off)
    return (cb * co - sb * so).reshape(count, LANES), (sb * co + cb * so).reshape(count, LANES)


def _rotary_tables(start, count, split=None):
    lane = jnp.arange(LANES)
    half = RET_DK // 2
    freq = jnp.exp(-(lane % half).astype(F32) * (math.log(RET_THETA) / half))
    cos, sin = _cos_sin(start, count, freq, split)
    rc, rs = cos, jnp.where(lane < half, -sin, sin)
    dim = lane % HEAD_DIM
    half = ROT_DIM // 2
    freq = jnp.exp(-(dim % half).astype(F32) * (math.log(ROPE_THETA) / half))
    cos, sin = _cos_sin(start, count, freq, split)
    ac = jnp.where(dim < ROT_DIM, cos, 1.0)
    alo = jnp.where(dim < half, -sin, 0.0)
    ahi = jnp.where(jnp.logical_and(dim >= half, dim < ROT_DIM), sin, 0.0)
    return rc, rs, ac, alo, ahi


def _finish(x, ple, cy, ry, attn, dils, w, layer, tm, merge_side=(), ffn_side=()):
    x, merge_out = _merge(x, cy, ry, attn, dils, w['gains'], w['w_gate'], w['w_conv_out'], w['w_ret_out'],
                          w['w_attn_out'], w['w_o'], layer, tm, side=merge_side)
    x, ffn_out = _ffn(x, w['gains'], w['ffn2_gu'], w['ffn2_down'], layer, 4, 5, tm,
                      ple=(ple, w['w_ple_gate'], w['w_ple_proj']), side=ffn_side)
    return x, merge_out, ffn_out


W_FFN1 = ('ffn1_gu', 'ffn1_down')
W_EARLY = W_FFN1 + ('w_in',)
W_LATE = ('w_gate', 'w_conv_out', 'w_ret_out', 'w_attn_out', 'w_o', 'ffn2_gu', 'ffn2_down', 'w_ple_gate',
          'w_ple_proj')


def _prompt_layer(x, ple, tabs, caches_t, new_kv, raw, w, layer):
    depth = raw['w_in'].shape[0]
    s = x.shape[0]
    tm = min(ROW_TILE, s)
    steps = s // tm
    wide_dil = DILATIONS[2]
    todo = tuple(n for n in W_EARLY[2:] + W_LATE if n not in w)
    side = [_shift_job(caches_t[2], new_kv[2], None, layer, steps, 0, wide_dil)]
    if todo:
        side.append(_convert_job([raw[n] for n in todo], layer, steps))
    x, outs = _ffn(x, w['gains'], w['ffn1_gu'], w['ffn1_down'], layer, 0, 1, tm, side=side)
    wide, wide_c = outs[:2]
    w = dict(w, **dict(zip(todo, outs[2:])))
    hist = jnp.zeros((8, CONV_DIM), F32)
    outs = _proj(x, w['gains'], w['w_in'], w['w_conv'], w['b_conv'], tabs, hist, layer, tm, decode=False)
    cy, ulast, ret_in = outs[:3]
    ry, ret_state = _retention(ret_in, tm)
    attn, kv = [], []
    for gi, (window, dil) in enumerate(ATTN_GROUPS):
        qkv = outs[3 + gi]
        attn.append(_attention(qkv))
        keep = min(window, s) // dil
        tail = jnp.swapaxes(qkv[:, s // dil - keep:, ATTN_DIM:], 0, 1)
        kv.append(tail.reshape(1, keep * dil, 2, ATTN_HEADS, HEAD_DIM))
    merge_side = [_shift_job(caches_t[gi], new_kv[gi], None, layer, steps, None, DILATIONS[gi]) for gi in (0, 1)]
    ffn_side = [_shift_job(caches_t[2], wide, wide_c, layer, steps, 1, wide_dil)]
    if layer + 1 < depth:
        merge_side.append(_convert_job([raw[n] for n in W_EARLY], layer + 1, steps))
        ffn_side.append(_convert_job([raw[n] for n in W_LATE], layer + 1, steps))
    x, m_out, f_out = _finish(x, ple, cy, ry, attn, DILATIONS, w, layer, tm, merge_side, ffn_side)
    w_next = dict(zip(W_EARLY, m_out[4:]), **dict(zip(W_LATE, f_out[2:])))
    new_kv = [m_out[0], m_out[2], f_out[0]]
    compact = [m_out[1], m_out[3], f_out[1]]
    return x, ulast[6:8][None], ret_state[None], kv, new_kv, compact, w, w_next


def _sample_layer(x, ple, tabs, gdec, conv_hist, ret_state, caches_t, prev_ret, new_kv, compact, w, layer):
    b = x.shape[0]
    x, _ = _ffn(x, w['gains'], w['ffn1_gu'], w['ffn1_down'], layer, 0, 1, b)
    outs = _proj(x, w['gains'], w['w_in'], w['w_conv'], w['b_conv'], tabs, conv_hist, layer, b, decode=True)
    cy, ulast, ret_in = outs[:3]
    ry, ret_new = _retention_decode(ret_in, gdec, ret_state, layer, prev_ret)
    attn, kv = [], []
    for gi in range(len(ATTN_GROUPS)):
        o_lse, cnew = _attention_decode(outs[3 + gi], compact[gi], caches_t[gi], new_kv[gi], layer)
        attn.append(o_lse)
        kv.append(cnew)
    x, _, _ = _finish(x, ple, cy, ry, attn, (1, 1, 1), w, layer, b)
    return x, jnp.swapaxes(ulast, 0, 1), ret_new, kv


def kernel(x_prompt, x_sample, state_conv, state_ret, cache_kv_w128, cache_kv_w512, cache_kv_w2048, p_prompt, p_sample, norm_gain, w_ffn1_gu, w_ffn1_down, w_in, w_conv, b_conv, w_conv_out, w_ret_out, w_attn_out, w_gate, w_o, w_ffn2_gu, w_ffn2_down, w_ple_gate, w_ple_proj):
    depth = norm_gain.shape[0]
    seq = x_prompt.shape[1]
    nb = x_sample.shape[0]
    assert x_prompt.shape[0] == 1 and x_sample.shape[1] == 1

    tabs_p = jnp.concatenate(_rotary_tables(0, seq, split=ROT_SPLIT), axis=1)
    tabs_s = jnp.broadcast_to(jnp.concatenate(_rotary_tables(PAST_LEN, 1), axis=1), (nb, 5 * LANES))
    gdec = jnp.broadcast_to(jnp.exp(_ret_log_decay())[:, None, None], (RET_HEADS, 1, RET_DK))
    caches_t = [jnp.transpose(c, (0, 1, 3, 4, 5, 2)) for c in (cache_kv_w128, cache_kv_w512, cache_kv_w2048)]
    raw = dict(ffn1_gu=w_ffn1_gu, ffn1_down=w_ffn1_down, w_in=w_in, w_conv_out=w_conv_out, w_ret_out=w_ret_out,
               w_attn_out=w_attn_out, w_gate=w_gate, w_o=w_o, ffn2_gu=w_ffn2_gu, ffn2_down=w_ffn2_down,
               w_ple_gate=w_ple_gate, w_ple_proj=w_ple_proj)
    shared = dict(gains=norm_gain, w_conv=w_conv, b_conv=b_conv.reshape(depth, 1, CONV_DIM))
    w_next = {n: raw[n][0].astype(BF16) for n in W_FFN1}
    ple_p = p_prompt[:, 0]
    ple_s = p_sample[:, :, 0]
    conv_hist = jnp.swapaxes(state_conv, 1, 2)

    yp, ys = x_prompt[0], x_sample[:, 0]
    conv_p, conv_s, ret_p = [], [], []
    kv_p = [[] for _ in ATTN_GROUPS]
    ret_s, kv_s = None, [None] * len(ATTN_GROUPS)
    for l in range(depth):
        yp, cp, rp, kp, kv_s, compact, w, w_next = _prompt_layer(yp, ple_p, tabs_p, caches_t, kv_s, raw,
                                                                 dict(shared, **w_next), l)
        ys, cs, ret_s, kv_s = _sample_layer(ys, ple_s, tabs_s, gdec, conv_hist, state_ret, caches_t, ret_s, kv_s,
                                            compact, w, l)
        conv_p.append(cp)
        conv_s.append(cs)
        ret_p.append(rp)
        for gi in range(len(ATTN_GROUPS)):
            kv_p[gi].append(kp[gi])
    back = lambda c: jnp.transpose(c, (0, 1, 5, 2, 3, 4))
    return (yp[None], ys[:, None], jnp.stack(conv_p), jnp.stack(conv_s), jnp.stack(ret_p), ret_s,
            jnp.stack(kv_p[0]), back(kv_s[0]), jnp.stack(kv_p[1]), back(kv_s[1]),
            jnp.stack(kv_p[2]), back(kv_s[2]))
```

```python
import functools
import math

import jax
import jax.numpy as jnp
from jax import lax
from jax.experimental import pallas as pl
from jax.experimental.pallas import tpu as pltpu

F32 = jnp.float32
BF16 = jnp.bfloat16

EPS = 1e-6
PAST_LEN = 16384
CONV_DIM = 512
RET_HEADS = 4
RET_DK = 128
RET_THETA = 10000.0
ATTN_GROUPS = ((128, 1), (512, 4), (2048, 16))
DILATIONS = tuple(d for _, d in ATTN_GROUPS)
ATTN_HEADS = 4
HEAD_DIM = 64
ATTN_DIM = ATTN_HEADS * HEAD_DIM
ATTN_SCALE = HEAD_DIM ** -0.5
ROT_DIM = 16
ROPE_THETA = 500000.0
ROT_SPLIT = 128
ATTN_BLOCK = 128
ATTN_STEP = 1024
DEC_BLOCK_WINDOW = 2048
DEC_BLOCK_SAMPLES = 8
NEG = -1e30
LANES = 128
MXU_WIDTH = 256
BF16_ROWS = 16
OFF_CONV = 0
OFF_RET = 3 * CONV_DIM
OFF_ATTN = OFF_RET + 4 * 512

VMEM_LIMIT_BYTES = 60000 * 1024
ROW_TILE = 512


def _params(*sem):
    return pltpu.CompilerParams(dimension_semantics=sem, vmem_limit_bytes=VMEM_LIMIT_BYTES)


def _const_spec(shape):
    zeros = (0,) * len(shape)
    return pl.BlockSpec(shape, lambda *_: zeros, pipeline_mode=pl.Buffered(1))


def _layer_spec(shape, layer):
    tail = tuple(shape[1:])
    idx = (layer,) + (0,) * len(tail)
    return pl.BlockSpec((None,) + tail, lambda *_: idx, pipeline_mode=pl.Buffered(1))


def _row_spec(tm, width):
    return pl.BlockSpec((tm, width), lambda i: (i, 0))


def _rms(x, g):
    return x * lax.rsqrt(jnp.mean(x * x, axis=-1, keepdims=True) + EPS) * g


def _dot(a, b):
    return jnp.dot(a, b, preferred_element_type=F32)


def _dot_nt(a, b):
    return lax.dot_general(a, b, (((1,), (1,)), ((), ())), preferred_element_type=F32)


def _silu(x):
    return x * jax.nn.sigmoid(x)


def _ff_chunks(d_ff):
    first = -(-(d_ff // 2) // MXU_WIDTH) * MXU_WIDTH
    return ((0, first), (first, d_ff - first)) if 0 < first < d_ff else ((0, d_ff),)


def _side_operands(n_args, n_outs, side):
    args, in_specs, out_shape, out_specs, aliases, layout = [], [], [], [], {}, []
    for job in side:
        for src, dst in job['aliases'].items():
            aliases[n_args + len(args) + src] = n_outs + len(out_shape) + dst
        layout.append((len(job['args']), len(job['out_shape']), job['fn']))
        args += job['args']
        in_specs += job['in_specs']
        out_shape += job['out_shape']
        out_specs += job['out_specs']
    return args, in_specs, out_shape, out_specs, aliases, tuple(layout)


def _run_side(layout, in_refs, out_refs):
    i = o = 0
    for n_in, n_out, fn in layout:
        fn(in_refs[i:i + n_in], out_refs[o:o + n_out])
        i += n_in
        o += n_out


def _shift_job(cache_t, new_cache, compact, layer, n_steps, kv, dil):
    _, b, _, nh, hd, window = cache_t.shape
    n_pos = window // dil
    bs = -(-b // n_steps)
    assert b % bs == 0
    n_blk = b // bs
    n_kv, kv_idx = (2, 0) if kv is None else (1, kv)
    blk = lambda i: jnp.minimum(i, n_blk - 1)
    spec = pl.BlockSpec((None, bs, n_kv, nh, hd, window), lambda i: (layer, blk(i), kv_idx, 0, 0, 0))
    cspec = pl.BlockSpec((bs, n_kv, nh * hd, n_pos), lambda i: (blk(i), kv_idx, 0, 0))
    pos = jnp.arange(window)[:, None] == dil * jnp.arange(n_pos)[None, :]
    pick = pos.astype(BF16)

    def fn(in_refs, out_refs):
        src, pick_ref = in_refs[:2]
        dst, cdst = out_refs
        for s in range(bs):
            for j in range(n_kv):
                for hh in range(nh):
                    t = src[s, j, hh]
                    dst[s, j, hh] = pltpu.roll(t, window - 1, 1)
                    tb = t.astype(BF16)
                    if dil > 1:
                        tb = _dot(tb, pick_ref[...]).astype(BF16)
                    cdst[s, j, hh * hd:(hh + 1) * hd, :] = tb

    job = dict(args=[cache_t, pick], in_specs=[spec, _const_spec(pick.shape)],
               out_shape=[jax.ShapeDtypeStruct(cache_t.shape, cache_t.dtype),
                          jax.ShapeDtypeStruct((b, 2, nh * hd, n_pos), BF16)],
               out_specs=[spec, cspec], aliases={}, fn=fn)
    for out_idx, buf in enumerate((new_cache, compact)):
        if buf is not None:
            job['aliases'][len(job['args'])] = out_idx
            job['args'].append(buf)
            job['in_specs'].append(pl.BlockSpec(memory_space=pl.ANY))
    return job


def _convert_job(weights, layer, n_steps):
    job = dict(args=[], in_specs=[], out_shape=[], out_specs=[], aliases={})
    for wt in weights:
        _, k, n = wt.shape
        n_band = max(c for c in range(1, n_steps + 1)
                     if n_steps % c == 0 and k % c == 0 and ((k // c) % BF16_ROWS == 0 or c == 1))
        band = lambda i, n_band=n_band: i * n_band // n_steps
        job['args'].append(wt)
        job['in_specs'].append(pl.BlockSpec((None, k // n_band, n), lambda i, band=band: (layer, band(i), 0)))
        job['out_shape'].append(jax.ShapeDtypeStruct((k, n), BF16))
        job['out_specs'].append(pl.BlockSpec((k // n_band, n), lambda i, band=band: (band(i), 0)))

    def fn(in_refs, out_refs):
        for src, dst in zip(in_refs, out_refs):
            dst[...] = src[...].astype(BF16)

    job['fn'] = fn
    return job


def _ffn_body(x_ref, g_ref, wgu_ref, wd_ref, *rest, pre, post, d_ff, with_ple, layout):
    n_ple = 3 if with_ple else 0
    n_side_in = sum(n_in for n_in, _, _ in layout)
    o_ref = rest[n_ple + n_side_in]
    _run_side(layout, rest[n_ple:n_ple + n_side_in], rest[n_ple + n_side_in + 1:])
    x = x_ref[...]
    h = _rms(x, g_ref[pre:pre + 1, :]).astype(BF16)
    acc = None
    for start, size in _ff_chunks(d_ff):
        gate = _dot(h, wgu_ref[:, start:start + size])
        up = _dot(h, wgu_ref[:, d_ff + start:d_ff + start + size])
        a = (_silu(gate) * up).astype(BF16)
        y = _dot(a, wd_ref[start:start + size, :])
        acc = y if acc is None else acc + y
    x = x + 0.5 * _rms(acc, g_ref[post:post + 1, :])
    if with_ple:
        p_ref, wg_ref, wp_ref = rest[:3]
        h = _rms(x, g_ref[6:7, :]).astype(BF16)
        gate = jax.nn.sigmoid(_dot(h, wg_ref[...]))
        proj = _dot(p_ref[...].astype(BF16), wp_ref[...])
        x = x + _rms(gate * proj, g_ref[7:8, :])
    o_ref[...] = x


def _ffn(x, gains, wgu, wd, layer, pre, post, tm, ple=None, side=()):
    m, d = x.shape
    d_ff = wd.shape[0]
    args = [x, gains, wgu, wd]
    in_specs = [_row_spec(tm, d), _layer_spec(gains.shape, layer), _const_spec(wgu.shape), _const_spec(wd.shape)]
    if ple is not None:
        p, wg, wp = ple
        args += [p, wg, wp]
        in_specs += [pl.BlockSpec((None, tm, p.shape[2]), lambda i: (layer, i, 0)),
                     _const_spec(wg.shape), _const_spec(wp.shape)]
    s_args, s_in, s_shape, s_out, aliases, layout = _side_operands(len(args), 1, side)
    body = functools.partial(_ffn_body, pre=pre, post=post, d_ff=d_ff, with_ple=ple is not None, layout=layout)
    outs = pl.pallas_call(
        body,
        grid=(m // tm,),
        in_specs=in_specs + s_in,
        out_specs=[_row_spec(tm, d)] + s_out,
        out_shape=[jax.ShapeDtypeStruct((m, d), F32)] + s_shape,
        input_output_aliases=aliases,
        compiler_params=_params("arbitrary"),
        name="ffn",
    )(*args, *s_args)
    return outs[0], outs[1:]


def _rot_ret(x, cos, sin):
    return x * cos + pltpu.roll(x, RET_DK // 2, 1) * sin


def _rot_attn(x, cos, sin_lo, sin_hi):
    return x * cos + pltpu.roll(x, LANES - ROT_DIM // 2, 1) * sin_lo + pltpu.roll(x, ROT_DIM // 2, 1) * sin_hi


def _to_streams(halves, out_ref, lane0, scr_ref, dil):
    if dil == 1:
        for s, v in enumerate(halves):
            out_ref[0, :, lane0 + s * LANES:lane0 + (s + 1) * LANES] = v
        return
    n = halves[0].shape[0] // dil
    for s, v in enumerate(halves):
        scr_ref[s] = v
    for r in range(dil):
        for s in range(2):
            out_ref[r, :, lane0 + s * LANES:lane0 + (s + 1) * LANES] = scr_ref[s, pl.ds(r, n, stride=dil), :]


def _proj_body(*refs, dils):
    x_ref, g_ref, w_ref, wc_ref, bc_ref, tab_ref, hist_ref = refs[:7]
    decode = dils is None
    if decode:
        cy_ref, ulast_ref, ret_ref = refs[7:10]
        attn_refs = refs[10:13]
    else:
        decay_ref, qw_ref, kw_ref, gch_ref = refs[7:11]
        cy_ref, ulast_ref, ret_ref, sout_ref = refs[11:15]
        attn_refs = refs[15:18]
        carry_ref, scr_ref, s_scr = refs[18:21]
    tab = lambda j: tab_ref[:, j * LANES:(j + 1) * LANES]

    x = x_ref[...]
    h = _rms(x, g_ref[2:3, :]).astype(BF16)

    def part(off, width):
        return _dot(h, w_ref[:, off:off + width])

    def mixer_c(gi):
        acos, alo, ahi = tab(2), tab(3), tab(4)
        base = OFF_ATTN + gi * 3 * ATTN_DIM
        zq = part(base, ATTN_DIM)
        zk = part(base + ATTN_DIM, ATTN_DIM)
        zv = part(base + 2 * ATTN_DIM, ATTN_DIM)
        halves = lambda z: [z[:, s * LANES:(s + 1) * LANES] for s in range(2)]
        q_h = [_rot_attn(v, acos, alo, ahi) for v in halves(zq)]
        k_h = [_rot_attn(v, acos, alo, ahi) for v in halves(zk)]
        out_ref = attn_refs[gi]
        for j, vals in enumerate((q_h, k_h, halves(zv))):
            if decode:
                for s, v in enumerate(vals):
                    out_ref[:, j * ATTN_DIM + s * LANES:j * ATTN_DIM + (s + 1) * LANES] = v
            else:
                _to_streams(vals, out_ref, j * ATTN_DIM, scr_ref.at[3 * gi + j], dils[gi])

    def mixer_a():
        b_gate = part(OFF_CONV, CONV_DIM)
        u = part(OFF_CONV + CONV_DIM, CONV_DIM) * part(OFF_CONV + 2 * CONV_DIM, CONV_DIM)
        if decode:
            u2 = hist_ref[0]
            u1 = hist_ref[1]
            ulast_ref[0] = u1
            ulast_ref[1] = u
        else:
            tm = u.shape[0]

            @pl.when(pl.program_id(0) == 0)
            def _():
                carry_ref[...] = hist_ref[...]

            carry = carry_ref[...]
            row = lax.broadcasted_iota(jnp.int32, u.shape, 0)
            u1 = jnp.where(row == 0, carry[7:8, :], pltpu.roll(u, 1, 0))
            u2 = jnp.where(row == 0, carry[6:7, :], jnp.where(row == 1, carry[7:8, :], pltpu.roll(u, 2, 0)))
            carry_ref[...] = u[tm - 8:tm, :]
            ulast_ref[...] = u[tm - 8:tm, :]
        y = bc_ref[...] + wc_ref[0:1, :] * u2
        y = y + wc_ref[1:2, :] * u1
        y = y + wc_ref[2:3, :] * u
        cy_ref[...] = (b_gate * y).astype(cy_ref.dtype)

    def mixer_b_inputs():
        return [part(OFF_RET + j * 512, 512) for j in range(4)]

    def mixer_b(z, heads):
        zq, zk, zv, zg = z
        cos, sin = tab(0), tab(1)
        for hh in heads:
            cols = slice(hh * RET_DK, (hh + 1) * RET_DK)
            q = _rot_ret(zq[:, cols], cos, sin)
            k = _rot_ret(zk[:, cols], cos, sin) * (RET_DK ** -0.5)
            if decode:
                ret_ref[:, cols] = q
                ret_ref[:, 512 + hh * RET_DK:512 + (hh + 1) * RET_DK] = k
                ret_ref[:, 1024 + hh * RET_DK:1024 + (hh + 1) * RET_DK] = zv[:, cols]
                ret_ref[:, 1536 + hh * RET_DK:1536 + (hh + 1) * RET_DK] = zg[:, cols]
                continue
            v = zv[:, cols].astype(BF16)
            s = s_scr[hh]
            scores = _dot_nt(q.astype(BF16), k.astype(BF16)) * decay_ref[hh]
            o = _dot(scores.astype(BF16), v) + _dot((q * qw_ref[hh]).astype(BF16), s.astype(BF16))
            kv = _dot(jnp.transpose(k * kw_ref[hh]).astype(BF16), v)
            s_scr[hh] = s * gch_ref[hh] + kv
            ret_ref[:, cols] = (_head_norm(o) * _silu(zg[:, cols])).astype(ret_ref.dtype)

    if not decode:
        @pl.when(pl.program_id(0) == 0)
        def _():
            s_scr[...] = jnp.zeros_like(s_scr)

    z = mixer_b_inputs()
    mixer_c(2)
    mixer_b(z, (0, 1))
    mixer_c(1)
    mixer_b(z, (2, 3))
    mixer_c(0)
    mixer_a()

    if not decode:
        @pl.when(pl.program_id(0) == pl.num_programs(0) - 1)
        def _():
            sout_ref[...] = s_scr[...]


def _proj(x, gains, w_in, w_conv, b_conv, tabs, hist, layer, tm, decode):
    m, d = x.shape
    n = m // tm
    qkv = 3 * ATTN_DIM
    args = [x, gains, w_in, w_conv, b_conv, tabs, hist]
    state = (RET_HEADS, RET_DK, RET_DK)
    if decode:
        dils = None
        hist_spec = pl.BlockSpec((None, 2, tm, CONV_DIM), lambda i: (layer, 0, i, 0))
        ulast_spec = pl.BlockSpec((2, tm, CONV_DIM), lambda i: (0, i, 0))
        ulast_shape = jax.ShapeDtypeStruct((2, m, CONV_DIM), F32)
        ret_specs = [_row_spec(tm, 4 * 512)]
        ret_shapes = [jax.ShapeDtypeStruct((m, 4 * 512), F32)]
        attn_specs = [_row_spec(tm, qkv)] * 3
        attn_shapes = [jax.ShapeDtypeStruct((m, qkv), F32)] * 3
        scratch = []
    else:
        dils = DILATIONS
        hist_spec = _const_spec((8, CONV_DIM))
        ulast_spec = pl.BlockSpec((8, CONV_DIM), lambda i: (0, 0))
        ulast_shape = jax.ShapeDtypeStruct((8, CONV_DIM), F32)
        ret_specs = [_row_spec(tm, 512), pl.BlockSpec(state, lambda i: (0, 0, 0))]
        ret_shapes = [jax.ShapeDtypeStruct((m, 512), BF16), jax.ShapeDtypeStruct(state, F32)]
        attn_specs = [pl.BlockSpec((dl, tm // dl, qkv), lambda i: (0, i, 0)) for dl in dils]
        attn_shapes = [jax.ShapeDtypeStruct((dl, m // dl, qkv), F32) for dl in dils]
        scratch = [pltpu.VMEM((8, CONV_DIM), F32), pltpu.VMEM((9, 2, tm, LANES), F32), pltpu.VMEM(state, F32)]
        args += list(_ret_tables(tm))
    in_specs = [
        _row_spec(tm, d), _layer_spec(gains.shape, layer), _const_spec(w_in.shape),
        _layer_spec(w_conv.shape, layer), _layer_spec(b_conv.shape, layer), _row_spec(tm, tabs.shape[1]), hist_spec,
    ] + [_const_spec(a.shape) for a in args[7:]]
    out_specs = [_row_spec(tm, CONV_DIM), ulast_spec] + ret_specs + attn_specs
    out_shape = [jax.ShapeDtypeStruct((m, CONV_DIM), BF16), ulast_shape] + ret_shapes + attn_shapes
    return pl.pallas_call(
        functools.partial(_proj_body, dils=dils),
        grid=(n,),
        in_specs=in_specs,
        out_specs=out_specs,
        out_shape=out_shape,
        scratch_shapes=scratch,
        compiler_params=_params("arbitrary"),
        name="proj",
    )(*args)


def _head_norm(o):
    mu = jnp.mean(o, axis=-1, keepdims=True)
    var = jnp.mean(jnp.square(o - mu), axis=-1, keepdims=True)
    return (o - mu) * lax.rsqrt(var + EPS)


def _ret_log_decay():
    return jnp.log1p(-jnp.exp2(-5.0 - jnp.arange(RET_HEADS, dtype=F32)))


def _ret_tables(chunk):
    i = jnp.arange(chunk, dtype=F32)
    log_g = _ret_log_decay()
    diff = i[:, None] - i[None, :]
    decay = jnp.where(diff[None] >= 0, jnp.exp(jnp.maximum(diff, 0.0)[None] * log_g[:, None, None]), 0.0)
    k_w = jnp.exp((chunk - 1 - i)[:, None] * log_g[None, :])
    q_w = jnp.exp((i + 1)[:, None] * log_g[None, :])
    g_chunk = jnp.exp(chunk * log_g)
    qw = jnp.broadcast_to(q_w.T[:, :, None], (RET_HEADS, chunk, RET_DK))
    kw = jnp.broadcast_to(k_w.T[:, :, None], (RET_HEADS, chunk, RET_DK))
    gch = jnp.broadcast_to(g_chunk[:, None, None], (RET_HEADS, RET_DK, RET_DK))
    return decay, qw, kw, gch


def _ret_dec_body(r_ref, gd_ref, s0_ref, *rest):
    y_ref, sout_ref = rest[-2:]
    for b in range(r_ref.shape[0]):
        for hh in range(RET_HEADS):
            part = lambda j: r_ref[b:b + 1, j * 512 + hh * RET_DK:j * 512 + (hh + 1) * RET_DK]
            q, k, v = part(0), part(1), part(2)
            gd = gd_ref[hh]
            s0 = s0_ref[b, hh]
            wide = (RET_DK, RET_DK)
            q_col = jnp.transpose(jnp.broadcast_to(q * gd, wide))
            k_col = jnp.transpose(jnp.broadcast_to(k, wide))
            o_inter = jnp.sum(q_col * s0, axis=0, keepdims=True)
            o_intra = jnp.sum(q * k, axis=-1, keepdims=True) * v
            sout_ref[b, hh] = s0 * gd + k_col * v
            o = o_intra + o_inter
            y_ref[b:b + 1, hh * RET_DK:(hh + 1) * RET_DK] = _head_norm(o) * _silu(part(3))


def _retention_decode(r, gdec, state, layer, prev_out):
    b = r.shape[0]
    bs = DEC_BLOCK_SAMPLES
    while b % bs:
        bs -= 1
    st = pl.BlockSpec((None, bs, RET_HEADS, RET_DK, RET_DK), lambda i: (layer, i, 0, 0, 0))
    args = [r, gdec, state]
    in_specs = [_row_spec(bs, r.shape[1]), _const_spec(gdec.shape), st]
    aliases = {}
    if prev_out is not None:
        args.append(prev_out)
        in_specs.append(pl.BlockSpec(memory_space=pl.ANY))
        aliases = {3: 1}
    return pl.pallas_call(
        _ret_dec_body,
        grid=(b // bs,),
        in_specs=in_specs,
        out_specs=[_row_spec(bs, 512), st],
        out_shape=[jax.ShapeDtypeStruct((b, 512), F32), jax.ShapeDtypeStruct(state.shape, F32)],
        input_output_aliases=aliases,
        compiler_params=_params("parallel"),
        name="retention_decode",
    )(*args)


def _attn_body(cur_ref, prev_ref, out_ref, *, n_sub):
    n = pl.program_id(1)
    qb = ATTN_BLOCK
    ii = lax.broadcasted_iota(jnp.int32, (2 * qb, 2 * qb), 0) & (qb - 1)
    jj = lax.broadcasted_iota(jnp.int32, (2 * qb, 2 * qb), 1)
    band = jnp.logical_and(jj >= ii, jj <= ii + qb)
    first = jnp.logical_and(band, jnp.logical_or(jj >= qb, n > 0))
    lo = lax.broadcasted_iota(jnp.int32, (qb, LANES), 1) < HEAD_DIM
    for hp in range(ATTN_HEADS // 2):
        cols = lambda j: slice(j * ATTN_DIM + hp * LANES, j * ATTN_DIM + (hp + 1) * LANES)
        kk = jnp.concatenate([prev_ref[:, cols(1)], cur_ref[:, cols(1)]], axis=0).astype(BF16)
        vv = jnp.concatenate([prev_ref[:, cols(2)], cur_ref[:, cols(2)]], axis=0).astype(BF16)
        for b in range(n_sub):
            rows = slice(b * qb, (b + 1) * qb)
            q2 = cur_ref[rows, cols(0)] * ATTN_SCALE
            zero = jnp.zeros_like(q2)
            qs = jnp.concatenate([jnp.where(lo, q2, zero), jnp.where(lo, zero, q2)], axis=0).astype(BF16)
            s = jnp.where(first if b == 0 else band, _dot_nt(qs, kk[b * qb:(b + 2) * qb]), NEG)
            mx = jnp.max(s, axis=-1, keepdims=True)
            p = jnp.exp(s - mx)
            den = jnp.sum(p, axis=-1, keepdims=True)
            o = _dot(p.astype(BF16), vv[b * qb:(b + 2) * qb]) / den
            lse = mx + jnp.log(den)
            out_ref[rows, cols(0)] = jnp.where(lo, o[:qb], o[qb:])
            out_ref[rows, cols(1)] = jnp.where(lo, lse[:qb], lse[qb:])


def _attention(qkv):
    dil, length, width = qkv.shape
    step = min(ATTN_STEP, length)
    n_sub = step // ATTN_BLOCK
    cur = pl.BlockSpec((None, step, width), lambda r, n: (r, n, 0))
    prev = pl.BlockSpec((None, ATTN_BLOCK, width), lambda r, n: (r, jnp.maximum(n * n_sub - 1, 0), 0))
    return pl.pallas_call(
        functools.partial(_attn_body, n_sub=n_sub),
        grid=(dil, length // step),
        in_specs=[cur, prev],
        out_specs=pl.BlockSpec((None, step, 2 * ATTN_DIM), lambda r, n: (r, n, 0)),
        out_shape=jax.ShapeDtypeStruct((dil, length, 2 * ATTN_DIM), F32),
        compiler_params=_params("parallel", "arbitrary"),
        name="attention",
    )(qkv, qkv)


def _attn_dec_body(qkv_ref, kc_ref, tail_ref, new_ref, out_ref, cout_ref):
    del new_ref
    rows = BF16_ROWS
    head_of_lane = lax.broadcasted_iota(jnp.int32, (rows, ATTN_DIM), 1) // HEAD_DIM
    own = head_of_lane == lax.broadcasted_iota(jnp.int32, (rows, ATTN_DIM), 0)
    last = lax.broadcasted_iota(jnp.int32, (HEAD_DIM, LANES), 1) == LANES - 1
    wide = lambda a: jnp.broadcast_to(a, (rows, ATTN_DIM))
    pick = lambda a: jnp.sum(jnp.where(own, a, 0.0), axis=0, keepdims=True)
    for b in range(qkv_ref.shape[0]):
        q, kn, vn = (qkv_ref[b:b + 1, j * ATTN_DIM:(j + 1) * ATTN_DIM] for j in range(3))
        q_rows = jnp.where(own, wide(q), 0.0)
        s = _dot(q_rows.astype(BF16), kc_ref[b, 0]) * ATTN_SCALE
        s_new = jnp.sum(q_rows * wide(kn), axis=-1, keepdims=True) * ATTN_SCALE
        mx = jnp.maximum(jnp.max(s, axis=-1, keepdims=True), s_new)
        p = jnp.exp(s - mx)
        p_new = jnp.exp(s_new - mx)
        den = jnp.sum(p, axis=-1, keepdims=True) + p_new
        o = (_dot_nt(p.astype(BF16), kc_ref[b, 1]) + p_new * wide(vn)) / den
        out_ref[b:b + 1, 0:ATTN_DIM] = pick(o)
        out_ref[b:b + 1, ATTN_DIM:2 * ATTN_DIM] = pick(wide(mx + jnp.log(den)))
        new_rows = jnp.concatenate([kn, vn], axis=1)
        new_cols = jnp.transpose(jnp.broadcast_to(new_rows, (LANES, 2 * ATTN_DIM)))
        for kv in range(2):
            for hh in range(ATTN_HEADS):
                col = new_cols[kv * ATTN_DIM + hh * HEAD_DIM:kv * ATTN_DIM + (hh + 1) * HEAD_DIM, :]
                cout_ref[b, kv, hh] = jnp.where(last, col, pltpu.roll(tail_ref[b, kv, hh], LANES - 1, 1))


def _attention_decode(qkv, compact, cache_t, new_cache, layer):
    b = qkv.shape[0]
    window = cache_t.shape[-1]
    bs = DEC_BLOCK_SAMPLES
    while b % bs:
        bs -= 1
    kc = pl.BlockSpec((bs,) + compact.shape[1:], lambda i: (i, 0, 0, 0))
    tail = pl.BlockSpec((None, bs, 2, ATTN_HEADS, HEAD_DIM, LANES),
                        lambda i: (layer, i, 0, 0, 0, window // LANES - 1))
    out, cnew = pl.pallas_call(
        _attn_dec_body,
        grid=(b // bs,),
        in_specs=[_row_spec(bs, qkv.shape[1]), kc, tail, pl.BlockSpec(memory_space=pl.ANY)],
        out_specs=[_row_spec(bs, 2 * ATTN_DIM), tail],
        out_shape=[jax.ShapeDtypeStruct((b, 2 * ATTN_DIM), F32), jax.ShapeDtypeStruct(cache_t.shape, F32)],
        input_output_aliases={3: 1},
        compiler_params=_params("parallel"),
        name="attention_decode",
    )(qkv, compact, cache_t, new_cache)
    return out[None], cnew


def _from_streams(in_ref, lane0, scr_ref, dil):
    if dil == 1:
        return in_ref[0, :, lane0:lane0 + ATTN_DIM]
    n = in_ref.shape[1]
    for r in range(dil):
        for s in range(2):
            scr_ref[s, pl.ds(r, n, stride=dil), :] = in_ref[r, :, lane0 + s * LANES:lane0 + (s + 1) * LANES]
    return jnp.concatenate([scr_ref[0], scr_ref[1]], axis=1)


def _merge_body(x_ref, cy_ref, ry_ref, a0_ref, a1_ref, a2_ref, g_ref,
                wg_ref, wc_ref, wr_ref, wa_ref, wo_ref, *rest, dils, layout):
    n_side_in = sum(n_in for n_in, _, _ in layout)
    n_side_out = sum(n_out for _, n_out, _ in layout)
    out_ref = rest[n_side_in]
    scratch = rest[n_side_in + 1 + n_side_out:]
    _run_side(layout, rest[:n_side_in], rest[n_side_in + 1:n_side_in + 1 + n_side_out])
    x = x_ref[...]
    h = _rms(x, g_ref[2:3, :]).astype(BF16)
    scr = scratch[0] if scratch else None
    d = x.shape[1]
    merged = jax.nn.sigmoid(_dot(h, wg_ref[:, 0:d])) * _dot(cy_ref[...].astype(BF16), wc_ref[...])
    merged = merged + jax.nn.sigmoid(_dot(h, wg_ref[:, d:2 * d])) * _dot(ry_ref[...].astype(BF16), wr_ref[...])
    attn_gate = jax.nn.sigmoid(_dot(h, wg_ref[:, 2 * d:3 * d]))
    vals = []
    for j in range(6):
        ref = (a0_ref, a1_ref, a2_ref)[j // 2]
        vals.append(_from_streams(ref, (j % 2) * ATTN_DIM, None if scr is None else scr.at[j], dils[j // 2]))
    o0, l0, o1, l1, o2, l2 = vals
    mx = jnp.maximum(jnp.maximum(l0, l1), l2)
    e0, e1, e2 = jnp.exp(l0 - mx), jnp.exp(l1 - mx), jnp.exp(l2 - mx)
    den = e0 + e1 + e2
    attn_y = ((e0 / den) * o0 + (e1 / den) * o1 + (e2 / den) * o2).astype(BF16)
    merged = merged + attn_gate * _dot(attn_y, wa_ref[...])
    out_ref[...] = x + _rms(_dot(merged.astype(BF16), wo_ref[...]), g_ref[3:4, :])


def _merge(x, cy, ry, attn, dils, gains, wg, wc, wr, wa, wo, layer, tm, side=()):
    m, d = x.shape
    a_specs = [pl.BlockSpec((dl, tm // dl, 2 * ATTN_DIM), lambda i: (0, i, 0)) for dl in dils]
    scratch = [pltpu.VMEM((6, 2, tm, LANES), F32)] if max(dils) > 1 else []
    args = [x, cy, ry, *attn, gains, wg, wc, wr, wa, wo]
    s_args, s_in, s_shape, s_out, aliases, layout = _side_operands(len(args), 1, side)
    outs = pl.pallas_call(
        functools.partial(_merge_body, dils=dils, layout=layout),
        grid=(m // tm,),
        in_specs=[_row_spec(tm, d), _row_spec(tm, CONV_DIM), _row_spec(tm, 512)] + a_specs
        + [_layer_spec(gains.shape, layer)] + [_const_spec(a.shape) for a in (wg, wc, wr, wa, wo)] + s_in,
        out_specs=[_row_spec(tm, d)] + s_out,
        out_shape=[jax.ShapeDtypeStruct((m, d), F32)] + s_shape,
        scratch_shapes=scratch,
        input_output_aliases=aliases,
        compiler_params=_params("arbitrary"),
        name="merge",
    )(*args, *s_args)
    return outs[0], outs[1:]


def _cos_sin(start, count, freq, split):
    if split is None or count % split:
        ang = (start + jnp.arange(count)).astype(F32)[:, None] * freq[None, :]
        return jnp.cos(ang), jnp.sin(ang)
    base = (start + split * jnp.arange(count // split)).astype(F32)[:, None, None] * freq
    off = jnp.arange(split).astype(F32)[None, :, None] * freq
    cb, sb, co, so = jnp.cos(base), jnp.sin(base), jnp.cos(off), jnp.sin(off)
    return (cb * co - sb * so).reshape(count, LANES), (sb * co + cb * so).reshape(count, LANES)


def _rotary_tables(start, count, split=None):
    lane = jnp.arange(LANES)
    half = RET_DK // 2
    freq = jnp.exp(-(lane % half).astype(F32) * (math.log(RET_THETA) / half))
    cos, sin = _cos_sin(start, count, freq, split)
    rc, rs = cos, jnp.where(lane < half, -sin, sin)
    dim = lane % HEAD_DIM
    half = ROT_DIM // 2
    freq = jnp.exp(-(dim % half).astype(F32) * (math.log(ROPE_THETA) / half))
    cos, sin = _cos_sin(start, count, freq, split)
    ac = jnp.where(dim < ROT_DIM, cos, 1.0)
    alo = jnp.where(dim < half, -sin, 0.0)
    ahi = jnp.where(jnp.logical_and(dim >= half, dim < ROT_DIM), sin, 0.0)
    return rc, rs, ac, alo, ahi


def _finish(x, ple, cy, ry, attn, dils, w, layer, tm, merge_side=(), ffn_side=()):
    x, merge_out = _merge(x, cy, ry, attn, dils, w['gains'], w['w_gate'], w['w_conv_out'], w['w_ret_out'],
                          w['w_attn_out'], w['w_o'], layer, tm, side=merge_side)
    x, ffn_out = _ffn(x, w['gains'], w['ffn2_gu'], w['ffn2_down'], layer, 4, 5, tm,
                      ple=(ple, w['w_ple_gate'], w['w_ple_proj']), side=ffn_side)
    return x, merge_out, ffn_out


W_FFN1 = ('ffn1_gu', 'ffn1_down')
W_EARLY = W_FFN1 + ('w_in',)
W_LATE = ('w_gate', 'w_conv_out', 'w_ret_out', 'w_attn_out', 'w_o', 'ffn2_gu', 'ffn2_down', 'w_ple_gate',
          'w_ple_proj')


def _prompt_layer(x, ple, tabs, caches_t, new_kv, raw, w, layer):
    depth = raw['w_in'].shape[0]
    s = x.shape[0]
    tm = min(ROW_TILE, s)
    steps = s // tm
    wide_dil = DILATIONS[2]
    todo = tuple(n for n in W_EARLY[2:] + W_LATE if n not in w)
    side = [_shift_job(caches_t[2], new_kv[2], None, layer, steps, 0, wide_dil)]
    if todo:
        side.append(_convert_job([raw[n] for n in todo], layer, steps))
    x, outs = _ffn(x, w['gains'], w['ffn1_gu'], w['ffn1_down'], layer, 0, 1, tm, side=side)
    wide, wide_c = outs[:2]
    w = dict(w, **dict(zip(todo, outs[2:])))
    hist = jnp.zeros((8, CONV_DIM), F32)
    outs = _proj(x, w['gains'], w['w_in'], w['w_conv'], w['b_conv'], tabs, hist, layer, tm, decode=False)
    cy, ulast, ry, ret_state = outs[:4]
    attn, kv = [], []
    for gi, (window, dil) in enumerate(ATTN_GROUPS):
        qkv = outs[4 + gi]
        attn.append(_attention(qkv))
        keep = min(window, s) // dil
        tail = jnp.swapaxes(qkv[:, s // dil - keep:, ATTN_DIM:], 0, 1)
        kv.append(tail.reshape(1, keep * dil, 2, ATTN_HEADS, HEAD_DIM))
    merge_side = [_shift_job(caches_t[gi], new_kv[gi], None, layer, steps, None, DILATIONS[gi]) for gi in (0, 1)]
    ffn_side = [_shift_job(caches_t[2], wide, wide_c, layer, steps, 1, wide_dil)]
    if layer + 1 < depth:
        merge_side.append(_convert_job([raw[n] for n in W_EARLY], layer + 1, steps))
        ffn_side.append(_convert_job([raw[n] for n in W_LATE], layer + 1, steps))
    x, m_out, f_out = _finish(x, ple, cy, ry, attn, DILATIONS, w, layer, tm, merge_side, ffn_side)
    w_next = dict(zip(W_EARLY, m_out[4:]), **dict(zip(W_LATE, f_out[2:])))
    new_kv = [m_out[0], m_out[2], f_out[0]]
    compact = [m_out[1], m_out[3], f_out[1]]
    return x, ulast[6:8][None], ret_state[None], kv, new_kv, compact, w, w_next


def _sample_layer(x, ple, tabs, gdec, conv_hist, ret_state, caches_t, prev_ret, new_kv, compact, w, layer):
    b = x.shape[0]
    x, _ = _ffn(x, w['gains'], w['ffn1_gu'], w['ffn1_down'], layer, 0, 1, b)
    outs = _proj(x, w['gains'], w['w_in'], w['w_conv'], w['b_conv'], tabs, conv_hist, layer, b, decode=True)
    cy, ulast, ret_in = outs[:3]
    ry, ret_new = _retention_decode(ret_in, gdec, ret_state, layer, prev_ret)
    attn, kv = [], []
    for gi in range(len(ATTN_GROUPS)):
        o_lse, cnew = _attention_decode(outs[3 + gi], compact[gi], caches_t[gi], new_kv[gi], layer)
        attn.append(o_lse)
        kv.append(cnew)
    x, _, _ = _finish(x, ple, cy, ry, attn, (1, 1, 1), w, layer, b)
    return x, jnp.swapaxes(ulast, 0, 1), ret_new, kv


def kernel(x_prompt, x_sample, state_conv, state_ret, cache_kv_w128, cache_kv_w512, cache_kv_w2048, p_prompt, p_sample, norm_gain, w_ffn1_gu, w_ffn1_down, w_in, w_conv, b_conv, w_conv_out, w_ret_out, w_attn_out, w_gate, w_o, w_ffn2_gu, w_ffn2_down, w_ple_gate, w_ple_proj):
    depth = norm_gain.shape[0]
    seq = x_prompt.shape[1]
    nb = x_sample.shape[0]
    assert x_prompt.shape[0] == 1 and x_sample.shape[1] == 1

    tabs_p = jnp.concatenate(_rotary_tables(0, seq, split=ROT_SPLIT), axis=1)
    tabs_s = jnp.broadcast_to(jnp.concatenate(_rotary_tables(PAST_LEN, 1), axis=1), (nb, 5 * LANES))
    gdec = jnp.broadcast_to(jnp.exp(_ret_log_decay())[:, None, None], (RET_HEADS, 1, RET_DK))
    caches_t = [jnp.transpose(c, (0, 1, 3, 4, 5, 2)) for c in (cache_kv_w128, cache_kv_w512, cache_kv_w2048)]
    raw = dict(ffn1_gu=w_ffn1_gu, ffn1_down=w_ffn1_down, w_in=w_in, w_conv_out=w_conv_out, w_ret_out=w_ret_out,
               w_attn_out=w_attn_out, w_gate=w_gate, w_o=w_o, ffn2_gu=w_ffn2_gu, ffn2_down=w_ffn2_down,
               w_ple_gate=w_ple_gate, w_ple_proj=w_ple_proj)
    shared = dict(gains=norm_gain, w_conv=w_conv, b_conv=b_conv.reshape(depth, 1, CONV_DIM))
    w_next = {n: raw[n][0].astype(BF16) for n in W_FFN1}
    ple_p = p_prompt[:, 0]
    ple_s = p_sample[:, :, 0]
    conv_hist = jnp.swapaxes(state_conv, 1, 2)

    yp, ys = x_prompt[0], x_sample[:, 0]
    conv_p, conv_s, ret_p = [], [], []
    kv_p = [[] for _ in ATTN_GROUPS]
    ret_s, kv_s = None, [None] * len(ATTN_GROUPS)
    for l in range(depth):
        yp, cp, rp, kp, kv_s, compact, w, w_next = _prompt_layer(yp, ple_p, tabs_p, caches_t, kv_s, raw,
                                                                 dict(shared, **w_next), l)
        ys, cs, ret_s, kv_s = _sample_layer(ys, ple_s, tabs_s, gdec, conv_hist, state_ret, caches_t, ret_s, kv_s,
                                            compact, w, l)
        conv_p.append(cp)
        conv_s.append(cs)
        ret_p.append(rp)
        for gi in range(len(ATTN_GROUPS)):
            kv_p[gi].append(kp[gi])
    back = lambda c: jnp.transpose(c, (0, 1, 5, 2, 3, 4))
    return (yp[None], ys[:, None], jnp.stack(conv_p), jnp.stack(conv_s), jnp.stack(ret_p), ret_s,
            jnp.stack(kv_p[0]), back(kv_s[0]), jnp.stack(kv_p[1]), back(kv_s[1]),
            jnp.stack(kv_p[2]), back(kv_s[2]))
```

```python
import functools
import math

import jax
import jax.numpy as jnp
from jax import lax
from jax.experimental import pallas as pl
from jax.experimental.pallas import tpu as pltpu

F32 = jnp.float32
BF16 = jnp.bfloat16

EPS = 1e-6
PAST_LEN = 16384
CONV_DIM = 512
RET_HEADS = 4
RET_DK = 128
RET_THETA = 10000.0
ATTN_GROUPS = ((128, 1), (512, 4), (2048, 16))
DILATIONS = tuple(d for _, d in ATTN_GROUPS)
ATTN_HEADS = 4
HEAD_DIM = 64
ATTN_DIM = ATTN_HEADS * HEAD_DIM
ATTN_SCALE = HEAD_DIM ** -0.5
ROT_DIM = 16
ROPE_THETA = 500000.0
ROT_SPLIT = 128
ATTN_BLOCK = 128
ATTN_STEP = 2048
DEC_BLOCK_WINDOW = 2048
DEC_BLOCK_SAMPLES = 8
NEG = -1e30
LANES = 128
MXU_WIDTH = 256
BF16_ROWS = 16
OFF_CONV = 0
OFF_RET = 3 * CONV_DIM
OFF_ATTN = OFF_RET + 4 * 512

VMEM_LIMIT_BYTES = 60000 * 1024
ROW_TILE = 512


def _params(*sem):
    return pltpu.CompilerParams(dimension_semantics=sem, vmem_limit_bytes=VMEM_LIMIT_BYTES)


def _const_spec(shape):
    zeros = (0,) * len(shape)
    return pl.BlockSpec(shape, lambda *_: zeros, pipeline_mode=pl.Buffered(1))


def _layer_spec(shape, layer):
    tail = tuple(shape[1:])
    idx = (layer,) + (0,) * len(tail)
    return pl.BlockSpec((None,) + tail, lambda *_: idx, pipeline_mode=pl.Buffered(1))


def _row_spec(tm, width):
    return pl.BlockSpec((tm, width), lambda i: (i, 0))


def _rms(x, g):
    return x * lax.rsqrt(jnp.mean(x * x, axis=-1, keepdims=True) + EPS) * g


def _dot(a, b):
    return jnp.dot(a, b, preferred_element_type=F32)


def _dot_nt(a, b):
    return lax.dot_general(a, b, (((1,), (1,)), ((), ())), preferred_element_type=F32)


def _silu(x):
    return x * jax.nn.sigmoid(x)


def _ff_chunks(d_ff):
    first = -(-(d_ff // 2) // MXU_WIDTH) * MXU_WIDTH
    return ((0, first), (first, d_ff - first)) if 0 < first < d_ff else ((0, d_ff),)


def _side_operands(n_args, n_outs, side):
    args, in_specs, out_shape, out_specs, aliases, layout = [], [], [], [], {}, []
    for job in side:
        for src, dst in job['aliases'].items():
            aliases[n_args + len(args) + src] = n_outs + len(out_shape) + dst
        layout.append((len(job['args']), len(job['out_shape']), job['fn']))
        args += job['args']
        in_specs += job['in_specs']
        out_shape += job['out_shape']
        out_specs += job['out_specs']
    return args, in_specs, out_shape, out_specs, aliases, tuple(layout)


def _run_side(layout, in_refs, out_refs):
    i = o = 0
    for n_in, n_out, fn in layout:
        fn(in_refs[i:i + n_in], out_refs[o:o + n_out])
        i += n_in
        o += n_out


def _shift_job(cache_t, new_cache, compact, layer, n_steps, kv, dil):
    _, b, _, nh, hd, window = cache_t.shape
    n_pos = window // dil
    bs = -(-b // n_steps)
    assert b % bs == 0
    n_blk = b // bs
    n_kv, kv_idx = (2, 0) if kv is None else (1, kv)
    blk = lambda i: jnp.minimum(i, n_blk - 1)
    spec = pl.BlockSpec((None, bs, n_kv, nh, hd, window), lambda i: (layer, blk(i), kv_idx, 0, 0, 0))
    cspec = pl.BlockSpec((bs, n_kv, nh * hd, n_pos), lambda i: (blk(i), kv_idx, 0, 0))
    pos = jnp.arange(window)[:, None] == dil * jnp.arange(n_pos)[None, :]
    pick = pos.astype(BF16)

    def fn(in_refs, out_refs):
        src, pick_ref = in_refs[:2]
        dst, cdst = out_refs
        for s in range(bs):
            for j in range(n_kv):
                for hh in range(nh):
                    t = src[s, j, hh]
                    dst[s, j, hh] = pltpu.roll(t, window - 1, 1)
                    tb = t.astype(BF16)
                    if dil > 1:
                        tb = _dot(tb, pick_ref[...]).astype(BF16)
                    cdst[s, j, hh * hd:(hh + 1) * hd, :] = tb

    job = dict(args=[cache_t, pick], in_specs=[spec, _const_spec(pick.shape)],
               out_shape=[jax.ShapeDtypeStruct(cache_t.shape, cache_t.dtype),
                          jax.ShapeDtypeStruct((b, 2, nh * hd, n_pos), BF16)],
               out_specs=[spec, cspec], aliases={}, fn=fn)
    for out_idx, buf in enumerate((new_cache, compact)):
        if buf is not None:
            job['aliases'][len(job['args'])] = out_idx
            job['args'].append(buf)
            job['in_specs'].append(pl.BlockSpec(memory_space=pl.ANY))
    return job


def _convert_job(weights, layer, n_steps):
    job = dict(args=[], in_specs=[], out_shape=[], out_specs=[], aliases={})
    for wt in weights:
        _, k, n = wt.shape
        n_band = max(c for c in range(1, n_steps + 1)
                     if n_steps % c == 0 and k % c == 0 and ((k // c) % BF16_ROWS == 0 or c == 1))
        band = lambda i, n_band=n_band: i * n_band // n_steps
        job['args'].append(wt)
        job['in_specs'].append(pl.BlockSpec((None, k // n_band, n), lambda i, band=band: (layer, band(i), 0)))
        job['out_shape'].append(jax.ShapeDtypeStruct((k, n), BF16))
        job['out_specs'].append(pl.BlockSpec((k // n_band, n), lambda i, band=band: (band(i), 0)))

    def fn(in_refs, out_refs):
        for src, dst in zip(in_refs, out_refs):
            dst[...] = src[...].astype(BF16)

    job['fn'] = fn
    return job


def _ffn_body(x_ref, g_ref, wgu_ref, wd_ref, *rest, pre, post, d_ff, with_ple, layout):
    n_ple = 3 if with_ple else 0
    n_side_in = sum(n_in for n_in, _, _ in layout)
    o_ref = rest[n_ple + n_side_in]
    _run_side(layout, rest[n_ple:n_ple + n_side_in], rest[n_ple + n_side_in + 1:])
    x = x_ref[...]
    h = _rms(x, g_ref[pre:pre + 1, :]).astype(BF16)
    tm = x.shape[0]
    halves = (slice(0, tm // 2), slice(tm // 2, tm)) if tm % (2 * MXU_WIDTH) == 0 else (slice(0, tm),)
    chunks = _ff_chunks(d_ff)
    acc = None
    for ci, (start, size) in enumerate(chunks):
        wg = wgu_ref[:, start:start + size]
        if ci == 0:
            gate = jnp.concatenate([_dot(h[r], wg) for r in halves], axis=0)
        else:
            gate = _dot(h, wg)
        up = _dot(h, wgu_ref[:, d_ff + start:d_ff + start + size])
        a = (_silu(gate) * up).astype(BF16)
        if ci < len(chunks) - 1:
            y = _dot(a, wd_ref[start:start + size, :])
            acc = y if acc is None else acc + y
    out = []
    for r in halves:
        y = _dot(a[r], wd_ref[start:start + size, :])
        y = y if acc is None else acc[r] + y
        out.append(x[r] + 0.5 * _rms(y, g_ref[post:post + 1, :]))
    if with_ple:
        p_ref, wg_ref, wp_ref = rest[:3]
        proj = _dot(p_ref[...].astype(BF16), wp_ref[...])
        for i, r in enumerate(halves):
            h = _rms(out[i], g_ref[6:7, :]).astype(BF16)
            gate = jax.nn.sigmoid(_dot(h, wg_ref[...]))
            out[i] = out[i] + _rms(gate * proj[r], g_ref[7:8, :])
    for r, v in zip(halves, out):
        o_ref[r, :] = v


def _ffn(x, gains, wgu, wd, layer, pre, post, tm, ple=None, side=()):
    m, d = x.shape
    d_ff = wd.shape[0]
    args = [x, gains, wgu, wd]
    in_specs = [_row_spec(tm, d), _layer_spec(gains.shape, layer), _const_spec(wgu.shape), _const_spec(wd.shape)]
    if ple is not None:
        p, wg, wp = ple
        args += [p, wg, wp]
        in_specs += [pl.BlockSpec((None, tm, p.shape[2]), lambda i: (layer, i, 0)),
                     _const_spec(wg.shape), _const_spec(wp.shape)]
    s_args, s_in, s_shape, s_out, aliases, layout = _side_operands(len(args), 1, side)
    body = functools.partial(_ffn_body, pre=pre, post=post, d_ff=d_ff, with_ple=ple is not None, layout=layout)
    outs = pl.pallas_call(
        body,
        grid=(m // tm,),
        in_specs=in_specs + s_in,
        out_specs=[_row_spec(tm, d)] + s_out,
        out_shape=[jax.ShapeDtypeStruct((m, d), F32)] + s_shape,
        input_output_aliases=aliases,
        compiler_params=_params("arbitrary"),
        name="ffn",
    )(*args, *s_args)
    return outs[0], outs[1:]


def _rot_ret(x, cos, sin):
    return x * cos + pltpu.roll(x, RET_DK // 2, 1) * sin


def _rot_attn(x, cos, sin_lo, sin_hi):
    return x * cos + pltpu.roll(x, LANES - ROT_DIM // 2, 1) * sin_lo + pltpu.roll(x, ROT_DIM // 2, 1) * sin_hi


def _to_streams(halves, out_ref, lane0, scr_ref, dil):
    if dil == 1:
        for s, v in enumerate(halves):
            out_ref[0, :, lane0 + s * LANES:lane0 + (s + 1) * LANES] = v
        return
    n = halves[0].shape[0] // dil
    for s, v in enumerate(halves):
        scr_ref[s] = v
    for r in range(dil):
        for s in range(2):
            out_ref[r, :, lane0 + s * LANES:lane0 + (s + 1) * LANES] = scr_ref[s, pl.ds(r, n, stride=dil), :]


def _proj_body(*refs, dils):
    x_ref, g_ref, w_ref, wc_ref, bc_ref, tab_ref, hist_ref = refs[:7]
    decode = dils is None
    if decode:
        cy_ref, ulast_ref, ret_ref = refs[7:10]
        attn_refs = refs[10:13]
    else:
        decay_ref, qw_ref, kw_ref, gch_ref = refs[7:11]
        cy_ref, ulast_ref, ret_ref, sout_ref = refs[11:15]
        attn_refs = refs[15:18]
        carry_ref, scr_ref, s_scr = refs[18:21]
    tab = lambda j: tab_ref[:, j * LANES:(j + 1) * LANES]

    x = x_ref[...]
    h = _rms(x, g_ref[2:3, :]).astype(BF16)

    def part(off, width):
        return _dot(h, w_ref[:, off:off + width])

    def mixer_c(gi):
        acos, alo, ahi = tab(2), tab(3), tab(4)
        base = OFF_ATTN + gi * 3 * ATTN_DIM
        zq = part(base, ATTN_DIM)
        zk = part(base + ATTN_DIM, ATTN_DIM)
        zv = part(base + 2 * ATTN_DIM, ATTN_DIM)
        halves = lambda z: [z[:, s * LANES:(s + 1) * LANES] for s in range(2)]
        q_h = [_rot_attn(v, acos, alo, ahi) for v in halves(zq)]
        k_h = [_rot_attn(v, acos, alo, ahi) for v in halves(zk)]
        out_ref = attn_refs[gi]
        for j, vals in enumerate((q_h, k_h, halves(zv))):
            if decode:
                for s, v in enumerate(vals):
                    out_ref[:, j * ATTN_DIM + s * LANES:j * ATTN_DIM + (s + 1) * LANES] = v
            else:
                _to_streams(vals, out_ref, j * ATTN_DIM, scr_ref.at[3 * gi + j], dils[gi])

    def mixer_a():
        b_gate = part(OFF_CONV, CONV_DIM)
        u = part(OFF_CONV + CONV_DIM, CONV_DIM) * part(OFF_CONV + 2 * CONV_DIM, CONV_DIM)
        if decode:
            u2 = hist_ref[0]
            u1 = hist_ref[1]
            ulast_ref[0] = u1
            ulast_ref[1] = u
        else:
            tm = u.shape[0]

            @pl.when(pl.program_id(0) == 0)
            def _():
                carry_ref[...] = hist_ref[...]

            carry = carry_ref[...]
            row = lax.broadcasted_iota(jnp.int32, u.shape, 0)
            u1 = jnp.where(row == 0, carry[7:8, :], pltpu.roll(u, 1, 0))
            u2 = jnp.where(row == 0, carry[6:7, :], jnp.where(row == 1, carry[7:8, :], pltpu.roll(u, 2, 0)))
            carry_ref[...] = u[tm - 8:tm, :]
            ulast_ref[...] = u[tm - 8:tm, :]
        y = bc_ref[...] + wc_ref[0:1, :] * u2
        y = y + wc_ref[1:2, :] * u1
        y = y + wc_ref[2:3, :] * u
        cy_ref[...] = (b_gate * y).astype(cy_ref.dtype)

    def mixer_b_inputs():
        tm = h.shape[0]
        halves = (slice(0, tm // 2), slice(tm // 2, tm)) if tm % (2 * MXU_WIDTH) == 0 else (slice(0, tm),)
        zq = jnp.concatenate([_dot(h[r], w_ref[:, OFF_RET:OFF_RET + 512]) for r in halves], axis=0)
        return [zq] + [part(OFF_RET + j * 512, 512) for j in range(1, 4)]

    def mixer_b(z, heads):
        zq, zk, zv, zg = z
        cos, sin = tab(0), tab(1)
        for hh in heads:
            cols = slice(hh * RET_DK, (hh + 1) * RET_DK)
            q = _rot_ret(zq[:, cols], cos, sin)
            k = _rot_ret(zk[:, cols], cos, sin) * (RET_DK ** -0.5)
            if decode:
                ret_ref[:, cols] = q
                ret_ref[:, 512 + hh * RET_DK:512 + (hh + 1) * RET_DK] = k
                ret_ref[:, 1024 + hh * RET_DK:1024 + (hh + 1) * RET_DK] = zv[:, cols]
                ret_ref[:, 1536 + hh * RET_DK:1536 + (hh + 1) * RET_DK] = zg[:, cols]
                continue
            v = zv[:, cols].astype(BF16)
            s = s_scr[hh]
            scores = _dot_nt(q.astype(BF16), k.astype(BF16)) * decay_ref[hh]
            o = _dot(scores.astype(BF16), v) + _dot((q * qw_ref[hh]).astype(BF16), s.astype(BF16))
            kv = _dot(jnp.transpose(k * kw_ref[hh]).astype(BF16), v)
            s_scr[hh] = s * gch_ref[hh] + kv
            ret_ref[:, cols] = (_head_norm(o) * _silu(zg[:, cols])).astype(ret_ref.dtype)

    if not decode:
        @pl.when(pl.program_id(0) == 0)
        def _():
            s_scr[...] = jnp.zeros_like(s_scr)

    z = mixer_b_inputs()
    mixer_c(2)
    mixer_b(z, (0, 1))
    mixer_c(1)
    mixer_b(z, (2, 3))
    mixer_c(0)
    mixer_a()

    if not decode:
        @pl.when(pl.program_id(0) == pl.num_programs(0) - 1)
        def _():
            sout_ref[...] = s_scr[...]


def _proj(x, gains, w_in, w_conv, b_conv, tabs, hist, layer, tm, decode):
    m, d = x.shape
    n = m // tm
    qkv = 3 * ATTN_DIM
    args = [x, gains, w_in, w_conv, b_conv, tabs, hist]
    state = (RET_HEADS, RET_DK, RET_DK)
    if decode:
        dils = None
        hist_spec = pl.BlockSpec((None, 2, tm, CONV_DIM), lambda i: (layer, 0, i, 0))
        ulast_spec = pl.BlockSpec((2, tm, CONV_DIM), lambda i: (0, i, 0))
        ulast_shape = jax.ShapeDtypeStruct((2, m, CONV_DIM), F32)
        ret_specs = [_row_spec(tm, 4 * 512)]
        ret_shapes = [jax.ShapeDtypeStruct((m, 4 * 512), F32)]
        attn_specs = [_row_spec(tm, qkv)] * 3
        attn_shapes = [jax.ShapeDtypeStruct((m, qkv), F32)] * 3
        scratch = []
    else:
        dils = DILATIONS
        hist_spec = _const_spec((8, CONV_DIM))
        ulast_spec = pl.BlockSpec((8, CONV_DIM), lambda i: (0, 0))
        ulast_shape = jax.ShapeDtypeStruct((8, CONV_DIM), F32)
        ret_specs = [_row_spec(tm, 512), pl.BlockSpec(state, lambda i: (0, 0, 0))]
        ret_shapes = [jax.ShapeDtypeStruct((m, 512), BF16), jax.ShapeDtypeStruct(state, F32)]
        attn_specs = [pl.BlockSpec((dl, tm // dl, qkv), lambda i: (0, i, 0)) for dl in dils]
        attn_shapes = [jax.ShapeDtypeStruct((dl, m // dl, qkv), F32) for dl in dils]
        scratch = [pltpu.VMEM((8, CONV_DIM), F32), pltpu.VMEM((9, 2, tm, LANES), F32), pltpu.VMEM(state, F32)]
        args += list(_ret_tables(tm))
    in_specs = [
        _row_spec(tm, d), _layer_spec(gains.shape, layer), _const_spec(w_in.shape),
        _layer_spec(w_conv.shape, layer), _layer_spec(b_conv.shape, layer), _row_spec(tm, tabs.shape[1]), hist_spec,
    ] + [_const_spec(a.shape) for a in args[7:]]
    out_specs = [_row_spec(tm, CONV_DIM), ulast_spec] + ret_specs + attn_specs
    out_shape = [jax.ShapeDtypeStruct((m, CONV_DIM), BF16), ulast_shape] + ret_shapes + attn_shapes
    return pl.pallas_call(
        functools.partial(_proj_body, dils=dils),
        grid=(n,),
        in_specs=in_specs,
        out_specs=out_specs,
        out_shape=out_shape,
        scratch_shapes=scratch,
        compiler_params=_params("arbitrary"),
        name="proj",
    )(*args)


def _head_norm(o):
    mu = jnp.mean(o, axis=-1, keepdims=True)
    var = jnp.mean(jnp.square(o - mu), axis=-1, keepdims=True)
    return (o - mu) * lax.rsqrt(var + EPS)


def _ret_log_decay():
    return jnp.log1p(-jnp.exp2(-5.0 - jnp.arange(RET_HEADS, dtype=F32)))


def _ret_tables(chunk):
    i = jnp.arange(chunk, dtype=F32)
    log_g = _ret_log_decay()
    diff = i[:, None] - i[None, :]
    decay = jnp.where(diff[None] >= 0, jnp.exp(jnp.maximum(diff, 0.0)[None] * log_g[:, None, None]), 0.0)
    k_w = jnp.exp((chunk - 1 - i)[:, None] * log_g[None, :])
    q_w = jnp.exp((i + 1)[:, None] * log_g[None, :])
    g_chunk = jnp.exp(chunk * log_g)
    qw = jnp.broadcast_to(q_w.T[:, :, None], (RET_HEADS, chunk, RET_DK))
    kw = jnp.broadcast_to(k_w.T[:, :, None], (RET_HEADS, chunk, RET_DK))
    gch = jnp.broadcast_to(g_chunk[:, None, None], (RET_HEADS, RET_DK, RET_DK))
    return decay, qw, kw, gch


def _ret_dec_body(r_ref, gd_ref, s0_ref, *rest):
    y_ref, sout_ref = rest[-2:]
    for b in range(r_ref.shape[0]):
        for hh in range(RET_HEADS):
            part = lambda j: r_ref[b:b + 1, j * 512 + hh * RET_DK:j * 512 + (hh + 1) * RET_DK]
            q, k, v = part(0), part(1), part(2)
            gd = gd_ref[hh]
            s0 = s0_ref[b, hh]
            wide = (RET_DK, RET_DK)
            q_col = jnp.transpose(jnp.broadcast_to(q * gd, wide))
            k_col = jnp.transpose(jnp.broadcast_to(k, wide))
            o_inter = jnp.sum(q_col * s0, axis=0, keepdims=True)
            o_intra = jnp.sum(q * k, axis=-1, keepdims=True) * v
            sout_ref[b, hh] = s0 * gd + k_col * v
            o = o_intra + o_inter
            y_ref[b:b + 1, hh * RET_DK:(hh + 1) * RET_DK] = _head_norm(o) * _silu(part(3))


def _retention_decode(r, gdec, state, layer, prev_out):
    b = r.shape[0]
    bs = DEC_BLOCK_SAMPLES
    while b % bs:
        bs -= 1
    st = pl.BlockSpec((None, bs, RET_HEADS, RET_DK, RET_DK), lambda i: (layer, i, 0, 0, 0))
    args = [r, gdec, state]
    in_specs = [_row_spec(bs, r.shape[1]), _const_spec(gdec.shape), st]
    aliases = {}
    if prev_out is not None:
        args.append(prev_out)
        in_specs.append(pl.BlockSpec(memory_space=pl.ANY))
        aliases = {3: 1}
    return pl.pallas_call(
        _ret_dec_body,
        grid=(b // bs,),
        in_specs=in_specs,
        out_specs=[_row_spec(bs, 512), st],
        out_shape=[jax.ShapeDtypeStruct((b, 512), F32), jax.ShapeDtypeStruct(state.shape, F32)],
        input_output_aliases=aliases,
        compiler_params=_params("parallel"),
        name="retention_decode",
    )(*args)


def _attn_body(cur_ref, prev_ref, out_ref, *, n_sub):
    n = pl.program_id(1)
    qb = ATTN_BLOCK
    ii = lax.broadcasted_iota(jnp.int32, (2 * qb, 2 * qb), 0) & (qb - 1)
    jj = lax.broadcasted_iota(jnp.int32, (2 * qb, 2 * qb), 1)
    band = jnp.logical_and(jj >= ii, jj <= ii + qb)
    first = jnp.logical_and(band, jnp.logical_or(jj >= qb, n > 0))
    lo = lax.broadcasted_iota(jnp.int32, (qb, LANES), 1) < HEAD_DIM
    for hp in range(ATTN_HEADS // 2):
        cols = lambda j: slice(j * ATTN_DIM + hp * LANES, j * ATTN_DIM + (hp + 1) * LANES)
        kk = jnp.concatenate([prev_ref[:, cols(1)], cur_ref[:, cols(1)]], axis=0).astype(BF16)
        vv = jnp.concatenate([prev_ref[:, cols(2)], cur_ref[:, cols(2)]], axis=0).astype(BF16)
        for b in range(n_sub):
            rows = slice(b * qb, (b + 1) * qb)
            q2 = cur_ref[rows, cols(0)] * ATTN_SCALE
            zero = jnp.zeros_like(q2)
            qs = jnp.concatenate([jnp.where(lo, q2, zero), jnp.where(lo, zero, q2)], axis=0).astype(BF16)
            s = jnp.where(first if b == 0 else band, _dot_nt(qs, kk[b * qb:(b + 2) * qb]), NEG)
            mx = jnp.max(s, axis=-1, keepdims=True)
            p = jnp.exp(s - mx)
            den = jnp.sum(p, axis=-1, keepdims=True)
            o = _dot(p.astype(BF16), vv[b * qb:(b + 2) * qb]) / den
            lse = mx + jnp.log(den)
            out_ref[rows, cols(0)] = jnp.where(lo, o[:qb], o[qb:])
            out_ref[rows, cols(1)] = jnp.where(lo, lse[:qb], lse[qb:])


def _attention(qkv):
    dil, length, width = qkv.shape
    step = min(ATTN_STEP, length)
    n_sub = step // ATTN_BLOCK
    cur = pl.BlockSpec((None, step, width), lambda r, n: (r, n, 0))
    prev = pl.BlockSpec((None, ATTN_BLOCK, width), lambda r, n: (r, jnp.maximum(n * n_sub - 1, 0), 0))
    return pl.pallas_call(
        functools.partial(_attn_body, n_sub=n_sub),
        grid=(dil, length // step),
        in_specs=[cur, prev],
        out_specs=pl.BlockSpec((None, step, 2 * ATTN_DIM), lambda r, n: (r, n, 0)),
        out_shape=jax.ShapeDtypeStruct((dil, length, 2 * ATTN_DIM), F32),
        compiler_params=_params("parallel", "arbitrary"),
        name="attention",
    )(qkv, qkv)


def _attn_dec_body(qkv_ref, kc_ref, tail_ref, new_ref, out_ref, cout_ref):
    del new_ref
    rows = BF16_ROWS
    head_of_lane = lax.broadcasted_iota(jnp.int32, (rows, ATTN_DIM), 1) // HEAD_DIM
    own = head_of_lane == lax.broadcasted_iota(jnp.int32, (rows, ATTN_DIM), 0)
    last = lax.broadcasted_iota(jnp.int32, (HEAD_DIM, LANES), 1) == LANES - 1
    wide = lambda a: jnp.broadcast_to(a, (rows, ATTN_DIM))
    pick = lambda a: jnp.sum(jnp.where(own, a, 0.0), axis=0, keepdims=True)
    for b in range(qkv_ref.shape[0]):
        q, kn, vn = (qkv_ref[b:b + 1, j * ATTN_DIM:(j + 1) * ATTN_DIM] for j in range(3))
        q_rows = jnp.where(own, wide(q), 0.0)
        s = _dot(q_rows.astype(BF16), kc_ref[b, 0]) * ATTN_SCALE
        s_new = jnp.sum(q_rows * wide(kn), axis=-1, keepdims=True) * ATTN_SCALE
        mx = jnp.maximum(jnp.max(s, axis=-1, keepdims=True), s_new)
        p = jnp.exp(s - mx)
        p_new = jnp.exp(s_new - mx)
        den = jnp.sum(p, axis=-1, keepdims=True) + p_new
        o = (_dot_nt(p.astype(BF16), kc_ref[b, 1]) + p_new * wide(vn)) / den
        out_ref[b:b + 1, 0:ATTN_DIM] = pick(o)
        out_ref[b:b + 1, ATTN_DIM:2 * ATTN_DIM] = pick(wide(mx + jnp.log(den)))
        new_rows = jnp.concatenate([kn, vn], axis=1)
        new_cols = jnp.transpose(jnp.broadcast_to(new_rows, (LANES, 2 * ATTN_DIM)))
        for kv in range(2):
            for hh in range(ATTN_HEADS):
                col = new_cols[kv * ATTN_DIM + hh * HEAD_DIM:kv * ATTN_DIM + (hh + 1) * HEAD_DIM, :]
                cout_ref[b, kv, hh] = jnp.where(last, col, pltpu.roll(tail_ref[b, kv, hh], LANES - 1, 1))


def _attention_decode(qkv, compact, cache_t, new_cache, layer):
    b = qkv.shape[0]
    window = cache_t.shape[-1]
    bs = DEC_BLOCK_SAMPLES
    while b % bs:
        bs -= 1
    kc = pl.BlockSpec((bs,) + compact.shape[1:], lambda i: (i, 0, 0, 0))
    tail = pl.BlockSpec((None, bs, 2, ATTN_HEADS, HEAD_DIM, LANES),
                        lambda i: (layer, i, 0, 0, 0, window // LANES - 1))
    out, cnew = pl.pallas_call(
        _attn_dec_body,
        grid=(b // bs,),
        in_specs=[_row_spec(bs, qkv.shape[1]), kc, tail, pl.BlockSpec(memory_space=pl.ANY)],
        out_specs=[_row_spec(bs, 2 * ATTN_DIM), tail],
        out_shape=[jax.ShapeDtypeStruct((b, 2 * ATTN_DIM), F32), jax.ShapeDtypeStruct(cache_t.shape, F32)],
        input_output_aliases={3: 1},
        compiler_params=_params("parallel"),
        name="attention_decode",
    )(qkv, compact, cache_t, new_cache)
    return out[None], cnew


def _from_streams(in_ref, lane0, scr_ref, dil):
    if dil == 1:
        return in_ref[0, :, lane0:lane0 + ATTN_DIM]
    n = in_ref.shape[1]
    for r in range(dil):
        for s in range(2):
            scr_ref[s, pl.ds(r, n, stride=dil), :] = in_ref[r, :, lane0 + s * LANES:lane0 + (s + 1) * LANES]
    return jnp.concatenate([scr_ref[0], scr_ref[1]], axis=1)


def _merge_body(x_ref, cy_ref, ry_ref, a0_ref, a1_ref, a2_ref, g_ref,
                wg_ref, wc_ref, wr_ref, wa_ref, wo_ref, *rest, dils, layout):
    n_side_in = sum(n_in for n_in, _, _ in layout)
    n_side_out = sum(n_out for _, n_out, _ in layout)
    out_ref = rest[n_side_in]
    scratch = rest[n_side_in + 1 + n_side_out:]
    _run_side(layout, rest[:n_side_in], rest[n_side_in + 1:n_side_in + 1 + n_side_out])
    x = x_ref[...]
    h = _rms(x, g_ref[2:3, :]).astype(BF16)
    scr = scratch[0] if scratch else None
    d = x.shape[1]
    tm = x.shape[0]
    halves = (slice(0, tm // 2), slice(tm // 2, tm)) if tm % (2 * MXU_WIDTH) == 0 else (slice(0, tm),)
    gate = jnp.concatenate([_dot(h[r], wg_ref[:, 0:d]) for r in halves], axis=0)
    merged = jax.nn.sigmoid(gate) * _dot(cy_ref[...].astype(BF16), wc_ref[...])
    merged = merged + jax.nn.sigmoid(_dot(h, wg_ref[:, d:2 * d])) * _dot(ry_ref[...].astype(BF16), wr_ref[...])
    attn_gate = jax.nn.sigmoid(_dot(h, wg_ref[:, 2 * d:3 * d]))
    vals = []
    for j in range(6):
        ref = (a0_ref, a1_ref, a2_ref)[j // 2]
        vals.append(_from_streams(ref, (j % 2) * ATTN_DIM, None if scr is None else scr.at[j], dils[j // 2]))
    o0, l0, o1, l1, o2, l2 = vals
    mx = jnp.maximum(jnp.maximum(l0, l1), l2)
    e0, e1, e2 = jnp.exp(l0 - mx), jnp.exp(l1 - mx), jnp.exp(l2 - mx)
    den = e0 + e1 + e2
    attn_y = ((e0 / den) * o0 + (e1 / den) * o1 + (e2 / den) * o2).astype(BF16)
    merged = (merged + attn_gate * _dot(attn_y, wa_ref[...])).astype(BF16)
    for r in halves:
        out_ref[r, :] = x[r] + _rms(_dot(merged[r], wo_ref[...]), g_ref[3:4, :])


def _merge(x, cy, ry, attn, dils, gains, wg, wc, wr, wa, wo, layer, tm, side=()):
    m, d = x.shape
    a_specs = [pl.BlockSpec((dl, tm // dl, 2 * ATTN_DIM), lambda i: (0, i, 0)) for dl in dils]
    scratch = [pltpu.VMEM((6, 2, tm, LANES), F32)] if max(dils) > 1 else []
    args = [x, cy, ry, *attn, gains, wg, wc, wr, wa, wo]
    s_args, s_in, s_shape, s_out, aliases, layout = _side_operands(len(args), 1, side)
    outs = pl.pallas_call(
        functools.partial(_merge_body, dils=dils, layout=layout),
        grid=(m // tm,),
        in_specs=[_row_spec(tm, d), _row_spec(tm, CONV_DIM), _row_spec(tm, 512)] + a_specs
        + [_layer_spec(gains.shape, layer)] + [_const_spec(a.shape) for a in (wg, wc, wr, wa, wo)] + s_in,
        out_specs=[_row_spec(tm, d)] + s_out,
        out_shape=[jax.ShapeDtypeStruct((m, d), F32)] + s_shape,
        scratch_shapes=scratch,
        input_output_aliases=aliases,
        compiler_params=_params("arbitrary"),
        name="merge",
    )(*args, *s_args)
    return outs[0], outs[1:]


def _cos_sin(start, count, freq, split):
    if split is None or count % split:
        ang = (start + jnp.arange(count)).astype(F32)[:, None] * freq[None, :]
        return jnp.cos(ang), jnp.sin(ang)
    base = (start + split * jnp.arange(count // split)).astype(F32)[:, None, None] * freq
    off = jnp.arange(split).astype(F32)[None, :, None] * freq
    cb, sb, co, so = jnp.cos(base), jnp.sin(base), jnp.cos(off), jnp.sin(off)
    return (cb * co - sb * so).reshape(count, LANES), (sb * co + cb * so).reshape(count, LANES)


def _rotary_tables(start, count, split=None):
    lane = jnp.arange(LANES)
    half = RET_DK // 2
    freq = jnp.exp(-(lane % half).astype(F32) * (math.log(RET_THETA) / half))
    cos, sin = _cos_sin(start, count, freq, split)
    rc, rs = cos, jnp.where(lane < half, -sin, sin)
    dim = lane % HEAD_DIM
    half = ROT_DIM // 2
    freq = jnp.exp(-(dim % half).astype(F32) * (math.log(ROPE_THETA) / half))
    cos, sin = _cos_sin(start, count, freq, split)
    ac = jnp.where(dim < ROT_DIM, cos, 1.0)
    alo = jnp.where(dim < half, -sin, 0.0)
    ahi = jnp.where(jnp.logical_and(dim >= half, dim < ROT_DIM), sin, 0.0)
    return rc, rs, ac, alo, ahi


def _finish(x, ple, cy, ry, attn, dils, w, layer, tm, merge_side=(), ffn_side=()):
    x, merge_out = _merge(x, cy, ry, attn, dils, w['gains'], w['w_gate'], w['w_conv_out'], w['w_ret_out'],
                          w['w_attn_out'], w['w_o'], layer, tm, side=merge_side)
    x, ffn_out = _ffn(x, w['gains'], w['ffn2_gu'], w['ffn2_down'], layer, 4, 5, tm,
                      ple=(ple, w['w_ple_gate'], w['w_ple_proj']), side=ffn_side)
    return x, merge_out, ffn_out


W_FFN1 = ('ffn1_gu', 'ffn1_down')
W_EARLY = W_FFN1 + ('w_in',)
W_LATE = ('w_gate', 'w_conv_out', 'w_ret_out', 'w_attn_out', 'w_o', 'ffn2_gu', 'ffn2_down', 'w_ple_gate',
          'w_ple_proj')


def _prompt_layer(x, ple, tabs, caches_t, new_kv, raw, w, layer):
    depth = raw['w_in'].shape[0]
    s = x.shape[0]
    tm = min(ROW_TILE, s)
    steps = s // tm
    wide_dil = DILATIONS[2]
    todo = tuple(n for n in W_EARLY[2:] + W_LATE if n not in w)
    side = [_shift_job(caches_t[2], new_kv[2], None, layer, steps, 0, wide_dil)]
    if todo:
        side.append(_convert_job([raw[n] for n in todo], layer, steps))
    x, outs = _ffn(x, w['gains'], w['ffn1_gu'], w['ffn1_down'], layer, 0, 1, tm, side=side)
    wide, wide_c = outs[:2]
    w = dict(w, **dict(zip(todo, outs[2:])))
    hist = jnp.zeros((8, CONV_DIM), F32)
    outs = _proj(x, w['gains'], w['w_in'], w['w_conv'], w['b_conv'], tabs, hist, layer, tm, decode=False)
    cy, ulast, ry, ret_state = outs[:4]
    attn, kv = [], []
    for gi, (window, dil) in enumerate(ATTN_GROUPS):
        qkv = outs[4 + gi]
        attn.append(_attention(qkv))
        keep = min(window, s) // dil
        tail = jnp.swapaxes(qkv[:, s // dil - keep:, ATTN_DIM:], 0, 1)
        kv.append(tail.reshape(1, keep * dil, 2, ATTN_HEADS, HEAD_DIM))
    merge_side = [_shift_job(caches_t[gi], new_kv[gi], None, layer, steps, None, DILATIONS[gi]) for gi in (0, 1)]
    ffn_side = [_shift_job(caches_t[2], wide, wide_c, layer, steps, 1, wide_dil)]
    if layer + 1 < depth:
        merge_side.append(_convert_job([raw[n] for n in W_EARLY], layer + 1, steps))
        ffn_side.append(_convert_job([raw[n] for n in W_LATE], layer + 1, steps))
    x, m_out, f_out = _finish(x, ple, cy, ry, attn, DILATIONS, w, layer, tm, merge_side, ffn_side)
    w_next = dict(zip(W_EARLY, m_out[4:]), **dict(zip(W_LATE, f_out[2:])))
    new_kv = [m_out[0], m_out[2], f_out[0]]
    compact = [m_out[1], m_out[3], f_out[1]]
    return x, ulast[6:8][None], ret_state[None], kv, new_kv, compact, w, w_next


def _sample_layer(x, ple, tabs, gdec, conv_hist, ret_state, caches_t, prev_ret, new_kv, compact, w, layer):
    b = x.shape[0]
    x, _ = _ffn(x, w['gains'], w['ffn1_gu'], w['ffn1_down'], layer, 0, 1, b)
    outs = _proj(x, w['gains'], w['w_in'], w['w_conv'], w['b_conv'], tabs, conv_hist, layer, b, decode=True)
    cy, ulast, ret_in = outs[:3]
    ry, ret_new = _retention_decode(ret_in, gdec, ret_state, layer, prev_ret)
    attn, kv = [], []
    for gi in range(len(ATTN_GROUPS)):
        o_lse, cnew = _attention_decode(outs[3 + gi], compact[gi], caches_t[gi], new_kv[gi], layer)
        attn.append(o_lse)
        kv.append(cnew)
    x, _, _ = _finish(x, ple, cy, ry, attn, (1, 1, 1), w, layer, b)
    return x, jnp.swapaxes(ulast, 0, 1), ret_new, kv


def kernel(x_prompt, x_sample, state_conv, state_ret, cache_kv_w128, cache_kv_w512, cache_kv_w2048, p_prompt, p_sample, norm_gain, w_ffn1_gu, w_ffn1_down, w_in, w_conv, b_conv, w_conv_out, w_ret_out, w_attn_out, w_gate, w_o, w_ffn2_gu, w_ffn2_down, w_ple_gate, w_ple_proj):
    depth = norm_gain.shape[0]
    seq = x_prompt.shape[1]
    nb = x_sample.shape[0]
    assert x_prompt.shape[0] == 1 and x_sample.shape[1] == 1

    tabs_p = jnp.concatenate(_rotary_tables(0, seq, split=ROT_SPLIT), axis=1)
    tabs_s = jnp.broadcast_to(jnp.concatenate(_rotary_tables(PAST_LEN, 1), axis=1), (nb, 5 * LANES))
    gdec = jnp.broadcast_to(jnp.exp(_ret_log_decay())[:, None, None], (RET_HEADS, 1, RET_DK))
    caches_t = [jnp.transpose(c, (0, 1, 3, 4, 5, 2)) for c in (cache_kv_w128, cache_kv_w512, cache_kv_w2048)]
    raw = dict(ffn1_gu=w_ffn1_gu, ffn1_down=w_ffn1_down, w_in=w_in, w_conv_out=w_conv_out, w_ret_out=w_ret_out,
               w_attn_out=w_attn_out, w_gate=w_gate, w_o=w_o, ffn2_gu=w_ffn2_gu, ffn2_down=w_ffn2_down,
               w_ple_gate=w_ple_gate, w_ple_proj=w_ple_proj)
    shared = dict(gains=norm_gain, w_conv=w_conv, b_conv=b_conv.reshape(depth, 1, CONV_DIM))
    w_next = {n: raw[n][0].astype(BF16) for n in W_FFN1}
    ple_p = p_prompt[:, 0]
    ple_s = p_sample[:, :, 0]
    conv_hist = jnp.swapaxes(state_conv, 1, 2)

    yp, ys = x_prompt[0], x_sample[:, 0]
    conv_p, conv_s, ret_p = [], [], []
    kv_p = [[] for _ in ATTN_GROUPS]
    ret_s, kv_s = None, [None] * len(ATTN_GROUPS)
    for l in range(depth):
        yp, cp, rp, kp, kv_s, compact, w, w_next = _prompt_layer(yp, ple_p, tabs_p, caches_t, kv_s, raw,
                                                                 dict(shared, **w_next), l)
        ys, cs, ret_s, kv_s = _sample_layer(ys, ple_s, tabs_s, gdec, conv_hist, state_ret, caches_t, ret_s, kv_s,
                                            compact, w, l)
        conv_p.append(cp)
        conv_s.append(cs)
        ret_p.append(rp)
        for gi in range(len(ATTN_GROUPS)):
            kv_p[gi].append(kp[gi])
    back = lambda c: jnp.transpose(c, (0, 1, 5, 2, 3, 4))
    return (yp[None], ys[:, None], jnp.stack(conv_p), jnp.stack(conv_s), jnp.stack(ret_p), ret_s,
            jnp.stack(kv_p[0]), back(kv_s[0]), jnp.stack(kv_p[1]), back(kv_s[1]),
            jnp.stack(kv_p[2]), back(kv_s[2]))
```

```python
import functools
import math

import jax
import jax.numpy as jnp
from jax import lax
from jax.experimental import pallas as pl
from jax.experimental.pallas import tpu as pltpu

F32 = jnp.float32
BF16 = jnp.bfloat16

EPS = 1e-6
PAST_LEN = 16384
CONV_DIM = 512
RET_HEADS = 4
RET_DK = 128
RET_THETA = 10000.0
ATTN_GROUPS = ((128, 1), (512, 4), (2048, 16))
DILATIONS = tuple(d for _, d in ATTN_GROUPS)
ATTN_HEADS = 4
HEAD_DIM = 64
ATTN_DIM = ATTN_HEADS * HEAD_DIM
ATTN_SCALE = HEAD_DIM ** -0.5
ROT_DIM = 16
ROPE_THETA = 500000.0
ROT_SPLIT = 128
ATTN_BLOCK = 128
ATTN_STEP = 2048
DEC_BLOCK_WINDOW = 2048
DEC_BLOCK_SAMPLES = 8
NEG = -1e30
LANES = 128
MXU_WIDTH = 256
BF16_ROWS = 16
OFF_CONV = 0
OFF_RET = 3 * CONV_DIM
OFF_ATTN = OFF_RET + 4 * 512

VMEM_LIMIT_BYTES = 60000 * 1024
ROW_TILE = 512


def _params(*sem):
    return pltpu.CompilerParams(dimension_semantics=sem, vmem_limit_bytes=VMEM_LIMIT_BYTES)


def _const_spec(shape):
    zeros = (0,) * len(shape)
    return pl.BlockSpec(shape, lambda *_: zeros, pipeline_mode=pl.Buffered(1))


def _layer_spec(shape, layer):
    tail = tuple(shape[1:])
    idx = (layer,) + (0,) * len(tail)
    return pl.BlockSpec((None,) + tail, lambda *_: idx, pipeline_mode=pl.Buffered(1))


def _row_spec(tm, width):
    return pl.BlockSpec((tm, width), lambda i: (i, 0))


def _rms(x, g):
    return x * lax.rsqrt(jnp.mean(x * x, axis=-1, keepdims=True) + EPS) * g


def _dot(a, b):
    return jnp.dot(a, b, preferred_element_type=F32)


def _dot_nt(a, b):
    return lax.dot_general(a, b, (((1,), (1,)), ((), ())), preferred_element_type=F32)


def _silu(x):
    return x * jax.nn.sigmoid(x)


def _ff_chunks(d_ff):
    first = -(-(d_ff // 2) // MXU_WIDTH) * MXU_WIDTH
    return ((0, first), (first, d_ff - first)) if 0 < first < d_ff else ((0, d_ff),)


def _side_operands(n_args, n_outs, side):
    args, in_specs, out_shape, out_specs, aliases, layout = [], [], [], [], {}, []
    for job in side:
        for src, dst in job['aliases'].items():
            aliases[n_args + len(args) + src] = n_outs + len(out_shape) + dst
        layout.append((len(job['args']), len(job['out_shape']), job['fn']))
        args += job['args']
        in_specs += job['in_specs']
        out_shape += job['out_shape']
        out_specs += job['out_specs']
    return args, in_specs, out_shape, out_specs, aliases, tuple(layout)


def _run_side(layout, in_refs, out_refs):
    i = o = 0
    for n_in, n_out, fn in layout:
        fn(in_refs[i:i + n_in], out_refs[o:o + n_out])
        i += n_in
        o += n_out


def _shift_job(cache_t, new_cache, compact, layer, n_steps, kv, dil):
    _, b, _, nh, hd, window = cache_t.shape
    n_pos = window // dil
    bs = -(-b // n_steps)
    assert b % bs == 0
    n_blk = b // bs
    n_kv, kv_idx = (2, 0) if kv is None else (1, kv)
    blk = lambda i: jnp.minimum(i, n_blk - 1)
    spec = pl.BlockSpec((None, bs, n_kv, nh, hd, window), lambda i: (layer, blk(i), kv_idx, 0, 0, 0))
    cspec = pl.BlockSpec((bs, n_kv, nh * hd, n_pos), lambda i: (blk(i), kv_idx, 0, 0))
    pos = jnp.arange(window)[:, None] == dil * jnp.arange(n_pos)[None, :]
    pick = pos.astype(BF16)

    def fn(in_refs, out_refs):
        src, pick_ref = in_refs[:2]
        dst, cdst = out_refs
        for s in range(bs):
            for j in range(n_kv):
                for hh in range(nh):
                    t = src[s, j, hh]
                    dst[s, j, hh] = pltpu.roll(t, window - 1, 1)
                    tb = t.astype(BF16)
                    if dil > 1:
                        tb = _dot(tb, pick_ref[...]).astype(BF16)
                    cdst[s, j, hh * hd:(hh + 1) * hd, :] = tb

    job = dict(args=[cache_t, pick], in_specs=[spec, _const_spec(pick.shape)],
               out_shape=[jax.ShapeDtypeStruct(cache_t.shape, cache_t.dtype),
                          jax.ShapeDtypeStruct((b, 2, nh * hd, n_pos), BF16)],
               out_specs=[spec, cspec], aliases={}, fn=fn)
    for out_idx, buf in enumerate((new_cache, compact)):
        if buf is not None:
            job['aliases'][len(job['args'])] = out_idx
            job['args'].append(buf)
            job['in_specs'].append(pl.BlockSpec(memory_space=pl.ANY))
    return job


def _convert_job(weights, layer, n_steps):
    job = dict(args=[], in_specs=[], out_shape=[], out_specs=[], aliases={})
    for wt in weights:
        _, k, n = wt.shape
        n_band = max(c for c in range(1, n_steps + 1)
                     if n_steps % c == 0 and k % c == 0 and ((k // c) % BF16_ROWS == 0 or c == 1))
        band = lambda i, n_band=n_band: i * n_band // n_steps
        job['args'].append(wt)
        job['in_specs'].append(pl.BlockSpec((None, k // n_band, n), lambda i, band=band: (layer, band(i), 0)))
        job['out_shape'].append(jax.ShapeDtypeStruct((k, n), BF16))
        job['out_specs'].append(pl.BlockSpec((k // n_band, n), lambda i, band=band: (band(i), 0)))

    def fn(in_refs, out_refs):
        for src, dst in zip(in_refs, out_refs):
            dst[...] = src[...].astype(BF16)

    job['fn'] = fn
    return job


def _ffn_body(x_ref, g_ref, wgu_ref, wd_ref, *rest, pre, post, d_ff, with_ple, layout):
    n_ple = 3 if with_ple else 0
    n_side_in = sum(n_in for n_in, _, _ in layout)
    o_ref = rest[n_ple + n_side_in]
    _run_side(layout, rest[n_ple:n_ple + n_side_in], rest[n_ple + n_side_in + 1:])
    x = x_ref[...]
    h = _rms(x, g_ref[pre:pre + 1, :]).astype(BF16)
    tm = x.shape[0]
    halves = (slice(0, tm // 2), slice(tm // 2, tm)) if tm % (2 * MXU_WIDTH) == 0 else (slice(0, tm),)
    chunks = _ff_chunks(d_ff)
    acc = None
    for ci, (start, size) in enumerate(chunks):
        wg = wgu_ref[:, start:start + size]
        if ci == 0:
            gate = jnp.concatenate([_dot(h[r], wg) for r in halves], axis=0)
        else:
            gate = _dot(h, wg)
        up = _dot(h, wgu_ref[:, d_ff + start:d_ff + start + size])
        a = (_silu(gate) * up).astype(BF16)
        if ci < len(chunks) - 1:
            y = _dot(a, wd_ref[start:start + size, :])
            acc = y if acc is None else acc + y
    out = []
    for r in halves:
        y = _dot(a[r], wd_ref[start:start + size, :])
        y = y if acc is None else acc[r] + y
        out.append(x[r] + 0.5 * _rms(y, g_ref[post:post + 1, :]))
    if with_ple:
        p_ref, wg_ref, wp_ref = rest[:3]
        proj = _dot(p_ref[...].astype(BF16), wp_ref[...])
        for i, r in enumerate(halves):
            h = _rms(out[i], g_ref[6:7, :]).astype(BF16)
            gate = jax.nn.sigmoid(_dot(h, wg_ref[...]))
            out[i] = out[i] + _rms(gate * proj[r], g_ref[7:8, :])
    for r, v in zip(halves, out):
        o_ref[r, :] = v


def _ffn(x, gains, wgu, wd, layer, pre, post, tm, ple=None, side=()):
    m, d = x.shape
    d_ff = wd.shape[0]
    args = [x, gains, wgu, wd]
    in_specs = [_row_spec(tm, d), _layer_spec(gains.shape, layer), _const_spec(wgu.shape), _const_spec(wd.shape)]
    if ple is not None:
        p, wg, wp = ple
        args += [p, wg, wp]
        in_specs += [pl.BlockSpec((None, tm, p.shape[2]), lambda i: (layer, i, 0)),
                     _const_spec(wg.shape), _const_spec(wp.shape)]
    s_args, s_in, s_shape, s_out, aliases, layout = _side_operands(len(args), 1, side)
    body = functools.partial(_ffn_body, pre=pre, post=post, d_ff=d_ff, with_ple=ple is not None, layout=layout)
    outs = pl.pallas_call(
        body,
        grid=(m // tm,),
        in_specs=in_specs + s_in,
        out_specs=[_row_spec(tm, d)] + s_out,
        out_shape=[jax.ShapeDtypeStruct((m, d), F32)] + s_shape,
        input_output_aliases=aliases,
        compiler_params=_params("arbitrary"),
        name="ffn",
    )(*args, *s_args)
    return outs[0], outs[1:]


def _rot_ret(x, cos, sin):
    return x * cos + pltpu.roll(x, RET_DK // 2, 1) * sin


def _rot_attn(x, cos, sin_lo, sin_hi):
    return x * cos + pltpu.roll(x, LANES - ROT_DIM // 2, 1) * sin_lo + pltpu.roll(x, ROT_DIM // 2, 1) * sin_hi


def _to_streams(halves, out_ref, lane0, scr_ref, dil):
    if dil == 1:
        for s, v in enumerate(halves):
            out_ref[0, :, lane0 + s * LANES:lane0 + (s + 1) * LANES] = v
        return
    n = halves[0].shape[0] // dil
    for s, v in enumerate(halves):
        scr_ref[s] = v
    for r in range(dil):
        for s in range(2):
            out_ref[r, :, lane0 + s * LANES:lane0 + (s + 1) * LANES] = scr_ref[s, pl.ds(r, n, stride=dil), :]


def _proj_body(*refs, dils):
    x_ref, g_ref, w_ref, wc_ref, bc_ref, tab_ref, hist_ref = refs[:7]
    decode = dils is None
    if decode:
        cy_ref, ulast_ref, ret_ref = refs[7:10]
        attn_refs = refs[10:13]
    else:
        decay_ref, qw_ref, kw_ref, gch_ref = refs[7:11]
        cy_ref, ulast_ref, ret_ref, sout_ref = refs[11:15]
        attn_refs = refs[15:18]
        carry_ref, scr_ref, s_scr = refs[18:21]
    tab = lambda j: tab_ref[:, j * LANES:(j + 1) * LANES]

    x = x_ref[...]
    h = _rms(x, g_ref[2:3, :]).astype(BF16)

    def part(off, width):
        return _dot(h, w_ref[:, off:off + width])

    def mixer_c(gi):
        acos, alo, ahi = tab(2), tab(3), tab(4)
        base = OFF_ATTN + gi * 3 * ATTN_DIM
        zq = part(base, ATTN_DIM)
        zk = part(base + ATTN_DIM, ATTN_DIM)
        zv = part(base + 2 * ATTN_DIM, ATTN_DIM)
        halves = lambda z: [z[:, s * LANES:(s + 1) * LANES] for s in range(2)]
        q_h = [_rot_attn(v, acos, alo, ahi) for v in halves(zq)]
        k_h = [_rot_attn(v, acos, alo, ahi) for v in halves(zk)]
        out_ref = attn_refs[gi]
        for j, vals in enumerate((q_h, k_h, halves(zv))):
            if decode:
                for s, v in enumerate(vals):
                    out_ref[:, j * ATTN_DIM + s * LANES:j * ATTN_DIM + (s + 1) * LANES] = v
            else:
                _to_streams(vals, out_ref, j * ATTN_DIM, scr_ref.at[3 * gi + j], dils[gi])

    def mixer_a():
        rows = h.shape[0]
        halves = (slice(0, rows // 2), slice(rows // 2, rows)) if rows % (2 * MXU_WIDTH) == 0 else (slice(0, rows),)
        split = lambda off: jnp.concatenate([_dot(h[r], w_ref[:, off:off + CONV_DIM]) for r in halves], axis=0)
        b_gate = split(OFF_CONV)
        u = split(OFF_CONV + CONV_DIM) * split(OFF_CONV + 2 * CONV_DIM)
        if decode:
            u2 = hist_ref[0]
            u1 = hist_ref[1]
            ulast_ref[0] = u1
            ulast_ref[1] = u
        else:
            tm = u.shape[0]

            @pl.when(pl.program_id(0) == 0)
            def _():
                carry_ref[...] = hist_ref[...]

            carry = carry_ref[...]
            row = lax.broadcasted_iota(jnp.int32, u.shape, 0)
            u1 = jnp.where(row == 0, carry[7:8, :], pltpu.roll(u, 1, 0))
            u2 = jnp.where(row == 0, carry[6:7, :], jnp.where(row == 1, carry[7:8, :], pltpu.roll(u, 2, 0)))
            carry_ref[...] = u[tm - 8:tm, :]
            ulast_ref[...] = u[tm - 8:tm, :]
        y = bc_ref[...] + wc_ref[0:1, :] * u2
        y = y + wc_ref[1:2, :] * u1
        y = y + wc_ref[2:3, :] * u
        cy_ref[...] = (b_gate * y).astype(cy_ref.dtype)

    def mixer_b_inputs():
        tm = h.shape[0]
        halves = (slice(0, tm // 2), slice(tm // 2, tm)) if tm % (2 * MXU_WIDTH) == 0 else (slice(0, tm),)
        zq = jnp.concatenate([_dot(h[r], w_ref[:, OFF_RET:OFF_RET + 512]) for r in halves], axis=0)
        return [zq] + [part(OFF_RET + j * 512, 512) for j in range(1, 4)]

    def mixer_b(z, heads):
        zq, zk, zv, zg = z
        cos, sin = tab(0), tab(1)
        for hh in heads:
            cols = slice(hh * RET_DK, (hh + 1) * RET_DK)
            q = _rot_ret(zq[:, cols], cos, sin)
            k = _rot_ret(zk[:, cols], cos, sin) * (RET_DK ** -0.5)
            if decode:
                ret_ref[:, cols] = q
                ret_ref[:, 512 + hh * RET_DK:512 + (hh + 1) * RET_DK] = k
                ret_ref[:, 1024 + hh * RET_DK:1024 + (hh + 1) * RET_DK] = zv[:, cols]
                ret_ref[:, 1536 + hh * RET_DK:1536 + (hh + 1) * RET_DK] = zg[:, cols]
                continue
            v = zv[:, cols].astype(BF16)
            s = s_scr[hh]
            scores = _dot_nt(q.astype(BF16), k.astype(BF16)) * decay_ref[hh]
            o = _dot(scores.astype(BF16), v) + _dot((q * qw_ref[hh]).astype(BF16), s.astype(BF16))
            kv = _dot(jnp.transpose(k * kw_ref[hh]).astype(BF16), v)
            s_scr[hh] = s * gch_ref[hh] + kv
            ret_ref[:, cols] = (_head_norm(o) * _silu(zg[:, cols])).astype(ret_ref.dtype)

    if not decode:
        @pl.when(pl.program_id(0) == 0)
        def _():
            s_scr[...] = jnp.zeros_like(s_scr)

    z = mixer_b_inputs()
    mixer_c(2)
    mixer_b(z, (0, 1))
    mixer_c(1)
    mixer_b(z, (2, 3))
    mixer_c(0)
    mixer_a()

    if not decode:
        @pl.when(pl.program_id(0) == pl.num_programs(0) - 1)
        def _():
            sout_ref[...] = s_scr[...]


def _proj(x, gains, w_in, w_conv, b_conv, tabs, hist, layer, tm, decode):
    m, d = x.shape
    n = m // tm
    qkv = 3 * ATTN_DIM
    args = [x, gains, w_in, w_conv, b_conv, tabs, hist]
    state = (RET_HEADS, RET_DK, RET_DK)
    if decode:
        dils = None
        hist_spec = pl.BlockSpec((None, 2, tm, CONV_DIM), lambda i: (layer, 0, i, 0))
        ulast_spec = pl.BlockSpec((2, tm, CONV_DIM), lambda i: (0, i, 0))
        ulast_shape = jax.ShapeDtypeStruct((2, m, CONV_DIM), F32)
        ret_specs = [_row_spec(tm, 4 * 512)]
        ret_shapes = [jax.ShapeDtypeStruct((m, 4 * 512), F32)]
        attn_specs = [_row_spec(tm, qkv)] * 3
        attn_shapes = [jax.ShapeDtypeStruct((m, qkv), F32)] * 3
        scratch = []
    else:
        dils = DILATIONS
        hist_spec = _const_spec((8, CONV_DIM))
        ulast_spec = pl.BlockSpec((8, CONV_DIM), lambda i: (0, 0))
        ulast_shape = jax.ShapeDtypeStruct((8, CONV_DIM), F32)
        ret_specs = [_row_spec(tm, 512), pl.BlockSpec(state, lambda i: (0, 0, 0))]
        ret_shapes = [jax.ShapeDtypeStruct((m, 512), BF16), jax.ShapeDtypeStruct(state, F32)]
        attn_specs = [pl.BlockSpec((dl, tm // dl, qkv), lambda i: (0, i, 0)) for dl in dils]
        attn_shapes = [jax.ShapeDtypeStruct((dl, m // dl, qkv), F32) for dl in dils]
        scratch = [pltpu.VMEM((8, CONV_DIM), F32), pltpu.VMEM((9, 2, tm, LANES), F32), pltpu.VMEM(state, F32)]
        args += list(_ret_tables(tm))
    in_specs = [
        _row_spec(tm, d), _layer_spec(gains.shape, layer), _const_spec(w_in.shape),
        _layer_spec(w_conv.shape, layer), _layer_spec(b_conv.shape, layer), _row_spec(tm, tabs.shape[1]), hist_spec,
    ] + [_const_spec(a.shape) for a in args[7:]]
    out_specs = [_row_spec(tm, CONV_DIM), ulast_spec] + ret_specs + attn_specs
    out_shape = [jax.ShapeDtypeStruct((m, CONV_DIM), BF16), ulast_shape] + ret_shapes + attn_shapes
    return pl.pallas_call(
        functools.partial(_proj_body, dils=dils),
        grid=(n,),
        in_specs=in_specs,
        out_specs=out_specs,
        out_shape=out_shape,
        scratch_shapes=scratch,
        compiler_params=_params("arbitrary"),
        name="proj",
    )(*args)


def _head_norm(o):
    mu = jnp.mean(o, axis=-1, keepdims=True)
    var = jnp.mean(jnp.square(o - mu), axis=-1, keepdims=True)
    return (o - mu) * lax.rsqrt(var + EPS)


def _ret_log_decay():
    return jnp.log1p(-jnp.exp2(-5.0 - jnp.arange(RET_HEADS, dtype=F32)))


def _ret_tables(chunk):
    i = jnp.arange(chunk, dtype=F32)
    log_g = _ret_log_decay()
    diff = i[:, None] - i[None, :]
    decay = jnp.where(diff[None] >= 0, jnp.exp(jnp.maximum(diff, 0.0)[None] * log_g[:, None, None]), 0.0)
    k_w = jnp.exp((chunk - 1 - i)[:, None] * log_g[None, :])
    q_w = jnp.exp((i + 1)[:, None] * log_g[None, :])
    g_chunk = jnp.exp(chunk * log_g)
    qw = jnp.broadcast_to(q_w.T[:, :, None], (RET_HEADS, chunk, RET_DK))
    kw = jnp.broadcast_to(k_w.T[:, :, None], (RET_HEADS, chunk, RET_DK))
    gch = jnp.broadcast_to(g_chunk[:, None, None], (RET_HEADS, RET_DK, RET_DK))
    return decay, qw, kw, gch


def _ret_dec_body(r_ref, gd_ref, s0_ref, *rest):
    y_ref, sout_ref = rest[-2:]
    for b in range(r_ref.shape[0]):
        for hh in range(RET_HEADS):
            part = lambda j: r_ref[b:b + 1, j * 512 + hh * RET_DK:j * 512 + (hh + 1) * RET_DK]
            q, k, v = part(0), part(1), part(2)
            gd = gd_ref[hh]
            s0 = s0_ref[b, hh]
            wide = (RET_DK, RET_DK)
            q_col = jnp.transpose(jnp.broadcast_to(q * gd, wide))
            k_col = jnp.transpose(jnp.broadcast_to(k, wide))
            o_inter = jnp.sum(q_col * s0, axis=0, keepdims=True)
            o_intra = jnp.sum(q * k, axis=-1, keepdims=True) * v
            sout_ref[b, hh] = s0 * gd + k_col * v
            o = o_intra + o_inter
            y_ref[b:b + 1, hh * RET_DK:(hh + 1) * RET_DK] = _head_norm(o) * _silu(part(3))


def _retention_decode(r, gdec, state, layer, prev_out):
    b = r.shape[0]
    bs = DEC_BLOCK_SAMPLES
    while b % bs:
        bs -= 1
    st = pl.BlockSpec((None, bs, RET_HEADS, RET_DK, RET_DK), lambda i: (layer, i, 0, 0, 0))
    args = [r, gdec, state]
    in_specs = [_row_spec(bs, r.shape[1]), _const_spec(gdec.shape), st]
    aliases = {}
    if prev_out is not None:
        args.append(prev_out)
        in_specs.append(pl.BlockSpec(memory_space=pl.ANY))
        aliases = {3: 1}
    return pl.pallas_call(
        _ret_dec_body,
        grid=(b // bs,),
        in_specs=in_specs,
        out_specs=[_row_spec(bs, 512), st],
        out_shape=[jax.ShapeDtypeStruct((b, 512), F32), jax.ShapeDtypeStruct(state.shape, F32)],
        input_output_aliases=aliases,
        compiler_params=_params("parallel"),
        name="retention_decode",
    )(*args)


def _attn_body(cur_ref, prev_ref, out_ref, *, n_sub):
    n = pl.program_id(1)
    qb = ATTN_BLOCK
    ii = lax.broadcasted_iota(jnp.int32, (2 * qb, 2 * qb), 0) & (qb - 1)
    jj = lax.broadcasted_iota(jnp.int32, (2 * qb, 2 * qb), 1)
    band = jnp.logical_and(jj >= ii, jj <= ii + qb)
    first = jnp.logical_and(band, jnp.logical_or(jj >= qb, n > 0))
    lo = lax.broadcasted_iota(jnp.int32, (qb, LANES), 1) < HEAD_DIM
    for hp in range(ATTN_HEADS // 2):
        cols = lambda j: slice(j * ATTN_DIM + hp * LANES, j * ATTN_DIM + (hp + 1) * LANES)
        kk = jnp.concatenate([prev_ref[:, cols(1)], cur_ref[:, cols(1)]], axis=0).astype(BF16)
        vv = jnp.concatenate([prev_ref[:, cols(2)], cur_ref[:, cols(2)]], axis=0).astype(BF16)
        for b in range(n_sub):
            rows = slice(b * qb, (b + 1) * qb)
            q2 = cur_ref[rows, cols(0)] * ATTN_SCALE
            zero = jnp.zeros_like(q2)
            qs = jnp.concatenate([jnp.where(lo, q2, zero), jnp.where(lo, zero, q2)], axis=0).astype(BF16)
            s = jnp.where(first if b == 0 else band, _dot_nt(qs, kk[b * qb:(b + 2) * qb]), NEG)
            mx = jnp.max(s, axis=-1, keepdims=True)
            p = jnp.exp(s - mx)
            den = jnp.sum(p, axis=-1, keepdims=True)
            o = _dot(p.astype(BF16), vv[b * qb:(b + 2) * qb]) / den
            lse = mx + jnp.log(den)
            out_ref[rows, cols(0)] = jnp.where(lo, o[:qb], o[qb:])
            out_ref[rows, cols(1)] = jnp.where(lo, lse[:qb], lse[qb:])


def _attention(qkv):
    dil, length, width = qkv.shape
    step = min(ATTN_STEP, length)
    n_sub = step // ATTN_BLOCK
    cur = pl.BlockSpec((None, step, width), lambda r, n: (r, n, 0))
    prev = pl.BlockSpec((None, ATTN_BLOCK, width), lambda r, n: (r, jnp.maximum(n * n_sub - 1, 0), 0))
    return pl.pallas_call(
        functools.partial(_attn_body, n_sub=n_sub),
        grid=(dil, length // step),
        in_specs=[cur, prev],
        out_specs=pl.BlockSpec((None, step, 2 * ATTN_DIM), lambda r, n: (r, n, 0)),
        out_shape=jax.ShapeDtypeStruct((dil, length, 2 * ATTN_DIM), F32),
        compiler_params=_params("parallel", "arbitrary"),
        name="attention",
    )(qkv, qkv)


def _attn_dec_body(qkv_ref, kc_ref, tail_ref, new_ref, out_ref, cout_ref):
    del new_ref
    rows = BF16_ROWS
    head_of_lane = lax.broadcasted_iota(jnp.int32, (rows, ATTN_DIM), 1) // HEAD_DIM
    own = head_of_lane == lax.broadcasted_iota(jnp.int32, (rows, ATTN_DIM), 0)
    last = lax.broadcasted_iota(jnp.int32, (HEAD_DIM, LANES), 1) == LANES - 1
    wide = lambda a: jnp.broadcast_to(a, (rows, ATTN_DIM))
    pick = lambda a: jnp.sum(jnp.where(own, a, 0.0), axis=0, keepdims=True)
    for b in range(qkv_ref.shape[0]):
        q, kn, vn = (qkv_ref[b:b + 1, j * ATTN_DIM:(j + 1) * ATTN_DIM] for j in range(3))
        q_rows = jnp.where(own, wide(q), 0.0)
        s = _dot(q_rows.astype(BF16), kc_ref[b, 0]) * ATTN_SCALE
        s_new = jnp.sum(q_rows * wide(kn), axis=-1, keepdims=True) * ATTN_SCALE
        mx = jnp.maximum(jnp.max(s, axis=-1, keepdims=True), s_new)
        p = jnp.exp(s - mx)
        p_new = jnp.exp(s_new - mx)
        den = jnp.sum(p, axis=-1, keepdims=True) + p_new
        o = (_dot_nt(p.astype(BF16), kc_ref[b, 1]) + p_new * wide(vn)) / den
        out_ref[b:b + 1, 0:ATTN_DIM] = pick(o)
        out_ref[b:b + 1, ATTN_DIM:2 * ATTN_DIM] = pick(wide(mx + jnp.log(den)))
        new_rows = jnp.concatenate([kn, vn], axis=1)
        new_cols = jnp.transpose(jnp.broadcast_to(new_rows, (LANES, 2 * ATTN_DIM)))
        for kv in range(2):
            for hh in range(ATTN_HEADS):
                col = new_cols[kv * ATTN_DIM + hh * HEAD_DIM:kv * ATTN_DIM + (hh + 1) * HEAD_DIM, :]
                cout_ref[b, kv, hh] = jnp.where(last, col, pltpu.roll(tail_ref[b, kv, hh], LANES - 1, 1))


def _attention_decode(qkv, compact, cache_t, new_cache, layer):
    b = qkv.shape[0]
    window = cache_t.shape[-1]
    bs = DEC_BLOCK_SAMPLES
    while b % bs:
        bs -= 1
    kc = pl.BlockSpec((bs,) + compact.shape[1:], lambda i: (i, 0, 0, 0))
    tail = pl.BlockSpec((None, bs, 2, ATTN_HEADS, HEAD_DIM, LANES),
                        lambda i: (layer, i, 0, 0, 0, window // LANES - 1))
    out, cnew = pl.pallas_call(
        _attn_dec_body,
        grid=(b // bs,),
        in_specs=[_row_spec(bs, qkv.shape[1]), kc, tail, pl.BlockSpec(memory_space=pl.ANY)],
        out_specs=[_row_spec(bs, 2 * ATTN_DIM), tail],
        out_shape=[jax.ShapeDtypeStruct((b, 2 * ATTN_DIM), F32), jax.ShapeDtypeStruct(cache_t.shape, F32)],
        input_output_aliases={3: 1},
        compiler_params=_params("parallel"),
        name="attention_decode",
    )(qkv, compact, cache_t, new_cache)
    return out[None], cnew


def _from_streams(in_ref, lane0, scr_ref, dil):
    if dil == 1:
        return in_ref[0, :, lane0:lane0 + ATTN_DIM]
    n = in_ref.shape[1]
    for r in range(dil):
        for s in range(2):
            scr_ref[s, pl.ds(r, n, stride=dil), :] = in_ref[r, :, lane0 + s * LANES:lane0 + (s + 1) * LANES]
    return jnp.concatenate([scr_ref[0], scr_ref[1]], axis=1)


def _merge_body(x_ref, cy_ref, ry_ref, a0_ref, a1_ref, a2_ref, g_ref,
                wg_ref, wc_ref, wr_ref, wa_ref, wo_ref, *rest, dils, layout):
    n_side_in = sum(n_in for n_in, _, _ in layout)
    n_side_out = sum(n_out for _, n_out, _ in layout)
    out_ref = rest[n_side_in]
    scratch = rest[n_side_in + 1 + n_side_out:]
    _run_side(layout, rest[:n_side_in], rest[n_side_in + 1:n_side_in + 1 + n_side_out])
    x = x_ref[...]
    h = _rms(x, g_ref[2:3, :]).astype(BF16)
    scr = scratch[0] if scratch else None
    d = x.shape[1]
    tm = x.shape[0]
    halves = (slice(0, tm // 2), slice(tm // 2, tm)) if tm % (2 * MXU_WIDTH) == 0 else (slice(0, tm),)
    gate = jnp.concatenate([_dot(h[r], wg_ref[:, 0:d]) for r in halves], axis=0)
    merged = jax.nn.sigmoid(gate) * _dot(cy_ref[...].astype(BF16), wc_ref[...])
    merged = merged + jax.nn.sigmoid(_dot(h, wg_ref[:, d:2 * d])) * _dot(ry_ref[...].astype(BF16), wr_ref[...])
    attn_gate = jax.nn.sigmoid(_dot(h, wg_ref[:, 2 * d:3 * d]))
    vals = []
    for j in range(6):
        ref = (a0_ref, a1_ref, a2_ref)[j // 2]
        vals.append(_from_streams(ref, (j % 2) * ATTN_DIM, None if scr is None else scr.at[j], dils[j // 2]))
    o0, l0, o1, l1, o2, l2 = vals
    mx = jnp.maximum(jnp.maximum(l0, l1), l2)
    e0, e1, e2 = jnp.exp(l0 - mx), jnp.exp(l1 - mx), jnp.exp(l2 - mx)
    den = e0 + e1 + e2
    attn_y = ((e0 / den) * o0 + (e1 / den) * o1 + (e2 / den) * o2).astype(BF16)
    merged = (merged + attn_gate * _dot(attn_y, wa_ref[...])).astype(BF16)
    for r in halves:
        out_ref[r, :] = x[r] + _rms(_dot(merged[r], wo_ref[...]), g_ref[3:4, :])


def _merge(x, cy, ry, attn, dils, gains, wg, wc, wr, wa, wo, layer, tm, side=()):
    m, d = x.shape
    a_specs = [pl.BlockSpec((dl, tm // dl, 2 * ATTN_DIM), lambda i: (0, i, 0)) for dl in dils]
    scratch = [pltpu.VMEM((6, 2, tm, LANES), F32)] if max(dils) > 1 else []
    args = [x, cy, ry, *attn, gains, wg, wc, wr, wa, wo]
    s_args, s_in, s_shape, s_out, aliases, layout = _side_operands(len(args), 1, side)
    outs = pl.pallas_call(
        functools.partial(_merge_body, dils=dils, layout=layout),
        grid=(m // tm,),
        in_specs=[_row_spec(tm, d), _row_spec(tm, CONV_DIM), _row_spec(tm, 512)] + a_specs
        + [_layer_spec(gains.shape, layer)] + [_const_spec(a.shape) for a in (wg, wc, wr, wa, wo)] + s_in,
        out_specs=[_row_spec(tm, d)] + s_out,
        out_shape=[jax.ShapeDtypeStruct((m, d), F32)] + s_shape,
        scratch_shapes=scratch,
        input_output_aliases=aliases,
        compiler_params=_params("arbitrary"),
        name="merge",
    )(*args, *s_args)
    return outs[0], outs[1:]


def _cos_sin(start, count, freq, split):
    if split is None or count % split:
        ang = (start + jnp.arange(count)).astype(F32)[:, None] * freq[None, :]
        return jnp.cos(ang), jnp.sin(ang)
    base = (start + split * jnp.arange(count // split)).astype(F32)[:, None, None] * freq
    off = jnp.arange(split).astype(F32)[None, :, None] * freq
    cb, sb, co, so = jnp.cos(base), jnp.sin(base), jnp.cos(off), jnp.sin(off)
    return (cb * co - sb * so).reshape(count, LANES), (sb * co + cb * so).reshape(count, LANES)


def _rotary_tables(start, count, split=None):
    lane = jnp.arange(LANES)
    half = RET_DK // 2
    freq = jnp.exp(-(lane % half).astype(F32) * (math.log(RET_THETA) / half))
    cos, sin = _cos_sin(start, count, freq, split)
    rc, rs = cos, jnp.where(lane < half, -sin, sin)
    dim = lane % HEAD_DIM
    half = ROT_DIM // 2
    freq = jnp.exp(-(dim % half).astype(F32) * (math.log(ROPE_THETA) / half))
    cos, sin = _cos_sin(start, count, freq, split)
    ac = jnp.where(dim < ROT_DIM, cos, 1.0)
    alo = jnp.where(dim < half, -sin, 0.0)
    ahi = jnp.where(jnp.logical_and(dim >= half, dim < ROT_DIM), sin, 0.0)
    return rc, rs, ac, alo, ahi


def _finish(x, ple, cy, ry, attn, dils, w, layer, tm, merge_side=(), ffn_side=()):
    x, merge_out = _merge(x, cy, ry, attn, dils, w['gains'], w['w_gate'], w['w_conv_out'], w['w_ret_out'],
                          w['w_attn_out'], w['w_o'], layer, tm, side=merge_side)
    x, ffn_out = _ffn(x, w['gains'], w['ffn2_gu'], w['ffn2_down'], layer, 4, 5, tm,
                      ple=(ple, w['w_ple_gate'], w['w_ple_proj']), side=ffn_side)
    return x, merge_out, ffn_out


W_FFN1 = ('ffn1_gu', 'ffn1_down')
W_EARLY = W_FFN1 + ('w_in',)
W_LATE = ('w_gate', 'w_conv_out', 'w_ret_out', 'w_attn_out', 'w_o', 'ffn2_gu', 'ffn2_down', 'w_ple_gate',
          'w_ple_proj')


def _prompt_layer(x, ple, tabs, caches_t, new_kv, raw, w, layer):
    depth = raw['w_in'].shape[0]
    s = x.shape[0]
    tm = min(ROW_TILE, s)
    steps = s // tm
    wide_dil = DILATIONS[2]
    todo = tuple(n for n in W_EARLY[2:] + W_LATE if n not in w)
    side = [_shift_job(caches_t[2], new_kv[2], None, layer, steps, 0, wide_dil)]
    if todo:
        side.append(_convert_job([raw[n] for n in todo], layer, steps))
    x, outs = _ffn(x, w['gains'], w['ffn1_gu'], w['ffn1_down'], layer, 0, 1, tm, side=side)
    wide, wide_c = outs[:2]
    w = dict(w, **dict(zip(todo, outs[2:])))
    hist = jnp.zeros((8, CONV_DIM), F32)
    outs = _proj(x, w['gains'], w['w_in'], w['w_conv'], w['b_conv'], tabs, hist, layer, tm, decode=False)
    cy, ulast, ry, ret_state = outs[:4]
    attn, kv = [], []
    for gi, (window, dil) in enumerate(ATTN_GROUPS):
        qkv = outs[4 + gi]
        attn.append(_attention(qkv))
        keep = min(window, s) // dil
        tail = jnp.swapaxes(qkv[:, s // dil - keep:, ATTN_DIM:], 0, 1)
        kv.append(tail.reshape(1, keep * dil, 2, ATTN_HEADS, HEAD_DIM))
    merge_side = [_shift_job(caches_t[gi], new_kv[gi], None, layer, steps, None, DILATIONS[gi]) for gi in (0, 1)]
    ffn_side = [_shift_job(caches_t[2], wide, wide_c, layer, steps, 1, wide_dil)]
    if layer + 1 < depth:
        merge_side.append(_convert_job([raw[n] for n in W_EARLY], layer + 1, steps))
        ffn_side.append(_convert_job([raw[n] for n in W_LATE], layer + 1, steps))
    x, m_out, f_out = _finish(x, ple, cy, ry, attn, DILATIONS, w, layer, tm, merge_side, ffn_side)
    w_next = dict(zip(W_EARLY, m_out[4:]), **dict(zip(W_LATE, f_out[2:])))
    new_kv = [m_out[0], m_out[2], f_out[0]]
    compact = [m_out[1], m_out[3], f_out[1]]
    return x, ulast[6:8][None], ret_state[None], kv, new_kv, compact, w, w_next


def _sample_layer(x, ple, tabs, gdec, conv_hist, ret_state, caches_t, prev_ret, new_kv, compact, w, layer):
    b = x.shape[0]
    x, _ = _ffn(x, w['gains'], w['ffn1_gu'], w['ffn1_down'], layer, 0, 1, b)
    outs = _proj(x, w['gains'], w['w_in'], w['w_conv'], w['b_conv'], tabs, conv_hist, layer, b, decode=True)
    cy, ulast, ret_in = outs[:3]
    ry, ret_new = _retention_decode(ret_in, gdec, ret_state, layer, prev_ret)
    attn, kv = [], []
    for gi in range(len(ATTN_GROUPS)):
        o_lse, cnew = _attention_decode(outs[3 + gi], compact[gi], caches_t[gi], new_kv[gi], layer)
        attn.append(o_lse)
        kv.append(cnew)
    x, _, _ = _finish(x, ple, cy, ry, attn, (1, 1, 1), w, layer, b)
    return x, jnp.swapaxes(ulast, 0, 1), ret_new, kv


def kernel(x_prompt, x_sample, state_conv, state_ret, cache_kv_w128, cache_kv_w512, cache_kv_w2048, p_prompt, p_sample, norm_gain, w_ffn1_gu, w_ffn1_down, w_in, w_conv, b_conv, w_conv_out, w_ret_out, w_attn_out, w_gate, w_o, w_ffn2_gu, w_ffn2_down, w_ple_gate, w_ple_proj):
    depth = norm_gain.shape[0]
    seq = x_prompt.shape[1]
    nb = x_sample.shape[0]
    assert x_prompt.shape[0] == 1 and x_sample.shape[1] == 1

    tabs_p = jnp.concatenate(_rotary_tables(0, seq, split=ROT_SPLIT), axis=1)
    tabs_s = jnp.broadcast_to(jnp.concatenate(_rotary_tables(PAST_LEN, 1), axis=1), (nb, 5 * LANES))
    gdec = jnp.broadcast_to(jnp.exp(_ret_log_decay())[:, None, None], (RET_HEADS, 1, RET_DK))
    caches_t = [jnp.transpose(c, (0, 1, 3, 4, 5, 2)) for c in (cache_kv_w128, cache_kv_w512, cache_kv_w2048)]
    raw = dict(ffn1_gu=w_ffn1_gu, ffn1_down=w_ffn1_down, w_in=w_in, w_conv_out=w_conv_out, w_ret_out=w_ret_out,
               w_attn_out=w_attn_out, w_gate=w_gate, w_o=w_o, ffn2_gu=w_ffn2_gu, ffn2_down=w_ffn2_down,
               w_ple_gate=w_ple_gate, w_ple_proj=w_ple_proj)
    shared = dict(gains=norm_gain, w_conv=w_conv, b_conv=b_conv.reshape(depth, 1, CONV_DIM))
    w_next = {n: raw[n][0].astype(BF16) for n in W_FFN1}
    ple_p = p_prompt[:, 0]
    ple_s = p_sample[:, :, 0]
    conv_hist = jnp.swapaxes(state_conv, 1, 2)

    yp, ys = x_prompt[0], x_sample[:, 0]
    conv_p, conv_s, ret_p = [], [], []
    kv_p = [[] for _ in ATTN_GROUPS]
    ret_s, kv_s = None, [None] * len(ATTN_GROUPS)
    for l in range(depth):
        yp, cp, rp, kp, kv_s, compact, w, w_next = _prompt_layer(yp, ple_p, tabs_p, caches_t, kv_s, raw,
                                                                 dict(shared, **w_next), l)
        ys, cs, ret_s, kv_s = _sample_layer(ys, ple_s, tabs_s, gdec, conv_hist, state_ret, caches_t, ret_s, kv_s,
                                            compact, w, l)
        conv_p.append(cp)
        conv_s.append(cs)
        ret_p.append(rp)
        for gi in range(len(ATTN_GROUPS)):
            kv_p[gi].append(kp[gi])
    back = lambda c: jnp.transpose(c, (0, 1, 5, 2, 3, 4))
    return (yp[None], ys[:, None], jnp.stack(conv_p), jnp.stack(conv_s), jnp.stack(ret_p), ret_s,
            jnp.stack(kv_p[0]), back(kv_s[0]), jnp.stack(kv_p[1]), back(kv_s[1]),
            jnp.stack(kv_p[2]), back(kv_s[2]))
```

```python
import functools
import math

import jax
import jax.numpy as jnp
from jax import lax
from jax.experimental import pallas as pl
from jax.experimental.pallas import tpu as pltpu

F32 = jnp.float32
BF16 = jnp.bfloat16

EPS = 1e-6
PAST_LEN = 16384
CONV_DIM = 512
RET_HEADS = 4
RET_DK = 128
RET_THETA = 10000.0
ATTN_GROUPS = ((128, 1), (512, 4), (2048, 16))
DILATIONS = tuple(d for _, d in ATTN_GROUPS)
ATTN_HEADS = 4
HEAD_DIM = 64
ATTN_DIM = ATTN_HEADS * HEAD_DIM
ATTN_SCALE = HEAD_DIM ** -0.5
ROT_DIM = 16
ROPE_THETA = 500000.0
ROT_SPLIT = 128
ATTN_BLOCK = 128
ATTN_STEP = 2048
ATTN_GROUP = 4
DEC_BLOCK_SAMPLES = 8
NEG = -1e30
LANES = 128
MXU_WIDTH = 256
BF16_ROWS = 16
OFF_CONV = 0
OFF_RET = 3 * CONV_DIM
OFF_ATTN = OFF_RET + 4 * 512

VMEM_LIMIT_BYTES = 60000 * 1024
ROW_TILE = 512


def _params(*sem):
    return pltpu.CompilerParams(dimension_semantics=sem, vmem_limit_bytes=VMEM_LIMIT_BYTES)


def _const_spec(shape):
    zeros = (0,) * len(shape)
    return pl.BlockSpec(shape, lambda *_: zeros, pipeline_mode=pl.Buffered(1))


def _layer_spec(shape, layer):
    tail = tuple(shape[1:])
    idx = (layer,) + (0,) * len(tail)
    return pl.BlockSpec((None,) + tail, lambda *_: idx, pipeline_mode=pl.Buffered(1))


def _row_spec(tm, width):
    return pl.BlockSpec((tm, width), lambda i: (i, 0))


def _rms(x, g):
    return x * lax.rsqrt(jnp.mean(x * x, axis=-1, keepdims=True) + EPS) * g


def _dot(a, b):
    return jnp.dot(a, b, preferred_element_type=F32)


def _dot_nt(a, b):
    return lax.dot_general(a, b, (((1,), (1,)), ((), ())), preferred_element_type=F32)


def _silu(x):
    return x * jax.nn.sigmoid(x)


def _ff_chunks(d_ff):
    first = -(-(d_ff // 2) // MXU_WIDTH) * MXU_WIDTH
    return ((0, first), (first, d_ff - first)) if 0 < first < d_ff else ((0, d_ff),)


def _side_operands(n_args, n_outs, side):
    args, in_specs, out_shape, out_specs, aliases, layout = [], [], [], [], {}, []
    for job in side:
        for src, dst in job['aliases'].items():
            aliases[n_args + len(args) + src] = n_outs + len(out_shape) + dst
        layout.append((len(job['args']), len(job['out_shape']), job['fn']))
        args += job['args']
        in_specs += job['in_specs']
        out_shape += job['out_shape']
        out_specs += job['out_specs']
    return args, in_specs, out_shape, out_specs, aliases, tuple(layout)


def _run_side(layout, in_refs, out_refs):
    i = o = 0
    for n_in, n_out, fn in layout:
        fn(in_refs[i:i + n_in], out_refs[o:o + n_out])
        i += n_in
        o += n_out


def _shift_job(cache_t, new_cache, compact, layer, n_steps, kv, dil):
    _, b, _, nh, hd, window = cache_t.shape
    n_pos = window // dil
    bs = -(-b // n_steps)
    assert b % bs == 0
    n_blk = b // bs
    n_kv, kv_idx = (2, 0) if kv is None else (1, kv)
    blk = lambda i: jnp.minimum(i, n_blk - 1)
    spec = pl.BlockSpec((None, bs, n_kv, nh, hd, window), lambda i: (layer, blk(i), kv_idx, 0, 0, 0))
    cspec = pl.BlockSpec((bs, n_kv, nh * hd, n_pos), lambda i: (blk(i), kv_idx, 0, 0))
    pos = jnp.arange(window)[:, None] == dil * jnp.arange(n_pos)[None, :]
    pick = pos.astype(BF16)

    def fn(in_refs, out_refs):
        src, pick_ref = in_refs[:2]
        dst, cdst = out_refs
        for s in range(bs):
            for j in range(n_kv):
                for hh in range(nh):
                    t = src[s, j, hh]
                    dst[s, j, hh] = pltpu.roll(t, window - 1, 1)
                    tb = t.astype(BF16)
                    if dil > 1:
                        tb = _dot(tb, pick_ref[...]).astype(BF16)
                    cdst[s, j, hh * hd:(hh + 1) * hd, :] = tb

    job = dict(args=[cache_t, pick], in_specs=[spec, _const_spec(pick.shape)],
               out_shape=[jax.ShapeDtypeStruct(cache_t.shape, cache_t.dtype),
                          jax.ShapeDtypeStruct((b, 2, nh * hd, n_pos), BF16)],
               out_specs=[spec, cspec], aliases={}, fn=fn)
    for out_idx, buf in enumerate((new_cache, compact)):
        if buf is not None:
            job['aliases'][len(job['args'])] = out_idx
            job['args'].append(buf)
            job['in_specs'].append(pl.BlockSpec(memory_space=pl.ANY))
    return job


def _convert_job(weights, layer, n_steps):
    job = dict(args=[], in_specs=[], out_shape=[], out_specs=[], aliases={})
    for wt in weights:
        _, k, n = wt.shape
        n_band = max(c for c in range(1, n_steps + 1)
                     if n_steps % c == 0 and k % c == 0 and ((k // c) % BF16_ROWS == 0 or c == 1))
        band = lambda i, n_band=n_band: i * n_band // n_steps
        job['args'].append(wt)
        job['in_specs'].append(pl.BlockSpec((None, k // n_band, n), lambda i, band=band: (layer, band(i), 0)))
        job['out_shape'].append(jax.ShapeDtypeStruct((k, n), BF16))
        job['out_specs'].append(pl.BlockSpec((k // n_band, n), lambda i, band=band: (band(i), 0)))

    def fn(in_refs, out_refs):
        for src, dst in zip(in_refs, out_refs):
            dst[...] = src[...].astype(BF16)

    job['fn'] = fn
    return job


def _ffn_body(x_ref, g_ref, wgu_ref, wd_ref, *rest, pre, post, d_ff, with_ple, layout):
    n_ple = 3 if with_ple else 0
    n_side_in = sum(n_in for n_in, _, _ in layout)
    o_ref = rest[n_ple + n_side_in]
    _run_side(layout, rest[n_ple:n_ple + n_side_in], rest[n_ple + n_side_in + 1:])
    x = x_ref[...]
    h = _rms(x, g_ref[pre:pre + 1, :]).astype(BF16)
    tm = x.shape[0]
    halves = (slice(0, tm // 2), slice(tm // 2, tm)) if tm % (2 * MXU_WIDTH) == 0 else (slice(0, tm),)
    chunks = _ff_chunks(d_ff)
    acc = None
    for ci, (start, size) in enumerate(chunks):
        wg = wgu_ref[:, start:start + size]
        if ci == 0:
            gate = jnp.concatenate([_dot(h[r], wg) for r in halves], axis=0)
        else:
            gate = _dot(h, wg)
        up = _dot(h, wgu_ref[:, d_ff + start:d_ff + start + size])
        a = (_silu(gate) * up).astype(BF16)
        if ci < len(chunks) - 1:
            y = _dot(a, wd_ref[start:start + size, :])
            acc = y if acc is None else acc + y
    out = []
    for r in halves:
        y = _dot(a[r], wd_ref[start:start + size, :])
        y = y if acc is None else acc[r] + y
        out.append(x[r] + 0.5 * _rms(y, g_ref[post:post + 1, :]))
    if with_ple:
        p_ref, wg_ref, wp_ref = rest[:3]
        proj = _dot(p_ref[...].astype(BF16), wp_ref[...])
        for i, r in enumerate(halves):
            h = _rms(out[i], g_ref[6:7, :]).astype(BF16)
            gate = jax.nn.sigmoid(_dot(h, wg_ref[...]))
            out[i] = out[i] + _rms(gate * proj[r], g_ref[7:8, :])
    for r, v in zip(halves, out):
        o_ref[r, :] = v


def _ffn(x, gains, wgu, wd, layer, pre, post, tm, ple=None, side=()):
    m, d = x.shape
    d_ff = wd.shape[0]
    args = [x, gains, wgu, wd]
    in_specs = [_row_spec(tm, d), _layer_spec(gains.shape, layer), _const_spec(wgu.shape), _const_spec(wd.shape)]
    if ple is not None:
        p, wg, wp = ple
        args += [p, wg, wp]
        in_specs += [pl.BlockSpec((None, tm, p.shape[2]), lambda i: (layer, i, 0)),
                     _const_spec(wg.shape), _const_spec(wp.shape)]
    s_args, s_in, s_shape, s_out, aliases, layout = _side_operands(len(args), 1, side)
    body = functools.partial(_ffn_body, pre=pre, post=post, d_ff=d_ff, with_ple=ple is not None, layout=layout)
    outs = pl.pallas_call(
        body,
        grid=(m // tm,),
        in_specs=in_specs + s_in,
        out_specs=[_row_spec(tm, d)] + s_out,
        out_shape=[jax.ShapeDtypeStruct((m, d), F32)] + s_shape,
        input_output_aliases=aliases,
        compiler_params=_params("arbitrary"),
        name="ffn",
    )(*args, *s_args)
    return outs[0], outs[1:]


def _rot_ret(x, cos, sin):
    return x * cos + pltpu.roll(x, RET_DK // 2, 1) * sin


def _rot_attn(x, cos, sin_lo, sin_hi):
    return x * cos + pltpu.roll(x, LANES - ROT_DIM // 2, 1) * sin_lo + pltpu.roll(x, ROT_DIM // 2, 1) * sin_hi


def _to_streams(halves, out_ref, lane0, scr_ref, dil):
    if dil == 1:
        for s, v in enumerate(halves):
            out_ref[0, :, lane0 + s * LANES:lane0 + (s + 1) * LANES] = v
        return
    n = halves[0].shape[0] // dil
    for s, v in enumerate(halves):
        scr_ref[s] = v
    for r in range(dil):
        for s in range(2):
            out_ref[r, :, lane0 + s * LANES:lane0 + (s + 1) * LANES] = scr_ref[s, pl.ds(r, n, stride=dil), :]


def _proj_body(*refs, dils):
    x_ref, g_ref, w_ref, wc_ref, bc_ref, tab_ref, hist_ref = refs[:7]
    decode = dils is None
    if decode:
        cy_ref, ulast_ref, ret_ref = refs[7:10]
        attn_refs = refs[10:13]
    else:
        decay_ref, qw_ref, kw_ref, gch_ref = refs[7:11]
        cy_ref, ulast_ref, ret_ref, sout_ref = refs[11:15]
        attn_refs = refs[15:18]
        carry_ref, scr_ref, s_scr = refs[18:21]
    tab = lambda j: tab_ref[:, j * LANES:(j + 1) * LANES]

    x = x_ref[...]
    h = _rms(x, g_ref[2:3, :]).astype(BF16)

    def part(off, width):
        return _dot(h, w_ref[:, off:off + width])

    def mixer_c(gi):
        acos, alo, ahi = tab(2), tab(3), tab(4)
        base = OFF_ATTN + gi * 3 * ATTN_DIM
        zq = part(base, ATTN_DIM)
        zk = part(base + ATTN_DIM, ATTN_DIM)
        zv = part(base + 2 * ATTN_DIM, ATTN_DIM)
        halves = lambda z: [z[:, s * LANES:(s + 1) * LANES] for s in range(2)]
        q_h = [_rot_attn(v, acos, alo, ahi) for v in halves(zq)]
        k_h = [_rot_attn(v, acos, alo, ahi) for v in halves(zk)]
        out_ref = attn_refs[gi]
        for j, vals in enumerate((q_h, k_h, halves(zv))):
            if decode:
                for s, v in enumerate(vals):
                    out_ref[:, j * ATTN_DIM + s * LANES:j * ATTN_DIM + (s + 1) * LANES] = v
            else:
                _to_streams(vals, out_ref, j * ATTN_DIM, scr_ref.at[3 * gi + j], dils[gi])

    def mixer_a():
        rows = h.shape[0]
        halves = (slice(0, rows // 2), slice(rows // 2, rows)) if rows % (2 * MXU_WIDTH) == 0 else (slice(0, rows),)
        split = lambda off: jnp.concatenate([_dot(h[r], w_ref[:, off:off + CONV_DIM]) for r in halves], axis=0)
        b_gate = split(OFF_CONV)
        u = split(OFF_CONV + CONV_DIM) * split(OFF_CONV + 2 * CONV_DIM)
        if decode:
            u2 = hist_ref[0]
            u1 = hist_ref[1]
            ulast_ref[0] = u1
            ulast_ref[1] = u
        else:
            tm = u.shape[0]

            @pl.when(pl.program_id(0) == 0)
            def _():
                carry_ref[...] = hist_ref[...]

            carry = carry_ref[...]
            row = lax.broadcasted_iota(jnp.int32, u.shape, 0)
            u1 = jnp.where(row == 0, carry[7:8, :], pltpu.roll(u, 1, 0))
            u2 = jnp.where(row == 0, carry[6:7, :], jnp.where(row == 1, carry[7:8, :], pltpu.roll(u, 2, 0)))
            carry_ref[...] = u[tm - 8:tm, :]
            ulast_ref[...] = u[tm - 8:tm, :]
        y = bc_ref[...] + wc_ref[0:1, :] * u2
        y = y + wc_ref[1:2, :] * u1
        y = y + wc_ref[2:3, :] * u
        cy_ref[...] = (b_gate * y).astype(cy_ref.dtype)

    def mixer_b_inputs():
        tm = h.shape[0]
        halves = (slice(0, tm // 2), slice(tm // 2, tm)) if tm % (2 * MXU_WIDTH) == 0 else (slice(0, tm),)
        zq = jnp.concatenate([_dot(h[r], w_ref[:, OFF_RET:OFF_RET + 512]) for r in halves], axis=0)
        return [zq] + [part(OFF_RET + j * 512, 512) for j in range(1, 4)]

    def mixer_b(z, heads):
        zq, zk, zv, zg = z
        cos, sin = tab(0), tab(1)
        for hh in heads:
            cols = slice(hh * RET_DK, (hh + 1) * RET_DK)
            q = _rot_ret(zq[:, cols], cos, sin)
            k = _rot_ret(zk[:, cols], cos, sin) * (RET_DK ** -0.5)
            if decode:
                ret_ref[:, cols] = q
                ret_ref[:, 512 + hh * RET_DK:512 + (hh + 1) * RET_DK] = k
                ret_ref[:, 1024 + hh * RET_DK:1024 + (hh + 1) * RET_DK] = zv[:, cols]
                ret_ref[:, 1536 + hh * RET_DK:1536 + (hh + 1) * RET_DK] = zg[:, cols]
                continue
            v = zv[:, cols].astype(BF16)
            s = s_scr[hh]
            scores = _dot_nt(q.astype(BF16), k.astype(BF16)) * decay_ref[hh]
            o = _dot(scores.astype(BF16), v) + _dot((q * qw_ref[hh]).astype(BF16), s.astype(BF16))
            kv = _dot(jnp.transpose(k * kw_ref[hh]).astype(BF16), v)
            s_scr[hh] = s * gch_ref[hh] + kv
            ret_ref[:, cols] = (_head_norm(o) * _silu(zg[:, cols])).astype(ret_ref.dtype)

    if not decode:
        @pl.when(pl.program_id(0) == 0)
        def _():
            s_scr[...] = jnp.zeros_like(s_scr)

    z = mixer_b_inputs()
    mixer_c(2)
    mixer_b(z, (0, 1))
    mixer_c(1)
    mixer_b(z, (2, 3))
    mixer_c(0)
    mixer_a()

    if not decode:
        @pl.when(pl.program_id(0) == pl.num_programs(0) - 1)
        def _():
            sout_ref[...] = s_scr[...]


def _proj(x, gains, w_in, w_conv, b_conv, tabs, hist, layer, tm, decode):
    m, d = x.shape
    n = m // tm
    qkv = 3 * ATTN_DIM
    args = [x, gains, w_in, w_conv, b_conv, tabs, hist]
    state = (RET_HEADS, RET_DK, RET_DK)
    if decode:
        dils = None
        hist_spec = pl.BlockSpec((None, 2, tm, CONV_DIM), lambda i: (layer, 0, i, 0))
        ulast_spec = pl.BlockSpec((2, tm, CONV_DIM), lambda i: (0, i, 0))
        ulast_shape = jax.ShapeDtypeStruct((2, m, CONV_DIM), F32)
        ret_specs = [_row_spec(tm, 4 * 512)]
        ret_shapes = [jax.ShapeDtypeStruct((m, 4 * 512), F32)]
        attn_specs = [_row_spec(tm, qkv)] * 3
        attn_shapes = [jax.ShapeDtypeStruct((m, qkv), F32)] * 3
        scratch = []
    else:
        dils = DILATIONS
        hist_spec = _const_spec((8, CONV_DIM))
        ulast_spec = pl.BlockSpec((8, CONV_DIM), lambda i: (0, 0))
        ulast_shape = jax.ShapeDtypeStruct((8, CONV_DIM), F32)
        ret_specs = [_row_spec(tm, 512), pl.BlockSpec(state, lambda i: (0, 0, 0))]
        ret_shapes = [jax.ShapeDtypeStruct((m, 512), BF16), jax.ShapeDtypeStruct(state, F32)]
        attn_specs = [pl.BlockSpec((dl, tm // dl, qkv), lambda i: (0, i, 0)) for dl in dils]
        attn_shapes = [jax.ShapeDtypeStruct((dl, m // dl, qkv), F32) for dl in dils]
        scratch = [pltpu.VMEM((8, CONV_DIM), F32), pltpu.VMEM((9, 2, tm, LANES), F32), pltpu.VMEM(state, F32)]
        args += list(_ret_tables(tm))
    in_specs = [
        _row_spec(tm, d), _layer_spec(gains.shape, layer), _const_spec(w_in.shape),
        _layer_spec(w_conv.shape, layer), _layer_spec(b_conv.shape, layer), _row_spec(tm, tabs.shape[1]), hist_spec,
    ] + [_const_spec(a.shape) for a in args[7:]]
    out_specs = [_row_spec(tm, CONV_DIM), ulast_spec] + ret_specs + attn_specs
    out_shape = [jax.ShapeDtypeStruct((m, CONV_DIM), BF16), ulast_shape] + ret_shapes + attn_shapes
    return pl.pallas_call(
        functools.partial(_proj_body, dils=dils),
        grid=(n,),
        in_specs=in_specs,
        out_specs=out_specs,
        out_shape=out_shape,
        scratch_shapes=scratch,
        compiler_params=_params("arbitrary"),
        name="proj",
    )(*args)


def _head_norm(o):
    mu = jnp.mean(o, axis=-1, keepdims=True)
    var = jnp.mean(jnp.square(o - mu), axis=-1, keepdims=True)
    return (o - mu) * lax.rsqrt(var + EPS)


def _ret_log_decay():
    return jnp.log1p(-jnp.exp2(-5.0 - jnp.arange(RET_HEADS, dtype=F32)))


def _ret_tables(chunk):
    i = jnp.arange(chunk, dtype=F32)
    log_g = _ret_log_decay()
    diff = i[:, None] - i[None, :]
    decay = jnp.where(diff[None] >= 0, jnp.exp(jnp.maximum(diff, 0.0)[None] * log_g[:, None, None]), 0.0)
    k_w = jnp.exp((chunk - 1 - i)[:, None] * log_g[None, :])
    q_w = jnp.exp((i + 1)[:, None] * log_g[None, :])
    g_chunk = jnp.exp(chunk * log_g)
    qw = jnp.broadcast_to(q_w.T[:, :, None], (RET_HEADS, chunk, RET_DK))
    kw = jnp.broadcast_to(k_w.T[:, :, None], (RET_HEADS, chunk, RET_DK))
    gch = jnp.broadcast_to(g_chunk[:, None, None], (RET_HEADS, RET_DK, RET_DK))
    return decay, qw, kw, gch


def _ret_dec_body(r_ref, gd_ref, s0_ref, *rest):
    y_ref, sout_ref = rest[-2:]
    for b in range(r_ref.shape[0]):
        for hh in range(RET_HEADS):
            part = lambda j: r_ref[b:b + 1, j * 512 + hh * RET_DK:j * 512 + (hh + 1) * RET_DK]
            q, k, v = part(0), part(1), part(2)
            gd = gd_ref[hh]
            s0 = s0_ref[b, hh]
            wide = (RET_DK, RET_DK)
            q_col = jnp.transpose(jnp.broadcast_to(q * gd, wide))
            k_col = jnp.transpose(jnp.broadcast_to(k, wide))
            o_inter = jnp.sum(q_col * s0, axis=0, keepdims=True)
            o_intra = jnp.sum(q * k, axis=-1, keepdims=True) * v
            sout_ref[b, hh] = s0 * gd + k_col * v
            o = o_intra + o_inter
            y_ref[b:b + 1, hh * RET_DK:(hh + 1) * RET_DK] = _head_norm(o) * _silu(part(3))


def _retention_decode(r, gdec, state, layer, prev_out):
    b = r.shape[0]
    bs = DEC_BLOCK_SAMPLES
    while b % bs:
        bs -= 1
    st = pl.BlockSpec((None, bs, RET_HEADS, RET_DK, RET_DK), lambda i: (layer, i, 0, 0, 0))
    args = [r, gdec, state]
    in_specs = [_row_spec(bs, r.shape[1]), _const_spec(gdec.shape), st]
    aliases = {}
    if prev_out is not None:
        args.append(prev_out)
        in_specs.append(pl.BlockSpec(memory_space=pl.ANY))
        aliases = {3: 1}
    return pl.pallas_call(
        _ret_dec_body,
        grid=(b // bs,),
        in_specs=in_specs,
        out_specs=[_row_spec(bs, 512), st],
        out_shape=[jax.ShapeDtypeStruct((b, 512), F32), jax.ShapeDtypeStruct(state.shape, F32)],
        input_output_aliases=aliases,
        compiler_params=_params("parallel"),
        name="retention_decode",
    )(*args)


def _attn_body(cur_ref, prev_ref, out_ref, *, n_sub):
    n = pl.program_id(1)
    qb = ATTN_BLOCK
    ii = lax.broadcasted_iota(jnp.int32, (2 * qb, 2 * qb), 0) & (qb - 1)
    jj = lax.broadcasted_iota(jnp.int32, (2 * qb, 2 * qb), 1)
    band = jnp.logical_and(jj >= ii, jj <= ii + qb)
    first = jnp.logical_and(band, jnp.logical_or(jj >= qb, n > 0))
    lo = lax.broadcasted_iota(jnp.int32, (qb, LANES), 1) < HEAD_DIM
    for hp in range(ATTN_HEADS // 2):
        cols = lambda j: slice(j * ATTN_DIM + hp * LANES, j * ATTN_DIM + (hp + 1) * LANES)
        kk = jnp.concatenate([prev_ref[:, cols(1)], cur_ref[:, cols(1)]], axis=0).astype(BF16)
        vv = jnp.concatenate([prev_ref[:, cols(2)], cur_ref[:, cols(2)]], axis=0).astype(BF16)
        for b0 in range(0, n_sub, ATTN_GROUP):
            blocks = range(b0, min(b0 + ATTN_GROUP, n_sub))
            scores = []
            for b in blocks:
                q2 = cur_ref[b * qb:(b + 1) * qb, cols(0)] * ATTN_SCALE
                zero = jnp.zeros_like(q2)
                qs = jnp.concatenate([jnp.where(lo, q2, zero), jnp.where(lo, zero, q2)], axis=0).astype(BF16)
                scores.append(_dot_nt(qs, kk[b * qb:(b + 2) * qb]))
            probs = []
            for b, s in zip(blocks, scores):
                s = jnp.where(first if b == 0 else band, s, NEG)
                mx = jnp.max(s, axis=-1, keepdims=True)
                p = jnp.exp(s - mx)
                den = jnp.sum(p, axis=-1, keepdims=True)
                probs.append((p.astype(BF16), den, mx + jnp.log(den)))
            for b, (p, den, lse) in zip(blocks, probs):
                rows = slice(b * qb, (b + 1) * qb)
                o = _dot(p, vv[b * qb:(b + 2) * qb]) / den
                out_ref[rows, cols(0)] = jnp.where(lo, o[:qb], o[qb:])
                out_ref[rows, cols(1)] = jnp.where(lo, lse[:qb], lse[qb:])


def _attention(qkv):
    dil, length, width = qkv.shape
    step = min(ATTN_STEP, length)
    n_sub = step // ATTN_BLOCK
    cur = pl.BlockSpec((None, step, width), lambda r, n: (r, n, 0))
    prev = pl.BlockSpec((None, ATTN_BLOCK, width), lambda r, n: (r, jnp.maximum(n * n_sub - 1, 0), 0))
    return pl.pallas_call(
        functools.partial(_attn_body, n_sub=n_sub),
        grid=(dil, length // step),
        in_specs=[cur, prev],
        out_specs=pl.BlockSpec((None, step, 2 * ATTN_DIM), lambda r, n: (r, n, 0)),
        out_shape=jax.ShapeDtypeStruct((dil, length, 2 * ATTN_DIM), F32),
        compiler_params=_params("parallel", "arbitrary"),
        name="attention",
    )(qkv, qkv)


def _attn_dec_body(qkv_ref, kc_ref, tail_ref, new_ref, out_ref, cout_ref):
    del new_ref
    rows = BF16_ROWS
    head_of_lane = lax.broadcasted_iota(jnp.int32, (rows, ATTN_DIM), 1) // HEAD_DIM
    own = head_of_lane == lax.broadcasted_iota(jnp.int32, (rows, ATTN_DIM), 0)
    last = lax.broadcasted_iota(jnp.int32, (HEAD_DIM, LANES), 1) == LANES - 1
    wide = lambda a: jnp.broadcast_to(a, (rows, ATTN_DIM))
    pick = lambda a: jnp.sum(jnp.where(own, a, 0.0), axis=0, keepdims=True)
    for b in range(qkv_ref.shape[0]):
        q, kn, vn = (qkv_ref[b:b + 1, j * ATTN_DIM:(j + 1) * ATTN_DIM] for j in range(3))
        q_rows = jnp.where(own, wide(q), 0.0)
        s = _dot(q_rows.astype(BF16), kc_ref[b, 0]) * ATTN_SCALE
        s_new = jnp.sum(q_rows * wide(kn), axis=-1, keepdims=True) * ATTN_SCALE
        mx = jnp.maximum(jnp.max(s, axis=-1, keepdims=True), s_new)
        p = jnp.exp(s - mx)
        p_new = jnp.exp(s_new - mx)
        den = jnp.sum(p, axis=-1, keepdims=True) + p_new
        o = (_dot_nt(p.astype(BF16), kc_ref[b, 1]) + p_new * wide(vn)) / den
        out_ref[b:b + 1, 0:ATTN_DIM] = pick(o)
        out_ref[b:b + 1, ATTN_DIM:2 * ATTN_DIM] = pick(wide(mx + jnp.log(den)))
        new_rows = jnp.concatenate([kn, vn], axis=1)
        new_cols = jnp.transpose(jnp.broadcast_to(new_rows, (LANES, 2 * ATTN_DIM)))
        for kv in range(2):
            for hh in range(ATTN_HEADS):
                col = new_cols[kv * ATTN_DIM + hh * HEAD_DIM:kv * ATTN_DIM + (hh + 1) * HEAD_DIM, :]
                cout_ref[b, kv, hh] = jnp.where(last, col, pltpu.roll(tail_ref[b, kv, hh], LANES - 1, 1))


def _attention_decode(qkv, compact, cache_t, new_cache, layer):
    b = qkv.shape[0]
    window = cache_t.shape[-1]
    bs = DEC_BLOCK_SAMPLES
    while b % bs:
        bs -= 1
    kc = pl.BlockSpec((bs,) + compact.shape[1:], lambda i: (i, 0, 0, 0))
    tail = pl.BlockSpec((None, bs, 2, ATTN_HEADS, HEAD_DIM, LANES),
                        lambda i: (layer, i, 0, 0, 0, window // LANES - 1))
    out, cnew = pl.pallas_call(
        _attn_dec_body,
        grid=(b // bs,),
        in_specs=[_row_spec(bs, qkv.shape[1]), kc, tail, pl.BlockSpec(memory_space=pl.ANY)],
        out_specs=[_row_spec(bs, 2 * ATTN_DIM), tail],
        out_shape=[jax.ShapeDtypeStruct((b, 2 * ATTN_DIM), F32), jax.ShapeDtypeStruct(cache_t.shape, F32)],
        input_output_aliases={3: 1},
        compiler_params=_params("parallel"),
        name="attention_decode",
    )(qkv, compact, cache_t, new_cache)
    return out[None], cnew


def _from_streams(in_ref, lane0, scr_ref, dil):
    if dil == 1:
        return in_ref[0, :, lane0:lane0 + ATTN_DIM]
    n = in_ref.shape[1]
    for r in range(dil):
        for s in range(2):
            scr_ref[s, pl.ds(r, n, stride=dil), :] = in_ref[r, :, lane0 + s * LANES:lane0 + (s + 1) * LANES]
    return jnp.concatenate([scr_ref[0], scr_ref[1]], axis=1)


def _merge_body(x_ref, cy_ref, ry_ref, a0_ref, a1_ref, a2_ref, g_ref,
                wg_ref, wc_ref, wr_ref, wa_ref, wo_ref, *rest, dils, layout):
    n_side_in = sum(n_in for n_in, _, _ in layout)
    n_side_out = sum(n_out for _, n_out, _ in layout)
    out_ref = rest[n_side_in]
    scratch = rest[n_side_in + 1 + n_side_out:]
    _run_side(layout, rest[:n_side_in], rest[n_side_in + 1:n_side_in + 1 + n_side_out])
    x = x_ref[...]
    h = _rms(x, g_ref[2:3, :]).astype(BF16)
    scr = scratch[0] if scratch else None
    d = x.shape[1]
    tm = x.shape[0]
    halves = (slice(0, tm // 2), slice(tm // 2, tm)) if tm % (2 * MXU_WIDTH) == 0 else (slice(0, tm),)
    gate = jnp.concatenate([_dot(h[r], wg_ref[:, 0:d]) for r in halves], axis=0)
    merged = jax.nn.sigmoid(gate) * _dot(cy_ref[...].astype(BF16), wc_ref[...])
    merged = merged + jax.nn.sigmoid(_dot(h, wg_ref[:, d:2 * d])) * _dot(ry_ref[...].astype(BF16), wr_ref[...])
    attn_gate = jax.nn.sigmoid(_dot(h, wg_ref[:, 2 * d:3 * d]))
    vals = []
    for j in range(6):
        ref = (a0_ref, a1_ref, a2_ref)[j // 2]
        vals.append(_from_streams(ref, (j % 2) * ATTN_DIM, None if scr is None else scr.at[j], dils[j // 2]))
    o0, l0, o1, l1, o2, l2 = vals
    mx = jnp.maximum(jnp.maximum(l0, l1), l2)
    e0, e1, e2 = jnp.exp(l0 - mx), jnp.exp(l1 - mx), jnp.exp(l2 - mx)
    den = e0 + e1 + e2
    attn_y = ((e0 / den) * o0 + (e1 / den) * o1 + (e2 / den) * o2).astype(BF16)
    merged = (merged + attn_gate * _dot(attn_y, wa_ref[...])).astype(BF16)
    for r in halves:
        out_ref[r, :] = x[r] + _rms(_dot(merged[r], wo_ref[...]), g_ref[3:4, :])


def _merge(x, cy, ry, attn, dils, gains, wg, wc, wr, wa, wo, layer, tm, side=()):
    m, d = x.shape
    a_specs = [pl.BlockSpec((dl, tm // dl, 2 * ATTN_DIM), lambda i: (0, i, 0)) for dl in dils]
    scratch = [pltpu.VMEM((6, 2, tm, LANES), F32)] if max(dils) > 1 else []
    args = [x, cy, ry, *attn, gains, wg, wc, wr, wa, wo]
    s_args, s_in, s_shape, s_out, aliases, layout = _side_operands(len(args), 1, side)
    outs = pl.pallas_call(
        functools.partial(_merge_body, dils=dils, layout=layout),
        grid=(m // tm,),
        in_specs=[_row_spec(tm, d), _row_spec(tm, CONV_DIM), _row_spec(tm, 512)] + a_specs
        + [_layer_spec(gains.shape, layer)] + [_const_spec(a.shape) for a in (wg, wc, wr, wa, wo)] + s_in,
        out_specs=[_row_spec(tm, d)] + s_out,
        out_shape=[jax.ShapeDtypeStruct((m, d), F32)] + s_shape,
        scratch_shapes=scratch,
        input_output_aliases=aliases,
        compiler_params=_params("arbitrary"),
        name="merge",
    )(*args, *s_args)
    return outs[0], outs[1:]


def _cos_sin(start, count, freq, split):
    if split is None or count % split:
        ang = (start + jnp.arange(count)).astype(F32)[:, None] * freq[None, :]
        return jnp.cos(ang), jnp.sin(ang)
    base = (start + split * jnp.arange(count // split)).astype(F32)[:, None, None] * freq
    off = jnp.arange(split).astype(F32)[None, :, None] * freq
    cb, sb, co, so = jnp.cos(base), jnp.sin(base), jnp.cos(off), jnp.sin(off)
    return (cb * co - sb * so).reshape(count, LANES), (sb * co + cb * so).reshape(count, LANES)


def _rotary_tables(start, count, split=None):
    lane = jnp.arange(LANES)
    half = RET_DK // 2
    freq = jnp.exp(-(lane % half).astype(F32) * (math.log(RET_THETA) / half))
    cos, sin = _cos_sin(start, count, freq, split)
    rc, rs = cos, jnp.where(lane < half, -sin, sin)
    dim = lane % HEAD_DIM
    half = ROT_DIM // 2
    freq = jnp.exp(-(dim % half).astype(F32) * (math.log(ROPE_THETA) / half))
    cos, sin = _cos_sin(start, count, freq, split)
    ac = jnp.where(dim < ROT_DIM, cos, 1.0)
    alo = jnp.where(dim < half, -sin, 0.0)
    ahi = jnp.where(jnp.logical_and(dim >= half, dim < ROT_DIM), sin, 0.0)
    return rc, rs, ac, alo, ahi


def _finish(x, ple, cy, ry, attn, dils, w, layer, tm, merge_side=(), ffn_side=()):
    x, merge_out = _merge(x, cy, ry, attn, dils, w['gains'], w['w_gate'], w['w_conv_out'], w['w_ret_out'],
                          w['w_attn_out'], w['w_o'], layer, tm, side=merge_side)
    x, ffn_out = _ffn(x, w['gains'], w['ffn2_gu'], w['ffn2_down'], layer, 4, 5, tm,
                      ple=(ple, w['w_ple_gate'], w['w_ple_proj']), side=ffn_side)
    return x, merge_out, ffn_out


W_FFN1 = ('ffn1_gu', 'ffn1_down')
W_EARLY = W_FFN1 + ('w_in',)
W_LATE = ('w_gate', 'w_conv_out', 'w_ret_out', 'w_attn_out', 'w_o', 'ffn2_gu', 'ffn2_down', 'w_ple_gate',
          'w_ple_proj')


def _prompt_layer(x, ple, tabs, caches_t, new_kv, raw, w, layer):
    depth = raw['w_in'].shape[0]
    s = x.shape[0]
    tm = min(ROW_TILE, s)
    steps = s // tm
    wide_dil = DILATIONS[2]
    todo = tuple(n for n in W_EARLY[2:] + W_LATE if n not in w)
    side = [_shift_job(caches_t[2], new_kv[2], None, layer, steps, 0, wide_dil)]
    if todo:
        side.append(_convert_job([raw[n] for n in todo], layer, steps))
    x, outs = _ffn(x, w['gains'], w['ffn1_gu'], w['ffn1_down'], layer, 0, 1, tm, side=side)
    wide, wide_c = outs[:2]
    w = dict(w, **dict(zip(todo, outs[2:])))
    hist = jnp.zeros((8, CONV_DIM), F32)
    outs = _proj(x, w['gains'], w['w_in'], w['w_conv'], w['b_conv'], tabs, hist, layer, tm, decode=False)
    cy, ulast, ry, ret_state = outs[:4]
    attn, kv = [], []
    for gi, (window, dil) in enumerate(ATTN_GROUPS):
        qkv = outs[4 + gi]
        attn.append(_attention(qkv))
        keep = min(window, s) // dil
        tail = jnp.swapaxes(qkv[:, s // dil - keep:, ATTN_DIM:], 0, 1)
        kv.append(tail.reshape(1, keep * dil, 2, ATTN_HEADS, HEAD_DIM))
    merge_side = [_shift_job(caches_t[gi], new_kv[gi], None, layer, steps, None, DILATIONS[gi]) for gi in (0, 1)]
    ffn_side = [_shift_job(caches_t[2], wide, wide_c, layer, steps, 1, wide_dil)]
    if layer + 1 < depth:
        merge_side.append(_convert_job([raw[n] for n in W_EARLY], layer + 1, steps))
        ffn_side.append(_convert_job([raw[n] for n in W_LATE], layer + 1, steps))
    x, m_out, f_out = _finish(x, ple, cy, ry, attn, DILATIONS, w, layer, tm, merge_side, ffn_side)
    w_next = dict(zip(W_EARLY, m_out[4:]), **dict(zip(W_LATE, f_out[2:])))
    new_kv = [m_out[0], m_out[2], f_out[0]]
    compact = [m_out[1], m_out[3], f_out[1]]
    return x, ulast[6:8][None], ret_state[None], kv, new_kv, compact, w, w_next


def _sample_layer(x, ple, tabs, gdec, conv_hist, ret_state, caches_t, prev_ret, new_kv, compact, w, layer):
    b = x.shape[0]
    x, _ = _ffn(x, w['gains'], w['ffn1_gu'], w['ffn1_down'], layer, 0, 1, b)
    outs = _proj(x, w['gains'], w['w_in'], w['w_conv'], w['b_conv'], tabs, conv_hist, layer, b, decode=True)
    cy, ulast, ret_in = outs[:3]
    ry, ret_new = _retention_decode(ret_in, gdec, ret_state, layer, prev_ret)
    attn, kv = [], []
    for gi in range(len(ATTN_GROUPS)):
        o_lse, cnew = _attention_decode(outs[3 + gi], compact[gi], caches_t[gi], new_kv[gi], layer)
        attn.append(o_lse)
        kv.append(cnew)
    x, _, _ = _finish(x, ple, cy, ry, attn, (1, 1, 1), w, layer, b)
    return x, jnp.swapaxes(ulast, 0, 1), ret_new, kv


def kernel(x_prompt, x_sample, state_conv, state_ret, cache_kv_w128, cache_kv_w512, cache_kv_w2048, p_prompt, p_sample, norm_gain, w_ffn1_gu, w_ffn1_down, w_in, w_conv, b_conv, w_conv_out, w_ret_out, w_attn_out, w_gate, w_o, w_ffn2_gu, w_ffn2_down, w_ple_gate, w_ple_proj):
    depth = norm_gain.shape[0]
    seq = x_prompt.shape[1]
    nb = x_sample.shape[0]
    assert x_prompt.shape[0] == 1 and x_sample.shape[1] == 1

    tabs_p = jnp.concatenate(_rotary_tables(0, seq, split=ROT_SPLIT), axis=1)
    tabs_s = jnp.broadcast_to(jnp.concatenate(_rotary_tables(PAST_LEN, 1), axis=1), (nb, 5 * LANES))
    gdec = jnp.broadcast_to(jnp.exp(_ret_log_decay())[:, None, None], (RET_HEADS, 1, RET_DK))
    caches_t = [jnp.transpose(c, (0, 1, 3, 4, 5, 2)) for c in (cache_kv_w128, cache_kv_w512, cache_kv_w2048)]
    raw = dict(ffn1_gu=w_ffn1_gu, ffn1_down=w_ffn1_down, w_in=w_in, w_conv_out=w_conv_out, w_ret_out=w_ret_out,
               w_attn_out=w_attn_out, w_gate=w_gate, w_o=w_o, ffn2_gu=w_ffn2_gu, ffn2_down=w_ffn2_down,
               w_ple_gate=w_ple_gate, w_ple_proj=w_ple_proj)
    shared = dict(gains=norm_gain, w_conv=w_conv, b_conv=b_conv.reshape(depth, 1, CONV_DIM))
    w_next = {n: raw[n][0].astype(BF16) for n in W_FFN1}
    ple_p = p_prompt[:, 0]
    ple_s = p_sample[:, :, 0]
    conv_hist = jnp.swapaxes(state_conv, 1, 2)

    yp, ys = x_prompt[0], x_sample[:, 0]
    conv_p, conv_s, ret_p = [], [], []
    kv_p = [[] for _ in ATTN_GROUPS]
    ret_s, kv_s = None, [None] * len(ATTN_GROUPS)
    for l in range(depth):
        yp, cp, rp, kp, kv_s, compact, w, w_next = _prompt_layer(yp, ple_p, tabs_p, caches_t, kv_s, raw,
                                                                 dict(shared, **w_next), l)
        ys, cs, ret_s, kv_s = _sample_layer(ys, ple_s, tabs_s, gdec, conv_hist, state_ret, caches_t, ret_s, kv_s,
                                            compact, w, l)
        conv_p.append(cp)
        conv_s.append(cs)
        ret_p.append(rp)
        for gi in range(len(ATTN_GROUPS)):
            kv_p[gi].append(kp[gi])
    back = lambda c: jnp.transpose(c, (0, 1, 5, 2, 3, 4))
    return (yp[None], ys[:, None], jnp.stack(conv_p), jnp.stack(conv_s), jnp.stack(ret_p), ret_s,
            jnp.stack(kv_p[0]), back(kv_s[0]), jnp.stack(kv_p[1]), back(kv_s[1]),
            jnp.stack(kv_p[2]), back(kv_s[2]))
```

```python
import functools
import math

import jax
import jax.numpy as jnp
from jax import lax
from jax.experimental import pallas as pl
from jax.experimental.pallas import tpu as pltpu

F32 = jnp.float32
BF16 = jnp.bfloat16

EPS = 1e-6
PAST_LEN = 16384
CONV_DIM = 512
RET_HEADS = 4
RET_DK = 128
RET_THETA = 10000.0
ATTN_GROUPS = ((128, 1), (512, 4), (2048, 16))
DILATIONS = tuple(d for _, d in ATTN_GROUPS)
ATTN_HEADS = 4
HEAD_DIM = 64
ATTN_DIM = ATTN_HEADS * HEAD_DIM
ATTN_SCALE = HEAD_DIM ** -0.5
ROT_DIM = 16
ROPE_THETA = 500000.0
ROT_SPLIT = 128
ATTN_BLOCK = 128
ATTN_STEP = 2048
ATTN_GROUP = 4
DEC_BLOCK_SAMPLES = 8
NEG = -1e30
LANES = 128
MXU_WIDTH = 256
BF16_ROWS = 16
OFF_CONV = 0
OFF_RET = 3 * CONV_DIM
OFF_ATTN = OFF_RET + 4 * 512

VMEM_LIMIT_BYTES = 60000 * 1024
ROW_TILE = 512


def _params(*sem):
    return pltpu.CompilerParams(dimension_semantics=sem, vmem_limit_bytes=VMEM_LIMIT_BYTES)


def _const_spec(shape):
    zeros = (0,) * len(shape)
    return pl.BlockSpec(shape, lambda *_: zeros, pipeline_mode=pl.Buffered(1))


def _layer_spec(shape, layer):
    tail = tuple(shape[1:])
    idx = (layer,) + (0,) * len(tail)
    return pl.BlockSpec((None,) + tail, lambda *_: idx, pipeline_mode=pl.Buffered(1))


def _row_spec(tm, width):
    return pl.BlockSpec((tm, width), lambda i: (i, 0))


def _rms(x, g):
    return x * lax.rsqrt(jnp.mean(x * x, axis=-1, keepdims=True) + EPS) * g


def _dot(a, b):
    return jnp.dot(a, b, preferred_element_type=F32)


def _dot_nt(a, b):
    return lax.dot_general(a, b, (((1,), (1,)), ((), ())), preferred_element_type=F32)


def _silu(x):
    return x * jax.nn.sigmoid(x)


def _ff_chunks(d_ff):
    first = -(-(d_ff // 2) // MXU_WIDTH) * MXU_WIDTH
    return ((0, first), (first, d_ff - first)) if 0 < first < d_ff else ((0, d_ff),)


def _side_operands(n_args, n_outs, side):
    args, in_specs, out_shape, out_specs, aliases, layout = [], [], [], [], {}, []
    for job in side:
        for src, dst in job['aliases'].items():
            aliases[n_args + len(args) + src] = n_outs + len(out_shape) + dst
        layout.append((len(job['args']), len(job['out_shape']), job['fn']))
        args += job['args']
        in_specs += job['in_specs']
        out_shape += job['out_shape']
        out_specs += job['out_specs']
    return args, in_specs, out_shape, out_specs, aliases, tuple(layout)


def _run_side(layout, in_refs, out_refs):
    i = o = 0
    for n_in, n_out, fn in layout:
        fn(in_refs[i:i + n_in], out_refs[o:o + n_out])
        i += n_in
        o += n_out


def _shift_job(cache_t, new_cache, compact, layer, n_steps, kv, dil):
    _, b, _, nh, hd, window = cache_t.shape
    n_pos = window // dil
    bs = -(-b // n_steps)
    assert b % bs == 0
    n_blk = b // bs
    n_kv, kv_idx = (2, 0) if kv is None else (1, kv)
    blk = lambda i: jnp.minimum(i, n_blk - 1)
    spec = pl.BlockSpec((None, bs, n_kv, nh, hd, window), lambda i: (layer, blk(i), kv_idx, 0, 0, 0))
    cspec = pl.BlockSpec((bs, n_kv, nh * hd, n_pos), lambda i: (blk(i), kv_idx, 0, 0))
    pos = jnp.arange(window)[:, None] == dil * jnp.arange(n_pos)[None, :]
    pick = pos.astype(BF16)

    def fn(in_refs, out_refs):
        src, pick_ref = in_refs[:2]
        dst, cdst = out_refs
        for s in range(bs):
            for j in range(n_kv):
                for hh in range(nh):
                    t = src[s, j, hh]
                    dst[s, j, hh] = pltpu.roll(t, window - 1, 1)
                    tb = t.astype(BF16)
                    if dil > 1:
                        tb = _dot(tb, pick_ref[...]).astype(BF16)
                    cdst[s, j, hh * hd:(hh + 1) * hd, :] = tb

    job = dict(args=[cache_t, pick], in_specs=[spec, _const_spec(pick.shape)],
               out_shape=[jax.ShapeDtypeStruct(cache_t.shape, cache_t.dtype),
                          jax.ShapeDtypeStruct((b, 2, nh * hd, n_pos), BF16)],
               out_specs=[spec, cspec], aliases={}, fn=fn)
    for out_idx, buf in enumerate((new_cache, compact)):
        if buf is not None:
            job['aliases'][len(job['args'])] = out_idx
            job['args'].append(buf)
            job['in_specs'].append(pl.BlockSpec(memory_space=pl.ANY))
    return job


def _convert_job(weights, layer, n_steps):
    job = dict(args=[], in_specs=[], out_shape=[], out_specs=[], aliases={})
    for wt in weights:
        _, k, n = wt.shape
        n_band = max(c for c in range(1, n_steps + 1)
                     if n_steps % c == 0 and k % c == 0 and ((k // c) % BF16_ROWS == 0 or c == 1))
        band = lambda i, n_band=n_band: i * n_band // n_steps
        job['args'].append(wt)
        job['in_specs'].append(pl.BlockSpec((None, k // n_band, n), lambda i, band=band: (layer, band(i), 0)))
        job['out_shape'].append(jax.ShapeDtypeStruct((k, n), BF16))
        job['out_specs'].append(pl.BlockSpec((k // n_band, n), lambda i, band=band: (band(i), 0)))

    def fn(in_refs, out_refs):
        for src, dst in zip(in_refs, out_refs):
            dst[...] = src[...].astype(BF16)

    job['fn'] = fn
    return job


def _ffn_body(x_ref, g_ref, wgu_ref, wd_ref, *rest, pre, post, d_ff, with_ple, layout):
    n_ple = 3 if with_ple else 0
    n_side_in = sum(n_in for n_in, _, _ in layout)
    o_ref = rest[n_ple + n_side_in]
    _run_side(layout, rest[n_ple:n_ple + n_side_in], rest[n_ple + n_side_in + 1:])
    x = x_ref[...]
    h = _rms(x, g_ref[pre:pre + 1, :]).astype(BF16)
    tm = x.shape[0]
    halves = (slice(0, tm // 2), slice(tm // 2, tm)) if tm % (2 * MXU_WIDTH) == 0 else (slice(0, tm),)
    chunks = _ff_chunks(d_ff)
    acc = None
    for ci, (start, size) in enumerate(chunks):
        wg = wgu_ref[:, start:start + size]
        if ci == 0:
            gate = jnp.concatenate([_dot(h[r], wg) for r in halves], axis=0)
        else:
            gate = _dot(h, wg)
        up = _dot(h, wgu_ref[:, d_ff + start:d_ff + start + size])
        a = (_silu(gate) * up).astype(BF16)
        if ci < len(chunks) - 1:
            y = _dot(a, wd_ref[start:start + size, :])
            acc = y if acc is None else acc + y
    out = []
    for r in halves:
        y = _dot(a[r], wd_ref[start:start + size, :])
        y = y if acc is None else acc[r] + y
        out.append(x[r] + 0.5 * _rms(y, g_ref[post:post + 1, :]))
    if with_ple:
        p_ref, wg_ref, wp_ref = rest[:3]
        proj = _dot(p_ref[...].astype(BF16), wp_ref[...])
        for i, r in enumerate(halves):
            h = _rms(out[i], g_ref[6:7, :]).astype(BF16)
            gate = jax.nn.sigmoid(_dot(h, wg_ref[...]))
            out[i] = out[i] + _rms(gate * proj[r], g_ref[7:8, :])
    for r, v in zip(halves, out):
        o_ref[r, :] = v


def _ffn(x, gains, wgu, wd, layer, pre, post, tm, ple=None, side=()):
    m, d = x.shape
    d_ff = wd.shape[0]
    args = [x, gains, wgu, wd]
    in_specs = [_row_spec(tm, d), _layer_spec(gains.shape, layer), _const_spec(wgu.shape), _const_spec(wd.shape)]
    if ple is not None:
        p, wg, wp = ple
        args += [p, wg, wp]
        in_specs += [pl.BlockSpec((None, tm, p.shape[2]), lambda i: (layer, i, 0)),
                     _const_spec(wg.shape), _const_spec(wp.shape)]
    s_args, s_in, s_shape, s_out, aliases, layout = _side_operands(len(args), 1, side)
    body = functools.partial(_ffn_body, pre=pre, post=post, d_ff=d_ff, with_ple=ple is not None, layout=layout)
    outs = pl.pallas_call(
        body,
        grid=(m // tm,),
        in_specs=in_specs + s_in,
        out_specs=[_row_spec(tm, d)] + s_out,
        out_shape=[jax.ShapeDtypeStruct((m, d), F32)] + s_shape,
        input_output_aliases=aliases,
        compiler_params=_params("arbitrary"),
        name="ffn",
    )(*args, *s_args)
    return outs[0], outs[1:]


def _rot_ret(x, cos, sin):
    return x * cos + pltpu.roll(x, RET_DK // 2, 1) * sin


def _rot_attn(x, cos, sin_lo, sin_hi):
    return x * cos + pltpu.roll(x, LANES - ROT_DIM // 2, 1) * sin_lo + pltpu.roll(x, ROT_DIM // 2, 1) * sin_hi


def _to_streams(halves, out_ref, lane0, dil):
    n = halves[0].shape[0] // dil
    for s, v in enumerate(halves):
        lanes = slice(lane0 + s * LANES, lane0 + (s + 1) * LANES)
        if dil == 1:
            out_ref[0, :, lanes] = v
        else:
            out_ref[:, :, lanes] = jnp.swapaxes(v.reshape(n, dil, LANES), 0, 1)


def _proj_body(*refs, dils):
    x_ref, g_ref, w_ref, wc_ref, bc_ref, tab_ref, hist_ref = refs[:7]
    decode = dils is None
    if decode:
        cy_ref, ulast_ref, ret_ref = refs[7:10]
        attn_refs = refs[10:13]
    else:
        decay_ref, qw_ref, kw_ref, gch_ref = refs[7:11]
        cy_ref, ulast_ref, ret_ref, sout_ref = refs[11:15]
        attn_refs = refs[15:18]
        carry_ref, s_scr = refs[18:20]
    tab = lambda j: tab_ref[:, j * LANES:(j + 1) * LANES]

    x = x_ref[...]
    h = _rms(x, g_ref[2:3, :]).astype(BF16)

    def part(off, width):
        return _dot(h, w_ref[:, off:off + width])

    def mixer_c(gi):
        acos, alo, ahi = tab(2), tab(3), tab(4)
        base = OFF_ATTN + gi * 3 * ATTN_DIM
        zq = part(base, ATTN_DIM)
        zk = part(base + ATTN_DIM, ATTN_DIM)
        zv = part(base + 2 * ATTN_DIM, ATTN_DIM)
        halves = lambda z: [z[:, s * LANES:(s + 1) * LANES] for s in range(2)]
        q_h = [_rot_attn(v, acos, alo, ahi) for v in halves(zq)]
        k_h = [_rot_attn(v, acos, alo, ahi) for v in halves(zk)]
        out_ref = attn_refs[gi]
        for j, vals in enumerate((q_h, k_h, halves(zv))):
            if decode:
                for s, v in enumerate(vals):
                    out_ref[:, j * ATTN_DIM + s * LANES:j * ATTN_DIM + (s + 1) * LANES] = v
            else:
                _to_streams(vals, out_ref, j * ATTN_DIM, dils[gi])

    def mixer_a():
        rows = h.shape[0]
        halves = (slice(0, rows // 2), slice(rows // 2, rows)) if rows % (2 * MXU_WIDTH) == 0 else (slice(0, rows),)
        split = lambda off: jnp.concatenate([_dot(h[r], w_ref[:, off:off + CONV_DIM]) for r in halves], axis=0)
        b_gate = split(OFF_CONV)
        u = split(OFF_CONV + CONV_DIM) * split(OFF_CONV + 2 * CONV_DIM)
        if decode:
            u2 = hist_ref[0]
            u1 = hist_ref[1]
            ulast_ref[0] = u1
            ulast_ref[1] = u
        else:
            tm = u.shape[0]

            @pl.when(pl.program_id(0) == 0)
            def _():
                carry_ref[...] = hist_ref[...]

            carry = carry_ref[...]
            row = lax.broadcasted_iota(jnp.int32, u.shape, 0)
            u1 = jnp.where(row == 0, carry[7:8, :], pltpu.roll(u, 1, 0))
            u2 = jnp.where(row == 0, carry[6:7, :], jnp.where(row == 1, carry[7:8, :], pltpu.roll(u, 2, 0)))
            carry_ref[...] = u[tm - 8:tm, :]
            ulast_ref[...] = u[tm - 8:tm, :]
        y = bc_ref[...] + wc_ref[0:1, :] * u2
        y = y + wc_ref[1:2, :] * u1
        y = y + wc_ref[2:3, :] * u
        cy_ref[...] = (b_gate * y).astype(cy_ref.dtype)

    def mixer_b_inputs():
        tm = h.shape[0]
        halves = (slice(0, tm // 2), slice(tm // 2, tm)) if tm % (2 * MXU_WIDTH) == 0 else (slice(0, tm),)
        zq = jnp.concatenate([_dot(h[r], w_ref[:, OFF_RET:OFF_RET + 512]) for r in halves], axis=0)
        return [zq] + [part(OFF_RET + j * 512, 512) for j in range(1, 4)]

    def mixer_b(z, heads):
        zq, zk, zv, zg = z
        cos, sin = tab(0), tab(1)
        for hh in heads:
            cols = slice(hh * RET_DK, (hh + 1) * RET_DK)
            q = _rot_ret(zq[:, cols], cos, sin)
            k = _rot_ret(zk[:, cols], cos, sin) * (RET_DK ** -0.5)
            if decode:
                ret_ref[:, cols] = q
                ret_ref[:, 512 + hh * RET_DK:512 + (hh + 1) * RET_DK] = k
                ret_ref[:, 1024 + hh * RET_DK:1024 + (hh + 1) * RET_DK] = zv[:, cols]
                ret_ref[:, 1536 + hh * RET_DK:1536 + (hh + 1) * RET_DK] = zg[:, cols]
                continue
            v = zv[:, cols].astype(BF16)
            s = s_scr[hh]
            scores = _dot_nt(q.astype(BF16), k.astype(BF16)) * decay_ref[hh]
            o = _dot(scores.astype(BF16), v) + _dot((q * qw_ref[hh]).astype(BF16), s.astype(BF16))
            kv = _dot(jnp.transpose(k * kw_ref[hh]).astype(BF16), v)
            s_scr[hh] = s * gch_ref[hh] + kv
            ret_ref[:, cols] = (_head_norm(o) * _silu(zg[:, cols])).astype(ret_ref.dtype)

    if not decode:
        @pl.when(pl.program_id(0) == 0)
        def _():
            s_scr[...] = jnp.zeros_like(s_scr)

    z = mixer_b_inputs()
    mixer_c(2)
    mixer_b(z, (0, 1))
    mixer_c(1)
    mixer_b(z, (2, 3))
    mixer_c(0)
    mixer_a()

    if not decode:
        @pl.when(pl.program_id(0) == pl.num_programs(0) - 1)
        def _():
            sout_ref[...] = s_scr[...]


def _proj(x, gains, w_in, w_conv, b_conv, tabs, hist, layer, tm, decode):
    m, d = x.shape
    n = m // tm
    qkv = 3 * ATTN_DIM
    args = [x, gains, w_in, w_conv, b_conv, tabs, hist]
    state = (RET_HEADS, RET_DK, RET_DK)
    if decode:
        dils = None
        hist_spec = pl.BlockSpec((None, 2, tm, CONV_DIM), lambda i: (layer, 0, i, 0))
        ulast_spec = pl.BlockSpec((2, tm, CONV_DIM), lambda i: (0, i, 0))
        ulast_shape = jax.ShapeDtypeStruct((2, m, CONV_DIM), F32)
        ret_specs = [_row_spec(tm, 4 * 512)]
        ret_shapes = [jax.ShapeDtypeStruct((m, 4 * 512), F32)]
        attn_specs = [_row_spec(tm, qkv)] * 3
        attn_shapes = [jax.ShapeDtypeStruct((m, qkv), F32)] * 3
        scratch = []
    else:
        dils = DILATIONS
        hist_spec = _const_spec((8, CONV_DIM))
        ulast_spec = pl.BlockSpec((8, CONV_DIM), lambda i: (0, 0))
        ulast_shape = jax.ShapeDtypeStruct((8, CONV_DIM), F32)
        ret_specs = [_row_spec(tm, 512), pl.BlockSpec(state, lambda i: (0, 0, 0))]
        ret_shapes = [jax.ShapeDtypeStruct((m, 512), BF16), jax.ShapeDtypeStruct(state, F32)]
        attn_specs = [pl.BlockSpec((dl, tm // dl, qkv), lambda i: (0, i, 0)) for dl in dils]
        attn_shapes = [jax.ShapeDtypeStruct((dl, m // dl, qkv), F32) for dl in dils]
        scratch = [pltpu.VMEM((8, CONV_DIM), F32), pltpu.VMEM(state, F32)]
        args += list(_ret_tables(tm))
    in_specs = [
        _row_spec(tm, d), _layer_spec(gains.shape, layer), _const_spec(w_in.shape),
        _layer_spec(w_conv.shape, layer), _layer_spec(b_conv.shape, layer), _row_spec(tm, tabs.shape[1]), hist_spec,
    ] + [_const_spec(a.shape) for a in args[7:]]
    out_specs = [_row_spec(tm, CONV_DIM), ulast_spec] + ret_specs + attn_specs
    out_shape = [jax.ShapeDtypeStruct((m, CONV_DIM), BF16), ulast_shape] + ret_shapes + attn_shapes
    return pl.pallas_call(
        functools.partial(_proj_body, dils=dils),
        grid=(n,),
        in_specs=in_specs,
        out_specs=out_specs,
        out_shape=out_shape,
        scratch_shapes=scratch,
        compiler_params=_params("arbitrary"),
        name="proj",
    )(*args)


def _head_norm(o):
    mu = jnp.mean(o, axis=-1, keepdims=True)
    var = jnp.mean(jnp.square(o - mu), axis=-1, keepdims=True)
    return (o - mu) * lax.rsqrt(var + EPS)


def _ret_log_decay():
    return jnp.log1p(-jnp.exp2(-5.0 - jnp.arange(RET_HEADS, dtype=F32)))


def _ret_tables(chunk):
    i = jnp.arange(chunk, dtype=F32)
    log_g = _ret_log_decay()
    diff = i[:, None] - i[None, :]
    decay = jnp.where(diff[None] >= 0, jnp.exp(jnp.maximum(diff, 0.0)[None] * log_g[:, None, None]), 0.0)
    k_w = jnp.exp((chunk - 1 - i)[:, None] * log_g[None, :])
    q_w = jnp.exp((i + 1)[:, None] * log_g[None, :])
    g_chunk = jnp.exp(chunk * log_g)
    qw = jnp.broadcast_to(q_w.T[:, :, None], (RET_HEADS, chunk, RET_DK))
    kw = jnp.broadcast_to(k_w.T[:, :, None], (RET_HEADS, chunk, RET_DK))
    gch = jnp.broadcast_to(g_chunk[:, None, None], (RET_HEADS, RET_DK, RET_DK))
    return decay, qw, kw, gch


def _ret_dec_body(r_ref, gd_ref, s0_ref, *rest):
    y_ref, sout_ref = rest[-2:]
    for b in range(r_ref.shape[0]):
        for hh in range(RET_HEADS):
            part = lambda j: r_ref[b:b + 1, j * 512 + hh * RET_DK:j * 512 + (hh + 1) * RET_DK]
            q, k, v = part(0), part(1), part(2)
            gd = gd_ref[hh]
            s0 = s0_ref[b, hh]
            wide = (RET_DK, RET_DK)
            q_col = jnp.transpose(jnp.broadcast_to(q * gd, wide))
            k_col = jnp.transpose(jnp.broadcast_to(k, wide))
            o_inter = jnp.sum(q_col * s0, axis=0, keepdims=True)
            o_intra = jnp.sum(q * k, axis=-1, keepdims=True) * v
            sout_ref[b, hh] = s0 * gd + k_col * v
            o = o_intra + o_inter
            y_ref[b:b + 1, hh * RET_DK:(hh + 1) * RET_DK] = _head_norm(o) * _silu(part(3))


def _retention_decode(r, gdec, state, layer, prev_out):
    b = r.shape[0]
    bs = DEC_BLOCK_SAMPLES
    while b % bs:
        bs -= 1
    st = pl.BlockSpec((None, bs, RET_HEADS, RET_DK, RET_DK), lambda i: (layer, i, 0, 0, 0))
    args = [r, gdec, state]
    in_specs = [_row_spec(bs, r.shape[1]), _const_spec(gdec.shape), st]
    aliases = {}
    if prev_out is not None:
        args.append(prev_out)
        in_specs.append(pl.BlockSpec(memory_space=pl.ANY))
        aliases = {3: 1}
    return pl.pallas_call(
        _ret_dec_body,
        grid=(b // bs,),
        in_specs=in_specs,
        out_specs=[_row_spec(bs, 512), st],
        out_shape=[jax.ShapeDtypeStruct((b, 512), F32), jax.ShapeDtypeStruct(state.shape, F32)],
        input_output_aliases=aliases,
        compiler_params=_params("parallel"),
        name="retention_decode",
    )(*args)


def _attn_body(cur_ref, prev_ref, out_ref, *, n_sub):
    n = pl.program_id(1)
    qb = ATTN_BLOCK
    ii = lax.broadcasted_iota(jnp.int32, (2 * qb, 2 * qb), 0) & (qb - 1)
    jj = lax.broadcasted_iota(jnp.int32, (2 * qb, 2 * qb), 1)
    band = jnp.logical_and(jj >= ii, jj <= ii + qb)
    first = jnp.logical_and(band, jnp.logical_or(jj >= qb, n > 0))
    lo = lax.broadcasted_iota(jnp.int32, (qb, LANES), 1) < HEAD_DIM
    for hp in range(ATTN_HEADS // 2):
        cols = lambda j: slice(j * ATTN_DIM + hp * LANES, j * ATTN_DIM + (hp + 1) * LANES)
        kk = jnp.concatenate([prev_ref[:, cols(1)], cur_ref[:, cols(1)]], axis=0).astype(BF16)
        vv = jnp.concatenate([prev_ref[:, cols(2)], cur_ref[:, cols(2)]], axis=0).astype(BF16)
        for b0 in range(0, n_sub, ATTN_GROUP):
            blocks = range(b0, min(b0 + ATTN_GROUP, n_sub))
            scores = []
            for b in blocks:
                q2 = cur_ref[b * qb:(b + 1) * qb, cols(0)] * ATTN_SCALE
                zero = jnp.zeros_like(q2)
                qs = jnp.concatenate([jnp.where(lo, q2, zero), jnp.where(lo, zero, q2)], axis=0).astype(BF16)
                scores.append(_dot_nt(qs, kk[b * qb:(b + 2) * qb]))
            probs = []
            for b, s in zip(blocks, scores):
                s = jnp.where(first if b == 0 else band, s, NEG)
                mx = jnp.max(s, axis=-1, keepdims=True)
                p = jnp.exp(s - mx)
                den = jnp.sum(p, axis=-1, keepdims=True)
                probs.append((p.astype(BF16), den, mx + jnp.log(den)))
            for b, (p, den, lse) in zip(blocks, probs):
                rows = slice(b * qb, (b + 1) * qb)
                o = _dot(p, vv[b * qb:(b + 2) * qb]) / den
                out_ref[rows, cols(0)] = jnp.where(lo, o[:qb], o[qb:])
                out_ref[rows, cols(1)] = jnp.where(lo, lse[:qb], lse[qb:])


def _attention(qkv):
    dil, length, width = qkv.shape
    step = min(ATTN_STEP, length)
    n_sub = step // ATTN_BLOCK
    cur = pl.BlockSpec((None, step, width), lambda r, n: (r, n, 0))
    prev = pl.BlockSpec((None, ATTN_BLOCK, width), lambda r, n: (r, jnp.maximum(n * n_sub - 1, 0), 0))
    return pl.pallas_call(
        functools.partial(_attn_body, n_sub=n_sub),
        grid=(dil, length // step),
        in_specs=[cur, prev],
        out_specs=pl.BlockSpec((None, step, 2 * ATTN_DIM), lambda r, n: (r, n, 0)),
        out_shape=jax.ShapeDtypeStruct((dil, length, 2 * ATTN_DIM), F32),
        compiler_params=_params("parallel", "arbitrary"),
        name="attention",
    )(qkv, qkv)


def _attn_dec_body(qkv_ref, kc_ref, tail_ref, new_ref, out_ref, cout_ref):
    del new_ref
    rows = BF16_ROWS
    head_of_lane = lax.broadcasted_iota(jnp.int32, (rows, ATTN_DIM), 1) // HEAD_DIM
    own = head_of_lane == lax.broadcasted_iota(jnp.int32, (rows, ATTN_DIM), 0)
    last = lax.broadcasted_iota(jnp.int32, (HEAD_DIM, LANES), 1) == LANES - 1
    wide = lambda a: jnp.broadcast_to(a, (rows, ATTN_DIM))
    pick = lambda a: jnp.sum(jnp.where(own, a, 0.0), axis=0, keepdims=True)
    for b in range(qkv_ref.shape[0]):
        q, kn, vn = (qkv_ref[b:b + 1, j * ATTN_DIM:(j + 1) * ATTN_DIM] for j in range(3))
        q_rows = jnp.where(own, wide(q), 0.0)
        s = _dot(q_rows.astype(BF16), kc_ref[b, 0]) * ATTN_SCALE
        s_new = jnp.sum(q_rows * wide(kn), axis=-1, keepdims=True) * ATTN_SCALE
        mx = jnp.maximum(jnp.max(s, axis=-1, keepdims=True), s_new)
        p = jnp.exp(s - mx)
        p_new = jnp.exp(s_new - mx)
        den = jnp.sum(p, axis=-1, keepdims=True) + p_new
        o = (_dot_nt(p.astype(BF16), kc_ref[b, 1]) + p_new * wide(vn)) / den
        out_ref[b:b + 1, 0:ATTN_DIM] = pick(o)
        out_ref[b:b + 1, ATTN_DIM:2 * ATTN_DIM] = pick(wide(mx + jnp.log(den)))
        new_rows = jnp.concatenate([kn, vn], axis=1)
        new_cols = jnp.transpose(jnp.broadcast_to(new_rows, (LANES, 2 * ATTN_DIM)))
        for kv in range(2):
            for hh in range(ATTN_HEADS):
                col = new_cols[kv * ATTN_DIM + hh * HEAD_DIM:kv * ATTN_DIM + (hh + 1) * HEAD_DIM, :]
                cout_ref[b, kv, hh] = jnp.where(last, col, pltpu.roll(tail_ref[b, kv, hh], LANES - 1, 1))


def _attention_decode(qkv, compact, cache_t, new_cache, layer):
    b = qkv.shape[0]
    window = cache_t.shape[-1]
    bs = DEC_BLOCK_SAMPLES
    while b % bs:
        bs -= 1
    kc = pl.BlockSpec((bs,) + compact.shape[1:], lambda i: (i, 0, 0, 0))
    tail = pl.BlockSpec((None, bs, 2, ATTN_HEADS, HEAD_DIM, LANES),
                        lambda i: (layer, i, 0, 0, 0, window // LANES - 1))
    out, cnew = pl.pallas_call(
        _attn_dec_body,
        grid=(b // bs,),
        in_specs=[_row_spec(bs, qkv.shape[1]), kc, tail, pl.BlockSpec(memory_space=pl.ANY)],
        out_specs=[_row_spec(bs, 2 * ATTN_DIM), tail],
        out_shape=[jax.ShapeDtypeStruct((b, 2 * ATTN_DIM), F32), jax.ShapeDtypeStruct(cache_t.shape, F32)],
        input_output_aliases={3: 1},
        compiler_params=_params("parallel"),
        name="attention_decode",
    )(qkv, compact, cache_t, new_cache)
    return out[None], cnew


def _from_streams(in_ref, lane0, scr_ref, dil):
    if dil == 1:
        return in_ref[0, :, lane0:lane0 + ATTN_DIM]
    n = in_ref.shape[1]
    for r in range(dil):
        for s in range(2):
            scr_ref[s, pl.ds(r, n, stride=dil), :] = in_ref[r, :, lane0 + s * LANES:lane0 + (s + 1) * LANES]
    return jnp.concatenate([scr_ref[0], scr_ref[1]], axis=1)


def _merge_body(x_ref, cy_ref, ry_ref, a0_ref, a1_ref, a2_ref, g_ref,
                wg_ref, wc_ref, wr_ref, wa_ref, wo_ref, *rest, dils, layout):
    n_side_in = sum(n_in for n_in, _, _ in layout)
    n_side_out = sum(n_out for _, n_out, _ in layout)
    out_ref = rest[n_side_in]
    scratch = rest[n_side_in + 1 + n_side_out:]
    _run_side(layout, rest[:n_side_in], rest[n_side_in + 1:n_side_in + 1 + n_side_out])
    x = x_ref[...]
    h = _rms(x, g_ref[2:3, :]).astype(BF16)
    scr = scratch[0] if scratch else None
    d = x.shape[1]
    tm = x.shape[0]
    halves = (slice(0, tm // 2), slice(tm // 2, tm)) if tm % (2 * MXU_WIDTH) == 0 else (slice(0, tm),)
    gate = jnp.concatenate([_dot(h[r], wg_ref[:, 0:d]) for r in halves], axis=0)
    merged = jax.nn.sigmoid(gate) * _dot(cy_ref[...].astype(BF16), wc_ref[...])
    merged = merged + jax.nn.sigmoid(_dot(h, wg_ref[:, d:2 * d])) * _dot(ry_ref[...].astype(BF16), wr_ref[...])
    attn_gate = jax.nn.sigmoid(_dot(h, wg_ref[:, 2 * d:3 * d]))
    vals = []
    for j in range(6):
        ref = (a0_ref, a1_ref, a2_ref)[j // 2]
        vals.append(_from_streams(ref, (j % 2) * ATTN_DIM, None if scr is None else scr.at[j], dils[j // 2]))
    o0, l0, o1, l1, o2, l2 = vals
    mx = jnp.maximum(jnp.maximum(l0, l1), l2)
    e0, e1, e2 = jnp.exp(l0 - mx), jnp.exp(l1 - mx), jnp.exp(l2 - mx)
    den = e0 + e1 + e2
    attn_y = ((e0 / den) * o0 + (e1 / den) * o1 + (e2 / den) * o2).astype(BF16)
    merged = (merged + attn_gate * _dot(attn_y, wa_ref[...])).astype(BF16)
    for r in halves:
        out_ref[r, :] = x[r] + _rms(_dot(merged[r], wo_ref[...]), g_ref[3:4, :])


def _merge(x, cy, ry, attn, dils, gains, wg, wc, wr, wa, wo, layer, tm, side=()):
    m, d = x.shape
    a_specs = [pl.BlockSpec((dl, tm // dl, 2 * ATTN_DIM), lambda i: (0, i, 0)) for dl in dils]
    scratch = [pltpu.VMEM((6, 2, tm, LANES), F32)] if max(dils) > 1 else []
    args = [x, cy, ry, *attn, gains, wg, wc, wr, wa, wo]
    s_args, s_in, s_shape, s_out, aliases, layout = _side_operands(len(args), 1, side)
    outs = pl.pallas_call(
        functools.partial(_merge_body, dils=dils, layout=layout),
        grid=(m // tm,),
        in_specs=[_row_spec(tm, d), _row_spec(tm, CONV_DIM), _row_spec(tm, 512)] + a_specs
        + [_layer_spec(gains.shape, layer)] + [_const_spec(a.shape) for a in (wg, wc, wr, wa, wo)] + s_in,
        out_specs=[_row_spec(tm, d)] + s_out,
        out_shape=[jax.ShapeDtypeStruct((m, d), F32)] + s_shape,
        scratch_shapes=scratch,
        input_output_aliases=aliases,
        compiler_params=_params("arbitrary"),
        name="merge",
    )(*args, *s_args)
    return outs[0], outs[1:]


def _cos_sin(start, count, freq, split):
    if split is None or count % split:
        ang = (start + jnp.arange(count)).astype(F32)[:, None] * freq[None, :]
        return jnp.cos(ang), jnp.sin(ang)
    base = (start + split * jnp.arange(count // split)).astype(F32)[:, None, None] * freq
    off = jnp.arange(split).astype(F32)[None, :, None] * freq
    cb, sb, co, so = jnp.cos(base), jnp.sin(base), jnp.cos(off), jnp.sin(off)
    return (cb * co - sb * so).reshape(count, LANES), (sb * co + cb * so).reshape(count, LANES)


def _rotary_tables(start, count, split=None):
    lane = jnp.arange(LANES)
    half = RET_DK // 2
    freq = jnp.exp(-(lane % half).astype(F32) * (math.log(RET_THETA) / half))
    cos, sin = _cos_sin(start, count, freq, split)
    rc, rs = cos, jnp.where(lane < half, -sin, sin)
    dim = lane % HEAD_DIM
    half = ROT_DIM // 2
    freq = jnp.exp(-(dim % half).astype(F32) * (math.log(ROPE_THETA) / half))
    cos, sin = _cos_sin(start, count, freq, split)
    ac = jnp.where(dim < ROT_DIM, cos, 1.0)
    alo = jnp.where(dim < half, -sin, 0.0)
    ahi = jnp.where(jnp.logical_and(dim >= half, dim < ROT_DIM), sin, 0.0)
    return rc, rs, ac, alo, ahi


def _finish(x, ple, cy, ry, attn, dils, w, layer, tm, merge_side=(), ffn_side=()):
    x, merge_out = _merge(x, cy, ry, attn, dils, w['gains'], w['w_gate'], w['w_conv_out'], w['w_ret_out'],
                          w['w_attn_out'], w['w_o'], layer, tm, side=merge_side)
    x, ffn_out = _ffn(x, w['gains'], w['ffn2_gu'], w['ffn2_down'], layer, 4, 5, tm,
                      ple=(ple, w['w_ple_gate'], w['w_ple_proj']), side=ffn_side)
    return x, merge_out, ffn_out


W_FFN1 = ('ffn1_gu', 'ffn1_down')
W_EARLY = W_FFN1 + ('w_in',)
W_LATE = ('w_gate', 'w_conv_out', 'w_ret_out', 'w_attn_out', 'w_o', 'ffn2_gu', 'ffn2_down', 'w_ple_gate',
          'w_ple_proj')


def _prompt_layer(x, ple, tabs, caches_t, new_kv, raw, w, layer):
    depth = raw['w_in'].shape[0]
    s = x.shape[0]
    tm = min(ROW_TILE, s)
    steps = s // tm
    wide_dil = DILATIONS[2]
    todo = tuple(n for n in W_EARLY[2:] + W_LATE if n not in w)
    side = [_shift_job(caches_t[2], new_kv[2], None, layer, steps, 0, wide_dil)]
    if todo:
        side.append(_convert_job([raw[n] for n in todo], layer, steps))
    x, outs = _ffn(x, w['gains'], w['ffn1_gu'], w['ffn1_down'], layer, 0, 1, tm, side=side)
    wide, wide_c = outs[:2]
    w = dict(w, **dict(zip(todo, outs[2:])))
    hist = jnp.zeros((8, CONV_DIM), F32)
    outs = _proj(x, w['gains'], w['w_in'], w['w_conv'], w['b_conv'], tabs, hist, layer, tm, decode=False)
    cy, ulast, ry, ret_state = outs[:4]
    attn, kv = [], []
    for gi, (window, dil) in enumerate(ATTN_GROUPS):
        qkv = outs[4 + gi]
        attn.append(_attention(qkv))
        keep = min(window, s) // dil
        tail = jnp.swapaxes(qkv[:, s // dil - keep:, ATTN_DIM:], 0, 1)
        kv.append(tail.reshape(1, keep * dil, 2, ATTN_HEADS, HEAD_DIM))
    merge_side = [_shift_job(caches_t[gi], new_kv[gi], None, layer, steps, None, DILATIONS[gi]) for gi in (0, 1)]
    ffn_side = [_shift_job(caches_t[2], wide, wide_c, layer, steps, 1, wide_dil)]
    if layer + 1 < depth:
        merge_side.append(_convert_job([raw[n] for n in W_EARLY], layer + 1, steps))
        ffn_side.append(_convert_job([raw[n] for n in W_LATE], layer + 1, steps))
    x, m_out, f_out = _finish(x, ple, cy, ry, attn, DILATIONS, w, layer, tm, merge_side, ffn_side)
    w_next = dict(zip(W_EARLY, m_out[4:]), **dict(zip(W_LATE, f_out[2:])))
    new_kv = [m_out[0], m_out[2], f_out[0]]
    compact = [m_out[1], m_out[3], f_out[1]]
    return x, ulast[6:8][None], ret_state[None], kv, new_kv, compact, w, w_next


def _sample_layer(x, ple, tabs, gdec, conv_hist, ret_state, caches_t, prev_ret, new_kv, compact, w, layer):
    b = x.shape[0]
    x, _ = _ffn(x, w['gains'], w['ffn1_gu'], w['ffn1_down'], layer, 0, 1, b)
    outs = _proj(x, w['gains'], w['w_in'], w['w_conv'], w['b_conv'], tabs, conv_hist, layer, b, decode=True)
    cy, ulast, ret_in = outs[:3]
    ry, ret_new = _retention_decode(ret_in, gdec, ret_state, layer, prev_ret)
    attn, kv = [], []
    for gi in range(len(ATTN_GROUPS)):
        o_lse, cnew = _attention_decode(outs[3 + gi], compact[gi], caches_t[gi], new_kv[gi], layer)
        attn.append(o_lse)
        kv.append(cnew)
    x, _, _ = _finish(x, ple, cy, ry, attn, (1, 1, 1), w, layer, b)
    return x, jnp.swapaxes(ulast, 0, 1), ret_new, kv


def kernel(x_prompt, x_sample, state_conv, state_ret, cache_kv_w128, cache_kv_w512, cache_kv_w2048, p_prompt, p_sample, norm_gain, w_ffn1_gu, w_ffn1_down, w_in, w_conv, b_conv, w_conv_out, w_ret_out, w_attn_out, w_gate, w_o, w_ffn2_gu, w_ffn2_down, w_ple_gate, w_ple_proj):
    depth = norm_gain.shape[0]
    seq = x_prompt.shape[1]
    nb = x_sample.shape[0]
    assert x_prompt.shape[0] == 1 and x_sample.shape[1] == 1

    tabs_p = jnp.concatenate(_rotary_tables(0, seq, split=ROT_SPLIT), axis=1)
    tabs_s = jnp.broadcast_to(jnp.concatenate(_rotary_tables(PAST_LEN, 1), axis=1), (nb, 5 * LANES))
    gdec = jnp.broadcast_to(jnp.exp(_ret_log_decay())[:, None, None], (RET_HEADS, 1, RET_DK))
    caches_t = [jnp.transpose(c, (0, 1, 3, 4, 5, 2)) for c in (cache_kv_w128, cache_kv_w512, cache_kv_w2048)]
    raw = dict(ffn1_gu=w_ffn1_gu, ffn1_down=w_ffn1_down, w_in=w_in, w_conv_out=w_conv_out, w_ret_out=w_ret_out,
               w_attn_out=w_attn_out, w_gate=w_gate, w_o=w_o, ffn2_gu=w_ffn2_gu, ffn2_down=w_ffn2_down,
               w_ple_gate=w_ple_gate, w_ple_proj=w_ple_proj)
    shared = dict(gains=norm_gain, w_conv=w_conv, b_conv=b_conv.reshape(depth, 1, CONV_DIM))
    w_next = {n: raw[n][0].astype(BF16) for n in W_FFN1}
    ple_p = p_prompt[:, 0]
    ple_s = p_sample[:, :, 0]
    conv_hist = jnp.swapaxes(state_conv, 1, 2)

    yp, ys = x_prompt[0], x_sample[:, 0]
    conv_p, conv_s, ret_p = [], [], []
    kv_p = [[] for _ in ATTN_GROUPS]
    ret_s, kv_s = None, [None] * len(ATTN_GROUPS)
    for l in range(depth):
        yp, cp, rp, kp, kv_s, compact, w, w_next = _prompt_layer(yp, ple_p, tabs_p, caches_t, kv_s, raw,
                                                                 dict(shared, **w_next), l)
        ys, cs, ret_s, kv_s = _sample_layer(ys, ple_s, tabs_s, gdec, conv_hist, state_ret, caches_t, ret_s, kv_s,
                                            compact, w, l)
        conv_p.append(cp)
        conv_s.append(cs)
        ret_p.append(rp)
        for gi in range(len(ATTN_GROUPS)):
            kv_p[gi].append(kp[gi])
    back = lambda c: jnp.transpose(c, (0, 1, 5, 2, 3, 4))
    return (yp[None], ys[:, None], jnp.stack(conv_p), jnp.stack(conv_s), jnp.stack(ret_p), ret_s,
            jnp.stack(kv_p[0]), back(kv_s[0]), jnp.stack(kv_p[1]), back(kv_s[1]),
            jnp.stack(kv_p[2]), back(kv_s[2]))
```

```python
import functools
import math

import jax
import jax.numpy as jnp
from jax import lax
from jax.experimental import pallas as pl
from jax.experimental.pallas import tpu as pltpu

F32 = jnp.float32
BF16 = jnp.bfloat16

EPS = 1e-6
PAST_LEN = 16384
CONV_DIM = 512
RET_HEADS = 4
RET_DK = 128
RET_THETA = 10000.0
ATTN_GROUPS = ((128, 1), (512, 4), (2048, 16))
DILATIONS = tuple(d for _, d in ATTN_GROUPS)
ATTN_HEADS = 4
HEAD_DIM = 64
ATTN_DIM = ATTN_HEADS * HEAD_DIM
ATTN_SCALE = HEAD_DIM ** -0.5
ROT_DIM = 16
ROPE_THETA = 500000.0
ROT_SPLIT = 128
ATTN_BLOCK = 128
ATTN_STEP = 2048
ATTN_GROUP = 4
DEC_BLOCK_SAMPLES = 8
NEG = -1e30
LANES = 128
MXU_WIDTH = 256
BF16_ROWS = 16
OFF_CONV = 0
OFF_RET = 3 * CONV_DIM
OFF_ATTN = OFF_RET + 4 * 512

VMEM_LIMIT_BYTES = 60000 * 1024
ROW_TILE = 512


def _params(*sem):
    return pltpu.CompilerParams(dimension_semantics=sem, vmem_limit_bytes=VMEM_LIMIT_BYTES)


def _const_spec(shape):
    zeros = (0,) * len(shape)
    return pl.BlockSpec(shape, lambda *_: zeros, pipeline_mode=pl.Buffered(1))


def _layer_spec(shape, layer):
    tail = tuple(shape[1:])
    idx = (layer,) + (0,) * len(tail)
    return pl.BlockSpec((None,) + tail, lambda *_: idx, pipeline_mode=pl.Buffered(1))


def _row_spec(tm, width):
    return pl.BlockSpec((tm, width), lambda i: (i, 0))


def _rms(x, g):
    return x * lax.rsqrt(jnp.mean(x * x, axis=-1, keepdims=True) + EPS) * g


def _dot(a, b):
    return jnp.dot(a, b, preferred_element_type=F32)


def _dot_nt(a, b):
    return lax.dot_general(a, b, (((1,), (1,)), ((), ())), preferred_element_type=F32)


def _silu(x):
    return x * jax.nn.sigmoid(x)


def _ff_chunks(d_ff):
    first = -(-(d_ff // 2) // MXU_WIDTH) * MXU_WIDTH
    return ((0, first), (first, d_ff - first)) if 0 < first < d_ff else ((0, d_ff),)


def _side_operands(n_args, n_outs, side):
    args, in_specs, out_shape, out_specs, aliases, layout = [], [], [], [], {}, []
    for job in side:
        for src, dst in job['aliases'].items():
            aliases[n_args + len(args) + src] = n_outs + len(out_shape) + dst
        layout.append((len(job['args']), len(job['out_shape']), job['fn']))
        args += job['args']
        in_specs += job['in_specs']
        out_shape += job['out_shape']
        out_specs += job['out_specs']
    return args, in_specs, out_shape, out_specs, aliases, tuple(layout)


def _run_side(layout, in_refs, out_refs):
    i = o = 0
    for n_in, n_out, fn in layout:
        fn(in_refs[i:i + n_in], out_refs[o:o + n_out])
        i += n_in
        o += n_out


def _shift_job(cache_t, new_cache, compact, layer, n_steps, kv, dil):
    _, b, _, nh, hd, window = cache_t.shape
    n_pos = window // dil
    bs = -(-b // n_steps)
    assert b % bs == 0
    n_blk = b // bs
    n_kv, kv_idx = (2, 0) if kv is None else (1, kv)
    blk = lambda i: jnp.minimum(i, n_blk - 1)
    spec = pl.BlockSpec((None, bs, n_kv, nh, hd, window), lambda i: (layer, blk(i), kv_idx, 0, 0, 0))
    cspec = pl.BlockSpec((bs, n_kv, nh * hd, n_pos), lambda i: (blk(i), kv_idx, 0, 0))
    pos = jnp.arange(window)[:, None] == dil * jnp.arange(n_pos)[None, :]
    pick = pos.astype(BF16)

    def fn(in_refs, out_refs):
        src, pick_ref = in_refs[:2]
        dst, cdst = out_refs
        for s in range(bs):
            for j in range(n_kv):
                for hh in range(nh):
                    t = src[s, j, hh]
                    dst[s, j, hh] = pltpu.roll(t, window - 1, 1)
                    tb = t.astype(BF16)
                    if dil > 1:
                        tb = _dot(tb, pick_ref[...]).astype(BF16)
                    cdst[s, j, hh * hd:(hh + 1) * hd, :] = tb

    job = dict(args=[cache_t, pick], in_specs=[spec, _const_spec(pick.shape)],
               out_shape=[jax.ShapeDtypeStruct(cache_t.shape, cache_t.dtype),
                          jax.ShapeDtypeStruct((b, 2, nh * hd, n_pos), BF16)],
               out_specs=[spec, cspec], aliases={}, fn=fn)
    for out_idx, buf in enumerate((new_cache, compact)):
        if buf is not None:
            job['aliases'][len(job['args'])] = out_idx
            job['args'].append(buf)
            job['in_specs'].append(pl.BlockSpec(memory_space=pl.ANY))
    return job


def _convert_job(weights, layer, n_steps):
    job = dict(args=[], in_specs=[], out_shape=[], out_specs=[], aliases={})
    for wt in weights:
        _, k, n = wt.shape
        n_band = max(c for c in range(1, n_steps + 1)
                     if n_steps % c == 0 and k % c == 0 and ((k // c) % BF16_ROWS == 0 or c == 1))
        band = lambda i, n_band=n_band: i * n_band // n_steps
        job['args'].append(wt)
        job['in_specs'].append(pl.BlockSpec((None, k // n_band, n), lambda i, band=band: (layer, band(i), 0)))
        job['out_shape'].append(jax.ShapeDtypeStruct((k, n), BF16))
        job['out_specs'].append(pl.BlockSpec((k // n_band, n), lambda i, band=band: (band(i), 0)))

    def fn(in_refs, out_refs):
        for src, dst in zip(in_refs, out_refs):
            dst[...] = src[...].astype(BF16)

    job['fn'] = fn
    return job


def _ffn_body(x_ref, g_ref, wgu_ref, wd_ref, *rest, pre, post, d_ff, with_ple, layout):
    n_ple = 3 if with_ple else 0
    n_side_in = sum(n_in for n_in, _, _ in layout)
    o_ref = rest[n_ple + n_side_in]
    _run_side(layout, rest[n_ple:n_ple + n_side_in], rest[n_ple + n_side_in + 1:])
    x = x_ref[...]
    h = _rms(x, g_ref[pre:pre + 1, :]).astype(BF16)
    tm = x.shape[0]
    halves = (slice(0, tm // 2), slice(tm // 2, tm)) if tm % (2 * MXU_WIDTH) == 0 else (slice(0, tm),)
    chunks = _ff_chunks(d_ff)
    acc = None
    for ci, (start, size) in enumerate(chunks):
        wg = wgu_ref[:, start:start + size]
        if ci == 0:
            gate = jnp.concatenate([_dot(h[r], wg) for r in halves], axis=0)
        else:
            gate = _dot(h, wg)
        up = _dot(h, wgu_ref[:, d_ff + start:d_ff + start + size])
        a = (_silu(gate) * up).astype(BF16)
        if ci < len(chunks) - 1:
            y = _dot(a, wd_ref[start:start + size, :])
            acc = y if acc is None else acc + y
    out = []
    for r in halves:
        y = _dot(a[r], wd_ref[start:start + size, :])
        y = y if acc is None else acc[r] + y
        out.append(x[r] + 0.5 * _rms(y, g_ref[post:post + 1, :]))
    if with_ple:
        p_ref, wg_ref, wp_ref = rest[:3]
        proj = _dot(p_ref[...].astype(BF16), wp_ref[...])
        for i, r in enumerate(halves):
            h = _rms(out[i], g_ref[6:7, :]).astype(BF16)
            gate = jax.nn.sigmoid(_dot(h, wg_ref[...]))
            out[i] = out[i] + _rms(gate * proj[r], g_ref[7:8, :])
    for r, v in zip(halves, out):
        o_ref[r, :] = v


def _ffn(x, gains, wgu, wd, layer, pre, post, tm, ple=None, side=()):
    m, d = x.shape
    d_ff = wd.shape[0]
    args = [x, gains, wgu, wd]
    in_specs = [_row_spec(tm, d), _layer_spec(gains.shape, layer), _const_spec(wgu.shape), _const_spec(wd.shape)]
    if ple is not None:
        p, wg, wp = ple
        args += [p, wg, wp]
        in_specs += [pl.BlockSpec((None, tm, p.shape[2]), lambda i: (layer, i, 0)),
                     _const_spec(wg.shape), _const_spec(wp.shape)]
    s_args, s_in, s_shape, s_out, aliases, layout = _side_operands(len(args), 1, side)
    body = functools.partial(_ffn_body, pre=pre, post=post, d_ff=d_ff, with_ple=ple is not None, layout=layout)
    outs = pl.pallas_call(
        body,
        grid=(m // tm,),
        in_specs=in_specs + s_in,
        out_specs=[_row_spec(tm, d)] + s_out,
        out_shape=[jax.ShapeDtypeStruct((m, d), F32)] + s_shape,
        input_output_aliases=aliases,
        compiler_params=_params("arbitrary"),
        name="ffn",
    )(*args, *s_args)
    return outs[0], outs[1:]


def _rot_ret(x, cos, sin):
    return x * cos + pltpu.roll(x, RET_DK // 2, 1) * sin


def _rot_attn(x, cos, sin_lo, sin_hi):
    return x * cos + pltpu.roll(x, LANES - ROT_DIM // 2, 1) * sin_lo + pltpu.roll(x, ROT_DIM // 2, 1) * sin_hi


def _to_streams(halves, out_ref, lane0, dil):
    n = halves[0].shape[0] // dil
    for s, v in enumerate(halves):
        lanes = slice(lane0 + s * LANES, lane0 + (s + 1) * LANES)
        if dil == 1:
            out_ref[0, :, lanes] = v
        else:
            out_ref[:, :, lanes] = jnp.swapaxes(v.reshape(n, dil, LANES), 0, 1)


def _proj_body(*refs, dils):
    x_ref, g_ref, w_ref, wc_ref, bc_ref, tab_ref, hist_ref = refs[:7]
    decode = dils is None
    if decode:
        cy_ref, ulast_ref, ret_ref = refs[7:10]
        attn_refs = refs[10:13]
    else:
        decay_ref, qw_ref, kw_ref, gch_ref = refs[7:11]
        cy_ref, ulast_ref, ret_ref, sout_ref = refs[11:15]
        attn_refs = refs[15:18]
        carry_ref, s_scr = refs[18:20]
    tab = lambda j: tab_ref[:, j * LANES:(j + 1) * LANES]

    x = x_ref[...]
    h = _rms(x, g_ref[2:3, :]).astype(BF16)

    def part(off, width):
        return _dot(h, w_ref[:, off:off + width])

    def mixer_c(gi):
        acos, alo, ahi = tab(2), tab(3), tab(4)
        base = OFF_ATTN + gi * 3 * ATTN_DIM
        zq = part(base, ATTN_DIM)
        zk = part(base + ATTN_DIM, ATTN_DIM)
        zv = part(base + 2 * ATTN_DIM, ATTN_DIM)
        halves = lambda z: [z[:, s * LANES:(s + 1) * LANES] for s in range(2)]
        q_h = [_rot_attn(v, acos, alo, ahi) for v in halves(zq)]
        k_h = [_rot_attn(v, acos, alo, ahi) for v in halves(zk)]
        out_ref = attn_refs[gi]
        for j, vals in enumerate((q_h, k_h, halves(zv))):
            if decode:
                for s, v in enumerate(vals):
                    out_ref[:, j * ATTN_DIM + s * LANES:j * ATTN_DIM + (s + 1) * LANES] = v
            else:
                _to_streams(vals, out_ref, j * ATTN_DIM, dils[gi])

    def mixer_a():
        rows = h.shape[0]
        halves = (slice(0, rows // 2), slice(rows // 2, rows)) if rows % (2 * MXU_WIDTH) == 0 else (slice(0, rows),)
        split = lambda off: jnp.concatenate([_dot(h[r], w_ref[:, off:off + CONV_DIM]) for r in halves], axis=0)
        b_gate = split(OFF_CONV)
        u = split(OFF_CONV + CONV_DIM) * split(OFF_CONV + 2 * CONV_DIM)
        if decode:
            u2 = hist_ref[0]
            u1 = hist_ref[1]
            ulast_ref[0] = u1
            ulast_ref[1] = u
        else:
            tm = u.shape[0]

            @pl.when(pl.program_id(0) == 0)
            def _():
                carry_ref[...] = hist_ref[...]

            carry = carry_ref[...]
            row = lax.broadcasted_iota(jnp.int32, u.shape, 0)
            u1 = jnp.where(row == 0, carry[7:8, :], pltpu.roll(u, 1, 0))
            u2 = jnp.where(row == 0, carry[6:7, :], jnp.where(row == 1, carry[7:8, :], pltpu.roll(u, 2, 0)))
            carry_ref[...] = u[tm - 8:tm, :]
            ulast_ref[...] = u[tm - 8:tm, :]
        y = bc_ref[...] + wc_ref[0:1, :] * u2
        y = y + wc_ref[1:2, :] * u1
        y = y + wc_ref[2:3, :] * u
        cy_ref[...] = (b_gate * y).astype(cy_ref.dtype)

    def mixer_b_inputs():
        tm = h.shape[0]
        halves = (slice(0, tm // 2), slice(tm // 2, tm)) if tm % (2 * MXU_WIDTH) == 0 else (slice(0, tm),)
        zq = jnp.concatenate([_dot(h[r], w_ref[:, OFF_RET:OFF_RET + 512]) for r in halves], axis=0)
        return [zq] + [part(OFF_RET + j * 512, 512) for j in range(1, 4)]

    def mixer_b(z, heads):
        zq, zk, zv, zg = z
        cos, sin = tab(0), tab(1)
        for hh in heads:
            cols = slice(hh * RET_DK, (hh + 1) * RET_DK)
            q = _rot_ret(zq[:, cols], cos, sin)
            k = _rot_ret(zk[:, cols], cos, sin) * (RET_DK ** -0.5)
            if decode:
                ret_ref[:, cols] = q
                ret_ref[:, 512 + hh * RET_DK:512 + (hh + 1) * RET_DK] = k
                ret_ref[:, 1024 + hh * RET_DK:1024 + (hh + 1) * RET_DK] = zv[:, cols]
                ret_ref[:, 1536 + hh * RET_DK:1536 + (hh + 1) * RET_DK] = zg[:, cols]
                continue
            v = zv[:, cols].astype(BF16)
            s = s_scr[hh]
            scores = _dot_nt(q.astype(BF16), k.astype(BF16)) * decay_ref[hh]
            o = _dot(scores.astype(BF16), v) + _dot((q * qw_ref[hh]).astype(BF16), s.astype(BF16))
            kv = _dot(jnp.transpose(k * kw_ref[hh]).astype(BF16), v)
            s_scr[hh] = s * gch_ref[hh] + kv
            ret_ref[:, cols] = (_head_norm(o) * _silu(zg[:, cols])).astype(ret_ref.dtype)

    if not decode:
        @pl.when(pl.program_id(0) == 0)
        def _():
            s_scr[...] = jnp.zeros_like(s_scr)

    z = mixer_b_inputs()
    mixer_c(2)
    mixer_b(z, (0, 1))
    mixer_c(1)
    mixer_b(z, (2, 3))
    mixer_c(0)
    mixer_a()

    if not decode:
        @pl.when(pl.program_id(0) == pl.num_programs(0) - 1)
        def _():
            sout_ref[...] = s_scr[...]


def _proj(x, gains, w_in, w_conv, b_conv, tabs, hist, layer, tm, decode):
    m, d = x.shape
    n = m // tm
    qkv = 3 * ATTN_DIM
    args = [x, gains, w_in, w_conv, b_conv, tabs, hist]
    state = (RET_HEADS, RET_DK, RET_DK)
    if decode:
        dils = None
        hist_spec = pl.BlockSpec((None, 2, tm, CONV_DIM), lambda i: (layer, 0, i, 0))
        ulast_spec = pl.BlockSpec((2, tm, CONV_DIM), lambda i: (0, i, 0))
        ulast_shape = jax.ShapeDtypeStruct((2, m, CONV_DIM), F32)
        ret_specs = [_row_spec(tm, 4 * 512)]
        ret_shapes = [jax.ShapeDtypeStruct((m, 4 * 512), F32)]
        attn_specs = [_row_spec(tm, qkv)] * 3
        attn_shapes = [jax.ShapeDtypeStruct((m, qkv), F32)] * 3
        scratch = []
    else:
        dils = DILATIONS
        hist_spec = _const_spec((8, CONV_DIM))
        ulast_spec = pl.BlockSpec((8, CONV_DIM), lambda i: (0, 0))
        ulast_shape = jax.ShapeDtypeStruct((8, CONV_DIM), F32)
        ret_specs = [_row_spec(tm, 512), pl.BlockSpec(state, lambda i: (0, 0, 0))]
        ret_shapes = [jax.ShapeDtypeStruct((m, 512), BF16), jax.ShapeDtypeStruct(state, F32)]
        attn_specs = [pl.BlockSpec((dl, tm // dl, qkv), lambda i: (0, i, 0)) for dl in dils]
        attn_shapes = [jax.ShapeDtypeStruct((dl, m // dl, qkv), F32) for dl in dils]
        scratch = [pltpu.VMEM((8, CONV_DIM), F32), pltpu.VMEM(state, F32)]
        args += list(_ret_tables(tm))
    in_specs = [
        _row_spec(tm, d), _layer_spec(gains.shape, layer), _const_spec(w_in.shape),
        _layer_spec(w_conv.shape, layer), _layer_spec(b_conv.shape, layer), _row_spec(tm, tabs.shape[1]), hist_spec,
    ] + [_const_spec(a.shape) for a in args[7:]]
    out_specs = [_row_spec(tm, CONV_DIM), ulast_spec] + ret_specs + attn_specs
    out_shape = [jax.ShapeDtypeStruct((m, CONV_DIM), BF16), ulast_shape] + ret_shapes + attn_shapes
    return pl.pallas_call(
        functools.partial(_proj_body, dils=dils),
        grid=(n,),
        in_specs=in_specs,
        out_specs=out_specs,
        out_shape=out_shape,
        scratch_shapes=scratch,
        compiler_params=_params("arbitrary"),
        name="proj",
    )(*args)


def _head_norm(o):
    mu = jnp.mean(o, axis=-1, keepdims=True)
    var = jnp.mean(jnp.square(o - mu), axis=-1, keepdims=True)
    return (o - mu) * lax.rsqrt(var + EPS)


def _ret_log_decay():
    return jnp.log1p(-jnp.exp2(-5.0 - jnp.arange(RET_HEADS, dtype=F32)))


def _ret_tables(chunk):
    i = jnp.arange(chunk, dtype=F32)
    log_g = _ret_log_decay()
    diff = i[:, None] - i[None, :]
    decay = jnp.where(diff[None] >= 0, jnp.exp(jnp.maximum(diff, 0.0)[None] * log_g[:, None, None]), 0.0)
    k_w = jnp.exp((chunk - 1 - i)[:, None] * log_g[None, :])
    q_w = jnp.exp((i + 1)[:, None] * log_g[None, :])
    g_chunk = jnp.exp(chunk * log_g)
    qw = jnp.broadcast_to(q_w.T[:, :, None], (RET_HEADS, chunk, RET_DK))
    kw = jnp.broadcast_to(k_w.T[:, :, None], (RET_HEADS, chunk, RET_DK))
    gch = jnp.broadcast_to(g_chunk[:, None, None], (RET_HEADS, RET_DK, RET_DK))
    return decay, qw, kw, gch


def _ret_dec_body(r_ref, gd_ref, s0_ref, *rest):
    y_ref, sout_ref = rest[-2:]
    for b in range(r_ref.shape[0]):
        for hh in range(RET_HEADS):
            part = lambda j: r_ref[b:b + 1, j * 512 + hh * RET_DK:j * 512 + (hh + 1) * RET_DK]
            q, k, v = part(0), part(1), part(2)
            gd = gd_ref[hh]
            s0 = s0_ref[b, hh]
            wide = (RET_DK, RET_DK)
            q_col = jnp.transpose(jnp.broadcast_to(q * gd, wide))
            k_col = jnp.transpose(jnp.broadcast_to(k, wide))
            o_inter = jnp.sum(q_col * s0, axis=0, keepdims=True)
            o_intra = jnp.sum(q * k, axis=-1, keepdims=True) * v
            sout_ref[b, hh] = s0 * gd + k_col * v
            o = o_intra + o_inter
            y_ref[b:b + 1, hh * RET_DK:(hh + 1) * RET_DK] = _head_norm(o) * _silu(part(3))


def _retention_decode(r, gdec, state, layer, prev_out):
    b = r.shape[0]
    bs = DEC_BLOCK_SAMPLES
    while b % bs:
        bs -= 1
    st = pl.BlockSpec((None, bs, RET_HEADS, RET_DK, RET_DK), lambda i: (layer, i, 0, 0, 0))
    args = [r, gdec, state]
    in_specs = [_row_spec(bs, r.shape[1]), _const_spec(gdec.shape), st]
    aliases = {}
    if prev_out is not None:
        args.append(prev_out)
        in_specs.append(pl.BlockSpec(memory_space=pl.ANY))
        aliases = {3: 1}
    return pl.pallas_call(
        _ret_dec_body,
        grid=(b // bs,),
        in_specs=in_specs,
        out_specs=[_row_spec(bs, 512), st],
        out_shape=[jax.ShapeDtypeStruct((b, 512), F32), jax.ShapeDtypeStruct(state.shape, F32)],
        input_output_aliases=aliases,
        compiler_params=_params("parallel"),
        name="retention_decode",
    )(*args)


def _attn_body(cur_ref, prev_ref, out_ref, *, n_sub):
    n = pl.program_id(1)
    qb = ATTN_BLOCK
    ii = lax.broadcasted_iota(jnp.int32, (2 * qb, 2 * qb), 0) & (qb - 1)
    jj = lax.broadcasted_iota(jnp.int32, (2 * qb, 2 * qb), 1)
    band = jnp.logical_and(jj >= ii, jj <= ii + qb)
    first = jnp.logical_and(band, jnp.logical_or(jj >= qb, n > 0))
    lo = lax.broadcasted_iota(jnp.int32, (qb, LANES), 1) < HEAD_DIM
    for hp in range(ATTN_HEADS // 2):
        cols = lambda j: slice(j * ATTN_DIM + hp * LANES, j * ATTN_DIM + (hp + 1) * LANES)
        kk = jnp.concatenate([prev_ref[:, cols(1)], cur_ref[:, cols(1)]], axis=0).astype(BF16)
        vv = jnp.concatenate([prev_ref[:, cols(2)], cur_ref[:, cols(2)]], axis=0).astype(BF16)
        for b0 in range(0, n_sub, ATTN_GROUP):
            blocks = range(b0, min(b0 + ATTN_GROUP, n_sub))
            scores = []
            for b in blocks:
                q2 = cur_ref[b * qb:(b + 1) * qb, cols(0)] * ATTN_SCALE
                zero = jnp.zeros_like(q2)
                qs = jnp.concatenate([jnp.where(lo, q2, zero), jnp.where(lo, zero, q2)], axis=0).astype(BF16)
                scores.append(_dot_nt(qs, kk[b * qb:(b + 2) * qb]))
            probs = []
            for b, s in zip(blocks, scores):
                s = jnp.where(first if b == 0 else band, s, NEG)
                mx = jnp.max(s, axis=-1, keepdims=True)
                p = jnp.exp(s - mx)
                den = jnp.sum(p, axis=-1, keepdims=True)
                probs.append((p.astype(BF16), den, mx + jnp.log(den)))
            for b, (p, den, lse) in zip(blocks, probs):
                rows = slice(b * qb, (b + 1) * qb)
                o = _dot(p, vv[b * qb:(b + 2) * qb]) / den
                out_ref[rows, cols(0)] = jnp.where(lo, o[:qb], o[qb:])
                out_ref[rows, cols(1)] = jnp.where(lo, lse[:qb], lse[qb:])


def _attention(qkv):
    dil, length, width = qkv.shape
    step = min(ATTN_STEP, length)
    n_sub = step // ATTN_BLOCK
    cur = pl.BlockSpec((None, step, width), lambda r, n: (r, n, 0))
    prev = pl.BlockSpec((None, ATTN_BLOCK, width), lambda r, n: (r, jnp.maximum(n * n_sub - 1, 0), 0))
    return pl.pallas_call(
        functools.partial(_attn_body, n_sub=n_sub),
        grid=(dil, length // step),
        in_specs=[cur, prev],
        out_specs=pl.BlockSpec((None, step, 2 * ATTN_DIM), lambda r, n: (r, n, 0)),
        out_shape=jax.ShapeDtypeStruct((dil, length, 2 * ATTN_DIM), F32),
        compiler_params=_params("parallel", "arbitrary"),
        name="attention",
    )(qkv, qkv)


def _attn_dec_body(qkv_ref, kc_ref, tail_ref, new_ref, out_ref, cout_ref):
    del new_ref
    rows = BF16_ROWS
    head_of_lane = lax.broadcasted_iota(jnp.int32, (rows, ATTN_DIM), 1) // HEAD_DIM
    own = head_of_lane == lax.broadcasted_iota(jnp.int32, (rows, ATTN_DIM), 0)
    last = lax.broadcasted_iota(jnp.int32, (HEAD_DIM, LANES), 1) == LANES - 1
    wide = lambda a: jnp.broadcast_to(a, (rows, ATTN_DIM))
    pick = lambda a: jnp.sum(jnp.where(own, a, 0.0), axis=0, keepdims=True)
    for b in range(qkv_ref.shape[0]):
        q, kn, vn = (qkv_ref[b:b + 1, j * ATTN_DIM:(j + 1) * ATTN_DIM] for j in range(3))
        q_rows = jnp.where(own, wide(q), 0.0)
        s = _dot(q_rows.astype(BF16), kc_ref[b, 0]) * ATTN_SCALE
        s_new = jnp.sum(q_rows * wide(kn), axis=-1, keepdims=True) * ATTN_SCALE
        mx = jnp.maximum(jnp.max(s, axis=-1, keepdims=True), s_new)
        p = jnp.exp(s - mx)
        p_new = jnp.exp(s_new - mx)
        den = jnp.sum(p, axis=-1, keepdims=True) + p_new
        o = (_dot_nt(p.astype(BF16), kc_ref[b, 1]) + p_new * wide(vn)) / den
        out_ref[b:b + 1, 0:ATTN_DIM] = pick(o)
        out_ref[b:b + 1, ATTN_DIM:2 * ATTN_DIM] = pick(wide(mx + jnp.log(den)))
        new_rows = jnp.concatenate([kn, vn], axis=1)
        new_cols = jnp.transpose(jnp.broadcast_to(new_rows, (LANES, 2 * ATTN_DIM)))
        for kv in range(2):
            for hh in range(ATTN_HEADS):
                col = new_cols[kv * ATTN_DIM + hh * HEAD_DIM:kv * ATTN_DIM + (hh + 1) * HEAD_DIM, :]
                cout_ref[b, kv, hh] = jnp.where(last, col, pltpu.roll(tail_ref[b, kv, hh], LANES - 1, 1))


def _attention_decode(qkv, compact, cache_t, new_cache, layer):
    b = qkv.shape[0]
    window = cache_t.shape[-1]
    bs = DEC_BLOCK_SAMPLES
    while b % bs:
        bs -= 1
    kc = pl.BlockSpec((bs,) + compact.shape[1:], lambda i: (i, 0, 0, 0))
    tail = pl.BlockSpec((None, bs, 2, ATTN_HEADS, HEAD_DIM, LANES),
                        lambda i: (layer, i, 0, 0, 0, window // LANES - 1))
    out, cnew = pl.pallas_call(
        _attn_dec_body,
        grid=(b // bs,),
        in_specs=[_row_spec(bs, qkv.shape[1]), kc, tail, pl.BlockSpec(memory_space=pl.ANY)],
        out_specs=[_row_spec(bs, 2 * ATTN_DIM), tail],
        out_shape=[jax.ShapeDtypeStruct((b, 2 * ATTN_DIM), F32), jax.ShapeDtypeStruct(cache_t.shape, F32)],
        input_output_aliases={3: 1},
        compiler_params=_params("parallel"),
        name="attention_decode",
    )(qkv, compact, cache_t, new_cache)
    return out[None], cnew


def _from_streams(in_ref, lane0, scr_ref, dil):
    if dil == 1:
        return in_ref[0, :, lane0:lane0 + ATTN_DIM]
    n = in_ref.shape[1]
    if dil % 8 == 0:
        return jnp.swapaxes(in_ref[:, :, lane0:lane0 + ATTN_DIM], 0, 1).reshape(dil * n, ATTN_DIM)
    for r in range(dil):
        for s in range(2):
            scr_ref[s, pl.ds(r, n, stride=dil), :] = in_ref[r, :, lane0 + s * LANES:lane0 + (s + 1) * LANES]
    return jnp.concatenate([scr_ref[0], scr_ref[1]], axis=1)


def _merge_body(x_ref, cy_ref, ry_ref, a0_ref, a1_ref, a2_ref, g_ref,
                wg_ref, wc_ref, wr_ref, wa_ref, wo_ref, *rest, dils, layout):
    n_side_in = sum(n_in for n_in, _, _ in layout)
    n_side_out = sum(n_out for _, n_out, _ in layout)
    out_ref = rest[n_side_in]
    scratch = rest[n_side_in + 1 + n_side_out:]
    _run_side(layout, rest[:n_side_in], rest[n_side_in + 1:n_side_in + 1 + n_side_out])
    x = x_ref[...]
    h = _rms(x, g_ref[2:3, :]).astype(BF16)
    scr = scratch[0] if scratch else None
    d = x.shape[1]
    tm = x.shape[0]
    halves = (slice(0, tm // 2), slice(tm // 2, tm)) if tm % (2 * MXU_WIDTH) == 0 else (slice(0, tm),)
    gate = jnp.concatenate([_dot(h[r], wg_ref[:, 0:d]) for r in halves], axis=0)
    merged = jax.nn.sigmoid(gate) * _dot(cy_ref[...].astype(BF16), wc_ref[...])
    merged = merged + jax.nn.sigmoid(_dot(h, wg_ref[:, d:2 * d])) * _dot(ry_ref[...].astype(BF16), wr_ref[...])
    attn_gate = jax.nn.sigmoid(_dot(h, wg_ref[:, 2 * d:3 * d]))
    vals = []
    for j in range(6):
        ref = (a0_ref, a1_ref, a2_ref)[j // 2]
        vals.append(_from_streams(ref, (j % 2) * ATTN_DIM, None if scr is None else scr.at[j], dils[j // 2]))
    o0, l0, o1, l1, o2, l2 = vals
    mx = jnp.maximum(jnp.maximum(l0, l1), l2)
    e0, e1, e2 = jnp.exp(l0 - mx), jnp.exp(l1 - mx), jnp.exp(l2 - mx)
    den = e0 + e1 + e2
    attn_y = ((e0 / den) * o0 + (e1 / den) * o1 + (e2 / den) * o2).astype(BF16)
    merged = (merged + attn_gate * _dot(attn_y, wa_ref[...])).astype(BF16)
    for r in halves:
        out_ref[r, :] = x[r] + _rms(_dot(merged[r], wo_ref[...]), g_ref[3:4, :])


def _merge(x, cy, ry, attn, dils, gains, wg, wc, wr, wa, wo, layer, tm, side=()):
    m, d = x.shape
    a_specs = [pl.BlockSpec((dl, tm // dl, 2 * ATTN_DIM), lambda i: (0, i, 0)) for dl in dils]
    scratch = [pltpu.VMEM((6, 2, tm, LANES), F32)] if max(dils) > 1 else []
    args = [x, cy, ry, *attn, gains, wg, wc, wr, wa, wo]
    s_args, s_in, s_shape, s_out, aliases, layout = _side_operands(len(args), 1, side)
    outs = pl.pallas_call(
        functools.partial(_merge_body, dils=dils, layout=layout),
        grid=(m // tm,),
        in_specs=[_row_spec(tm, d), _row_spec(tm, CONV_DIM), _row_spec(tm, 512)] + a_specs
        + [_layer_spec(gains.shape, layer)] + [_const_spec(a.shape) for a in (wg, wc, wr, wa, wo)] + s_in,
        out_specs=[_row_spec(tm, d)] + s_out,
        out_shape=[jax.ShapeDtypeStruct((m, d), F32)] + s_shape,
        scratch_shapes=scratch,
        input_output_aliases=aliases,
        compiler_params=_params("arbitrary"),
        name="merge",
    )(*args, *s_args)
    return outs[0], outs[1:]


def _cos_sin(start, count, freq, split):
    if split is None or count % split:
        ang = (start + jnp.arange(count)).astype(F32)[:, None] * freq[None, :]
        return jnp.cos(ang), jnp.sin(ang)
    base = (start + split * jnp.arange(count // split)).astype(F32)[:, None, None] * freq
    off = jnp.arange(split).astype(F32)[None, :, None] * freq
    cb, sb, co, so = jnp.cos(base), jnp.sin(base), jnp.cos(off), jnp.sin(off)
    return (cb * co - sb * so).reshape(count, LANES), (sb * co + cb * so).reshape(count, LANES)


def _rotary_tables(start, count, split=None):
    lane = jnp.arange(LANES)
    half = RET_DK // 2
    freq = jnp.exp(-(lane % half).astype(F32) * (math.log(RET_THETA) / half))
    cos, sin = _cos_sin(start, count, freq, split)
    rc, rs = cos, jnp.where(lane < half, -sin, sin)
    dim = lane % HEAD_DIM
    half = ROT_DIM // 2
    freq = jnp.exp(-(dim % half).astype(F32) * (math.log(ROPE_THETA) / half))
    cos, sin = _cos_sin(start, count, freq, split)
    ac = jnp.where(dim < ROT_DIM, cos, 1.0)
    alo = jnp.where(dim < half, -sin, 0.0)
    ahi = jnp.where(jnp.logical_and(dim >= half, dim < ROT_DIM), sin, 0.0)
    return rc, rs, ac, alo, ahi


def _finish(x, ple, cy, ry, attn, dils, w, layer, tm, merge_side=(), ffn_side=()):
    x, merge_out = _merge(x, cy, ry, attn, dils, w['gains'], w['w_gate'], w['w_conv_out'], w['w_ret_out'],
                          w['w_attn_out'], w['w_o'], layer, tm, side=merge_side)
    x, ffn_out = _ffn(x, w['gains'], w['ffn2_gu'], w['ffn2_down'], layer, 4, 5, tm,
                      ple=(ple, w['w_ple_gate'], w['w_ple_proj']), side=ffn_side)
    return x, merge_out, ffn_out


W_FFN1 = ('ffn1_gu', 'ffn1_down')
W_EARLY = W_FFN1 + ('w_in',)
W_LATE = ('w_gate', 'w_conv_out', 'w_ret_out', 'w_attn_out', 'w_o', 'ffn2_gu', 'ffn2_down', 'w_ple_gate',
          'w_ple_proj')


def _prompt_layer(x, ple, tabs, caches_t, new_kv, raw, w, layer):
    depth = raw['w_in'].shape[0]
    s = x.shape[0]
    tm = min(ROW_TILE, s)
    steps = s // tm
    wide_dil = DILATIONS[2]
    todo = tuple(n for n in W_EARLY[2:] + W_LATE if n not in w)
    side = [_shift_job(caches_t[2], new_kv[2], None, layer, steps, 0, wide_dil)]
    if todo:
        side.append(_convert_job([raw[n] for n in todo], layer, steps))
    x, outs = _ffn(x, w['gains'], w['ffn1_gu'], w['ffn1_down'], layer, 0, 1, tm, side=side)
    wide, wide_c = outs[:2]
    w = dict(w, **dict(zip(todo, outs[2:])))
    hist = jnp.zeros((8, CONV_DIM), F32)
    outs = _proj(x, w['gains'], w['w_in'], w['w_conv'], w['b_conv'], tabs, hist, layer, tm, decode=False)
    cy, ulast, ry, ret_state = outs[:4]
    attn, kv = [], []
    for gi, (window, dil) in enumerate(ATTN_GROUPS):
        qkv = outs[4 + gi]
        attn.append(_attention(qkv))
        keep = min(window, s) // dil
        tail = jnp.swapaxes(qkv[:, s // dil - keep:, ATTN_DIM:], 0, 1)
        kv.append(tail.reshape(1, keep * dil, 2, ATTN_HEADS, HEAD_DIM))
    merge_side = [_shift_job(caches_t[gi], new_kv[gi], None, layer, steps, None, DILATIONS[gi]) for gi in (0, 1)]
    ffn_side = [_shift_job(caches_t[2], wide, wide_c, layer, steps, 1, wide_dil)]
    if layer + 1 < depth:
        merge_side.append(_convert_job([raw[n] for n in W_EARLY], layer + 1, steps))
        ffn_side.append(_convert_job([raw[n] for n in W_LATE], layer + 1, steps))
    x, m_out, f_out = _finish(x, ple, cy, ry, attn, DILATIONS, w, layer, tm, merge_side, ffn_side)
    w_next = dict(zip(W_EARLY, m_out[4:]), **dict(zip(W_LATE, f_out[2:])))
    new_kv = [m_out[0], m_out[2], f_out[0]]
    compact = [m_out[1], m_out[3], f_out[1]]
    return x, ulast[6:8][None], ret_state[None], kv, new_kv, compact, w, w_next


def _sample_layer(x, ple, tabs, gdec, conv_hist, ret_state, caches_t, prev_ret, new_kv, compact, w, layer):
    b = x.shape[0]
    x, _ = _ffn(x, w['gains'], w['ffn1_gu'], w['ffn1_down'], layer, 0, 1, b)
    outs = _proj(x, w['gains'], w['w_in'], w['w_conv'], w['b_conv'], tabs, conv_hist, layer, b, decode=True)
    cy, ulast, ret_in = outs[:3]
    ry, ret_new = _retention_decode(ret_in, gdec, ret_state, layer, prev_ret)
    attn, kv = [], []
    for gi in range(len(ATTN_GROUPS)):
        o_lse, cnew = _attention_decode(outs[3 + gi], compact[gi], caches_t[gi], new_kv[gi], layer)
        attn.append(o_lse)
        kv.append(cnew)
    x, _, _ = _finish(x, ple, cy, ry, attn, (1, 1, 1), w, layer, b)
    return x, jnp.swapaxes(ulast, 0, 1), ret_new, kv


def kernel(x_prompt, x_sample, state_conv, state_ret, cache_kv_w128, cache_kv_w512, cache_kv_w2048, p_prompt, p_sample, norm_gain, w_ffn1_gu, w_ffn1_down, w_in, w_conv, b_conv, w_conv_out, w_ret_out, w_attn_out, w_gate, w_o, w_ffn2_gu, w_ffn2_down, w_ple_gate, w_ple_proj):
    depth = norm_gain.shape[0]
    seq = x_prompt.shape[1]
    nb = x_sample.shape[0]
    assert x_prompt.shape[0] == 1 and x_sample.shape[1] == 1

    tabs_p = jnp.concatenate(_rotary_tables(0, seq, split=ROT_SPLIT), axis=1)
    tabs_s = jnp.broadcast_to(jnp.concatenate(_rotary_tables(PAST_LEN, 1), axis=1), (nb, 5 * LANES))
    gdec = jnp.broadcast_to(jnp.exp(_ret_log_decay())[:, None, None], (RET_HEADS, 1, RET_DK))
    caches_t = [jnp.transpose(c, (0, 1, 3, 4, 5, 2)) for c in (cache_kv_w128, cache_kv_w512, cache_kv_w2048)]
    raw = dict(ffn1_gu=w_ffn1_gu, ffn1_down=w_ffn1_down, w_in=w_in, w_conv_out=w_conv_out, w_ret_out=w_ret_out,
               w_attn_out=w_attn_out, w_gate=w_gate, w_o=w_o, ffn2_gu=w_ffn2_gu, ffn2_down=w_ffn2_down,
               w_ple_gate=w_ple_gate, w_ple_proj=w_ple_proj)
    shared = dict(gains=norm_gain, w_conv=w_conv, b_conv=b_conv.reshape(depth, 1, CONV_DIM))
    w_next = {n: raw[n][0].astype(BF16) for n in W_FFN1}
    ple_p = p_prompt[:, 0]
    ple_s = p_sample[:, :, 0]
    conv_hist = jnp.swapaxes(state_conv, 1, 2)

    yp, ys = x_prompt[0], x_sample[:, 0]
    conv_p, conv_s, ret_p = [], [], []
    kv_p = [[] for _ in ATTN_GROUPS]
    ret_s, kv_s = None, [None] * len(ATTN_GROUPS)
    for l in range(depth):
        yp, cp, rp, kp, kv_s, compact, w, w_next = _prompt_layer(yp, ple_p, tabs_p, caches_t, kv_s, raw,
                                                                 dict(shared, **w_next), l)
        ys, cs, ret_s, kv_s = _sample_layer(ys, ple_s, tabs_s, gdec, conv_hist, state_ret, caches_t, ret_s, kv_s,
                                            compact, w, l)
        conv_p.append(cp)
        conv_s.append(cs)
        ret_p.append(rp)
        for gi in range(len(ATTN_GROUPS)):
            kv_p[gi].append(kp[gi])
    back = lambda c: jnp.transpose(c, (0, 1, 5, 2, 3, 4))
    return (yp[None], ys[:, None], jnp.stack(conv_p), jnp.stack(conv_s), jnp.stack(ret_p), ret_s,
            jnp.stack(kv_p[0]), back(kv_s[0]), jnp.stack(kv_p[1]), back(kv_s[1]),
            jnp.stack(kv_p[2]), back(kv_s[2]))
```
